```python
import math
import jax, jax.numpy as jnp
from jax import lax
import numpy as np

D_MODEL = 2048
BATCH = 4
SEQ = 8192
DEPTH = 2

MEM_LEN = 256
MEM_HEADS = 4
MEM_HEAD_DIM = D_MODEL // 8
MEM_WIDTH = MEM_HEADS * MEM_HEAD_DIM
SSD_INNER = 3 * D_MODEL // 2
SSD_HEAD_DIM = 64
SSD_HEADS = SSD_INNER // SSD_HEAD_DIM
SSD_GROUPS = 8
SSD_HEADS_PER_GROUP = SSD_HEADS // SSD_GROUPS
SSD_STATE = 128
SSD_CONV = 5
SSD_CHUNK = 128
SSD_NORM_GROUPS = 8
SSD_CONV_CH = SSD_INNER + 2 * SSD_GROUPS * SSD_STATE
SSD_IN = SSD_INNER + SSD_CONV_CH + 2 * SSD_HEADS + MEM_WIDTH
DIL_CONFIGS = ((128, 1), (512, 4), (2048, 16))
DIL_GROUPS = len(DIL_CONFIGS)
DIL_HEADS = 8
DIL_HEAD_DIM = D_MODEL // 16
DIL_WIDTH = DIL_HEADS * DIL_HEAD_DIM
DIL_IN = 3 * DIL_GROUPS * DIL_WIDTH + MEM_WIDTH
REL_BUCKETS = 32
REL_MAX_DISTANCE = 1024
D_FF = 11 * D_MODEL // 4
EPS = 1e-6

kernel_name = 'hybrid_ssd_dilated_memory_encoder'


def rmsnorm(x, g):
    xf = x.astype(jnp.float32)
    y = xf * lax.rsqrt(jnp.mean(xf * xf, axis=-1, keepdims=True) + EPS)
    return (y * g.astype(jnp.float32)).astype(x.dtype)


def swiglu(x, w_in, w_out):
    gate, up = jnp.split(x @ w_in, 2, axis=-1)
    return (jax.nn.silu(gate) * up) @ w_out


def t5_bucket(rel):
    half = REL_BUCKETS // 2
    exact = half // 2
    n = jnp.abs(rel)
    far = exact + (jnp.log(jnp.maximum(n, 1).astype(jnp.float32) / exact)
                   / math.log(REL_MAX_DISTANCE / exact) * (half - exact)).astype(jnp.int32)
    far = jnp.minimum(far, half - 1)
    return jnp.where(rel > 0, half, 0) + jnp.where(n < exact, n, far)


def memory_attention(q, memn, w_kv, q_gain, k_gain):
    b, S = q.shape[:2]
    kv = (memn @ w_kv).reshape(b, memn.shape[1], 2, MEM_HEADS, MEM_HEAD_DIM)
    k = rmsnorm(kv[:, :, 0], k_gain)
    v = kv[:, :, 1]
    q = rmsnorm(q, q_gain)
    s = jnp.einsum('bshd,bmhd->bhsm', q, k, preferred_element_type=jnp.float32) * MEM_HEAD_DIM ** -0.5
    p = jax.nn.softmax(s, axis=-1)
    o = jnp.einsum('bhsm,bmhd->bshd', p.astype(v.dtype), v)
    return o.reshape(b, S, MEM_WIDTH)


def ssd_chunked(xh, dt, A, Bm, Cm):
    b, S, G, R, P = xh.shape
    N = Bm.shape[-1]
    Q = SSD_CHUNK
    c = S // Q
    f32 = jnp.float32
    X = (xh.astype(f32) * dt[..., None]).reshape(b, c, Q, G, R, P)
    a = (dt * A).reshape(b, c, Q, G, R).transpose(0, 1, 3, 4, 2)
    a_cs = lax.cumsum(a, axis=4)
    Bc = Bm.astype(f32).reshape(b, c, Q, G, N)
    Cc = Cm.astype(f32).reshape(b, c, Q, G, N)
    tril = jnp.tril(jnp.ones((Q, Q), dtype=bool))
    Lmat = jnp.exp(jnp.where(tril, a_cs[..., :, None] - a_cs[..., None, :], -jnp.inf))
    CB = jnp.einsum('bclgn,bcsgn->bcgls', Cc, Bc)
    y_diag = jnp.einsum('bcgrls,bcsgrp->bclgrp', CB[:, :, :, None] * Lmat, X)
    decay_in = jnp.exp(a_cs[..., -1:] - a_cs).transpose(0, 1, 4, 2, 3)
    states = jnp.einsum('bclgn,bclgrp->bcgrpn', Bc, X * decay_in[..., None])
    chunk_decay = jnp.exp(a_cs[..., -1])

    def step(h, inp):
        s_c, d_c = inp
        return d_c[..., None, None] * h + s_c, h

    _, prev = lax.scan(step, jnp.zeros((b, G, R, P, N), f32),
                       (jnp.moveaxis(states, 1, 0), jnp.moveaxis(chunk_decay, 1, 0)))
    prev = jnp.moveaxis(prev, 0, 1)
    decay_out = jnp.exp(a_cs).transpose(0, 1, 4, 2, 3)
    y_off = jnp.einsum('bclgn,bcgrpn->bclgrp', Cc, prev) * decay_out[..., None]
    return (y_diag + y_off).reshape(b, S, G, R, P).astype(xh.dtype)


def ssd_layer(h, memn, w_in, conv_w, conv_b, dt_bias, A_log, d_skip, norm_g, w_out,
              mem_w_kv, mem_q_gain, mem_k_gain):
    b, S, _ = h.shape
    G, R, P, N = SSD_GROUPS, SSD_HEADS_PER_GROUP, SSD_HEAD_DIM, SSD_STATE
    proj = h @ w_in
    z = proj[..., :SSD_INNER]
    xbc = proj[..., SSD_INNER:SSD_INNER + SSD_CONV_CH]
    dt_raw = proj[..., SSD_INNER + SSD_CONV_CH:SSD_INNER + SSD_CONV_CH + 2 * SSD_HEADS]
    q_mem = proj[..., SSD_IN - MEM_WIDTH:].reshape(b, S, MEM_HEADS, MEM_HEAD_DIM)
    pad = SSD_CONV // 2
    xbc = lax.conv_general_dilated(xbc, conv_w[:, None, :], window_strides=(1,),
                                   padding=[(pad, pad)], dimension_numbers=('NWC', 'WIO', 'NWC'),
                                   feature_group_count=SSD_CONV_CH)
    xbc = jax.nn.silu(xbc + conv_b)
    xs = xbc[..., :SSD_INNER].reshape(b, S, G, R, P)
    Bm = xbc[..., SSD_INNER:SSD_INNER + G * N].reshape(b, S, G, N)
    Cm = xbc[..., SSD_INNER + G * N:].reshape(b, S, G, N)
    dt = jax.nn.softplus((dt_raw.reshape(b, S, 2, SSD_HEADS) + dt_bias).astype(jnp.float32))
    dt = dt.reshape(b, S, 2, G, R)
    A = -jnp.exp(A_log.astype(jnp.float32)).reshape(2, G, R)
    flip = lambda t: jnp.flip(t, axis=1)
    y_f = ssd_chunked(xs, dt[:, :, 0], A[0], Bm, Cm)
    y_b = flip(ssd_chunked(flip(xs), flip(dt[:, :, 1]), A[1], flip(Bm), flip(Cm)))
    y = y_f + y_b + d_skip.reshape(G, R)[..., None] * xs
    y = y.reshape(b, S, SSD_INNER) * jax.nn.silu(z)
    y = rmsnorm(y.reshape(b, S, SSD_NORM_GROUPS, SSD_INNER // SSD_NORM_GROUPS),
                norm_g.reshape(SSD_NORM_GROUPS, -1)).reshape(b, S, SSD_INNER)
    o_mem = memory_attention(q_mem, memn, mem_w_kv, mem_q_gain, mem_k_gain)
    return jnp.concatenate([y, o_mem], axis=-1) @ w_out


def dilated_branch(q, k, v, table, window, dilation):
    b, S, h, d = q.shape
    r = dilation
    K = window // (2 * dilation)
    W = K
    L = S // r
    nb = -(-L // W)
    Lp = nb * W

    def sub(t, lo, hi):
        t = t.reshape(b, L, r, h, d)
        return jnp.pad(t, ((0, 0), (lo, hi), (0, 0), (0, 0), (0, 0)))

    def band(t):
        t = sub(t, W, Lp - L + W).reshape(b, nb + 2, W, r, h, d)
        return jnp.concatenate([t[:, :-2], t[:, 1:-1], t[:, 2:]], axis=2)

    qs = sub(q, 0, Lp - L).reshape(b, nb, W, r, h, d)
    kb, vb = band(k), band(v)
    s = jnp.einsum('bnqrhd,bnkrhd->brhnqk', qs, kb, preferred_element_type=jnp.float32) * d ** -0.5
    rel = jnp.arange(3 * W)[None, :] - W - jnp.arange(W)[:, None]
    kpos = (jnp.arange(nb)[:, None] - 1) * W + jnp.arange(3 * W)[None, :]
    mask = (jnp.abs(rel) <= K)[None] & ((kpos >= 0) & (kpos < L))[:, None, :]
    bias = jnp.transpose(table[t5_bucket(rel * r)], (2, 0, 1)).astype(jnp.float32)
    s = jnp.where(mask, s + bias[:, None], -jnp.inf)
    lse = jax.nn.logsumexp(s, axis=-1)
    p = jnp.exp(s - lse[..., None])
    o = jnp.einsum('brhnqk,bnkrhd->bnqrhd', p.astype(vb.dtype), vb)
    o = o.reshape(b, Lp, r, h, d)[:, :L].reshape(b, S, h, d)
    lse = jnp.transpose(lse, (0, 3, 4, 1, 2)).reshape(b, Lp, r, h)[:, :L].reshape(b, S, h)
    return o, lse


def dilated_layer(h, memn, w_in, q_gain, k_gain, w_out, rel_bias, mem_w_kv, mem_q_gain, mem_k_gain):
    b, S, _ = h.shape
    proj = h @ w_in
    qkv = proj[..., :3 * DIL_GROUPS * DIL_WIDTH].reshape(b, S, DIL_GROUPS, 3, DIL_HEADS, DIL_HEAD_DIM)
    q_mem = proj[..., 3 * DIL_GROUPS * DIL_WIDTH:].reshape(b, S, MEM_HEADS, MEM_HEAD_DIM)
    outs, lses = [], []
    for g, (window, dilation) in enumerate(DIL_CONFIGS):
        q = rmsnorm(qkv[:, :, g, 0], q_gain[g])
        k = rmsnorm(qkv[:, :, g, 1], k_gain[g])
        o, l = dilated_branch(q, k, qkv[:, :, g, 2], rel_bias[:, g * DIL_HEADS:(g + 1) * DIL_HEADS],
                              window, dilation)
        outs.append(o)
        lses.append(l)
    wts = jax.nn.softmax(jnp.stack(lses), axis=0)
    o = jnp.sum(wts[..., None].astype(outs[0].dtype) * jnp.stack(outs), axis=0).reshape(b, S, DIL_WIDTH)
    o_mem = memory_attention(q_mem, memn, mem_w_kv, mem_q_gain, mem_k_gain)
    return jnp.concatenate([o, o_mem], axis=-1) @ w_out


def setup_inputs(seed: int = 0) -> dict:
    key = jax.random.key(seed)
    ks = jax.random.split(key, 26)
    f32 = jnp.float32
    na = (DEPTH + 1) // 2
    nbl = DEPTH // 2
    nrm = lambda k, shape, fan_in: jax.random.normal(k, shape, f32) * fan_in ** -0.5
    gain = lambda k, shape: 1.0 + 0.02 * jax.random.normal(k, shape, f32)
    dt0 = jnp.exp(jax.random.uniform(ks[17], (na, 2, SSD_HEADS), f32)
                  * (math.log(0.1) - math.log(0.001)) + math.log(0.001))
    return {
        'x': jax.random.normal(ks[0], (BATCH, SEQ, D_MODEL), f32),
        'mem': jax.random.normal(ks[1], (BATCH, MEM_LEN, D_MODEL), f32),
        'rel_bias': 0.5 * jax.random.normal(ks[2], (REL_BUCKETS, DIL_GROUPS * DIL_HEADS), f32),
        'ffn_norm': gain(ks[3], (DEPTH, 2, D_MODEL)),
        'ffn_w_in': nrm(ks[4], (DEPTH, 2, D_MODEL, 2 * D_FF), D_MODEL),
        'ffn_w_out': nrm(ks[5], (DEPTH, 2, D_FF, D_MODEL), D_FF),
        'mix_norm': gain(ks[6], (DEPTH, D_MODEL)),
        'mem_norm': gain(ks[7], (DEPTH, D_MODEL)),
        'mem_w_kv': nrm(ks[8], (DEPTH, D_MODEL, 2 * MEM_WIDTH), D_MODEL),
        'mem_q_gain': gain(ks[9], (DEPTH, MEM_HEAD_DIM)),
        'mem_k_gain': gain(ks[10], (DEPTH, MEM_HEAD_DIM)),
        'ssd_w_in': nrm(ks[11], (na, D_MODEL, SSD_IN), D_MODEL),
        'ssd_conv_w': nrm(ks[12], (na, SSD_CONV, SSD_CONV_CH), SSD_CONV),
        'ssd_conv_b': 0.02 * jax.random.normal(ks[13], (na, SSD_CONV_CH), f32),
        'ssd_dt_bias': dt0 + jnp.log(-jnp.expm1(-dt0)),
        'ssd_A_log': jnp.log(jax.random.uniform(ks[14], (na, 2, SSD_HEADS), f32, 1.0, 16.0)),
        'ssd_D': gain(ks[15], (na, SSD_HEADS)),
        'ssd_norm': gain(ks[16], (na, SSD_INNER)),
        'ssd_w_out': nrm(ks[18], (na, SSD_INNER + MEM_WIDTH, D_MODEL), SSD_INNER + MEM_WIDTH),
        'dil_w_in': nrm(ks[19], (nbl, D_MODEL, DIL_IN), D_MODEL),
        'dil_q_gain': gain(ks[20], (nbl, DIL_GROUPS, DIL_HEAD_DIM)),
        'dil_k_gain': gain(ks[21], (nbl, DIL_GROUPS, DIL_HEAD_DIM)),
        'dil_w_out': nrm(ks[22], (nbl, DIL_WIDTH + MEM_WIDTH, D_MODEL), DIL_WIDTH + MEM_WIDTH),
    }


def reference(x, mem, rel_bias, ffn_norm, ffn_w_in, ffn_w_out, mix_norm, mem_norm, mem_w_kv,
              mem_q_gain, mem_k_gain, ssd_w_in, ssd_conv_w, ssd_conv_b, ssd_dt_bias, ssd_A_log,
              ssd_D, ssd_norm, ssd_w_out, dil_w_in, dil_q_gain, dil_k_gain, dil_w_out):
    for i in range(DEPTH):
        x = x + 0.5 * swiglu(rmsnorm(x, ffn_norm[i, 0]), ffn_w_in[i, 0], ffn_w_out[i, 0])
        h = rmsnorm(x, mix_norm[i])
        memn = rmsnorm(mem, mem_norm[i])
        j = i // 2
        if i % 2 == 0:
            y = ssd_layer(h, memn, ssd_w_in[j], ssd_conv_w[j], ssd_conv_b[j], ssd_dt_bias[j],
                          ssd_A_log[j], ssd_D[j], ssd_norm[j], ssd_w_out[j],
                          mem_w_kv[i], mem_q_gain[i], mem_k_gain[i])
        else:
            y = dilated_layer(h, memn, dil_w_in[j], dil_q_gain[j], dil_k_gain[j], dil_w_out[j],
                              rel_bias, mem_w_kv[i], mem_q_gain[i], mem_k_gain[i])
        x = x + y
        x = x + 0.5 * swiglu(rmsnorm(x, ffn_norm[i, 1]), ffn_w_in[i, 1], ffn_w_out[i, 1])
    return x
```

```python
import functools
import math

import jax
import jax.numpy as jnp
import numpy as np
from jax import lax
from jax.experimental import pallas as pl
from jax.experimental.pallas import tpu as pltpu

F32 = jnp.float32
BF16 = jnp.bfloat16
EPS = 1e-6

MEM_HEADS = 4
SSD_HEAD_DIM = 64
SSD_GROUPS = 8
SSD_HEADS_PER_GROUP = 6
SSD_STATE = 128
SSD_CONV = 5
SSD_CHUNK = 128
DIL_CONFIGS = ((128, 1), (512, 4), (2048, 16))
DIL_HEADS = 8
DIL_HEAD_DIM = 128
DIL_HALF = 64
DIL_SEGS = 16
REL_BUCKETS = 32
REL_MAX_DISTANCE = 1024

LANES = 128
VMEM_LIMIT_BYTES = 56 * 1024 * 1024
GROUP_LANES = 16


def _cparams(*sem):
    return pltpu.CompilerParams(dimension_semantics=sem, vmem_limit_bytes=VMEM_LIMIT_BYTES)


def _rms(x, gain):
    ms = jnp.mean(x * x, axis=-1, keepdims=True)
    return x * lax.rsqrt(ms + EPS) * gain


def _dot(a, b):
    return jnp.dot(a, b, preferred_element_type=F32)


def _dot_nt(a, b):
    return lax.dot_general(a, b, (((1,), (1,)), ((), ())), preferred_element_type=F32)


def _rmsnorm_kernel(x_ref, g_ref, o_ref):
    o_ref[...] = _rms(x_ref[...], g_ref[...]).astype(o_ref.dtype)


def rmsnorm_bf16(x2, gain, tm):
    T, D = x2.shape
    return pl.pallas_call(
        _rmsnorm_kernel,
        grid=(T // tm,),
        in_specs=[pl.BlockSpec((tm, D), lambda i: (i, 0)),
                  pl.BlockSpec((1, D), lambda i: (0, 0))],
        out_specs=pl.BlockSpec((tm, D), lambda i: (i, 0)),
        out_shape=jax.ShapeDtypeStruct((T, D), BF16),
        compiler_params=_cparams("parallel"),
        name="rmsnorm",
    )(x2, gain.reshape(1, D))


def _ffn_kernel(x_ref, g_ref, wg_ref, wu_ref, wo_ref, o_ref, h_ref, acc_ref):
    j = pl.program_id(1)

    @pl.when(j == 0)
    def _():
        h_ref[...] = _rms(x_ref[...], g_ref[...]).astype(BF16)
        acc_ref[...] = jnp.zeros_like(acc_ref)

    h = h_ref[...]
    gate = _dot(h, wg_ref[...])
    up = _dot(h, wu_ref[...])
    a = (jax.nn.silu(gate) * up).astype(BF16)
    acc_ref[...] += _dot(a, wo_ref[...])

    @pl.when(j == pl.num_programs(1) - 1)
    def _():
        o_ref[...] = x_ref[...] + 0.5 * acc_ref[...]


def ffn(x2, gain, w_in, w_out, tm, tf):
    T, D = x2.shape
    F = w_out.shape[0]
    nf = F // tf
    return pl.pallas_call(
        _ffn_kernel,
        grid=(T // tm, nf),
        in_specs=[pl.BlockSpec((tm, D), lambda i, j: (i, 0)),
                  pl.BlockSpec((1, D), lambda i, j: (0, 0)),
                  pl.BlockSpec((D, tf), lambda i, j: (0, j)),
                  pl.BlockSpec((D, tf), lambda i, j: (0, j + nf)),
                  pl.BlockSpec((tf, D), lambda i, j: (j, 0))],
        out_specs=pl.BlockSpec((tm, D), lambda i, j: (i, 0)),
        out_shape=jax.ShapeDtypeStruct((T, D), F32),
        scratch_shapes=[pltpu.VMEM((tm, D), BF16), pltpu.VMEM((tm, D), F32)],
        compiler_params=_cparams("parallel", "arbitrary"),
        name="ffn",
    )(x2, gain.reshape(1, D), w_in, w_in, w_out)


def _proj_kernel(h_ref, w_ref, g_ref, o_ref, *scratch, norm_width, norm_tiles, r):
    tm, tn = h_ref.shape[1], w_ref.shape[1]
    res = _dot(h_ref[0], w_ref[...])
    if norm_width == 0 and r == 1:
        o_ref[0, 0] = res.astype(o_ref.dtype)
        return
    acc_ref, = scratch
    rows_per = tm // r
    if r > 1:
        assert norm_width in (0, LANES)
        for c in range(tn // LANES):
            acc_ref[c] = res[:, c * LANES:(c + 1) * LANES]
    else:
        acc_ref[...] = res

    def emit(normed):
        if r == 1 and not normed:
            o_ref[0, 0] = acc_ref[...].astype(o_ref.dtype)
            return
        if r == 1:
            for c in range(tn // norm_width):
                cols = slice(c * norm_width, (c + 1) * norm_width)
                o_ref[0, 0, :, cols] = _rms(acc_ref[:, cols], g_ref[0, :, cols]).astype(o_ref.dtype)
            return
        for m in range(r):
            rows = pl.ds(m, rows_per, stride=r)
            for c in range(tn // LANES):
                cols = slice(c * LANES, (c + 1) * LANES)
                blk = acc_ref[c, rows, :]
                if normed:
                    blk = _rms(blk, g_ref[0, :, cols])
                o_ref[0, m, :, cols] = blk.astype(o_ref.dtype)

    if norm_width == 0:
        emit(False)
    elif norm_tiles is None:
        emit(True)
    else:
        j = pl.program_id(2)
        pl.when(j < norm_tiles)(lambda: emit(True))
        pl.when(j >= norm_tiles)(lambda: emit(False))


def proj(h3, w, col0, ncols, gains, *, out_dtype, tm, tn, norm_width=0, norm_tiles=None, r=1):
    nb, S, K = h3.shape
    assert col0 % tn == 0 and ncols % tn == 0 and S % tm == 0 and tm % (8 * r) == 0
    j0 = col0 // tn
    nj = ncols // tn
    if gains is None:
        gains = jnp.ones((nj, 1, tn), F32)
    if norm_width == 0 and r == 1:
        scratch = []
    elif r == 1:
        scratch = [pltpu.VMEM((tm, tn), F32)]
    else:
        scratch = [pltpu.VMEM((tn // LANES, tm, LANES), F32)]
    kern = functools.partial(_proj_kernel, norm_width=norm_width, norm_tiles=norm_tiles, r=r)
    return pl.pallas_call(
        kern,
        grid=(nb, S // tm, nj),
        in_specs=[pl.BlockSpec((1, tm, K), lambda b, i, j: (b, i, 0)),
                  pl.BlockSpec((K, tn), lambda b, i, j: (0, j0 + j)),
                  pl.BlockSpec((1, 1, tn), lambda b, i, j: (j, 0, 0))],
        out_specs=pl.BlockSpec((1, r, tm // r, tn), lambda b, i, j: (b, 0, i, j)),
        out_shape=jax.ShapeDtypeStruct((nb, r, S // r, ncols), out_dtype),
        scratch_shapes=scratch,
        compiler_params=_cparams("parallel", "parallel", "arbitrary"),
        name="proj",
    )(h3, w, gains)


def _outproj_kernel(x_ref, a1_ref, a2_ref, w1_ref, w2_ref, o_ref):
    o_ref[...] = x_ref[...] + _dot(a1_ref[...], w1_ref[...]) + _dot(a2_ref[...], w2_ref[...])


def outproj(x2, a1, a2, w1, w2, tm, tn):
    T, D = x2.shape
    k1, k2 = a1.shape[1], a2.shape[1]
    return pl.pallas_call(
        _outproj_kernel,
        grid=(T // tm, D // tn),
        in_specs=[pl.BlockSpec((tm, tn), lambda i, j: (i, j)),
                  pl.BlockSpec((tm, k1), lambda i, j: (i, 0)),
                  pl.BlockSpec((tm, k2), lambda i, j: (i, 0)),
                  pl.BlockSpec((k1, tn), lambda i, j: (0, j)),
                  pl.BlockSpec((k2, tn), lambda i, j: (0, j))],
        out_specs=pl.BlockSpec((tm, tn), lambda i, j: (i, j)),
        out_shape=jax.ShapeDtypeStruct((T, D), F32),
        compiler_params=_cparams("parallel", "arbitrary"),
        name="outproj",
    )(x2, a1, a2, w1, w2)


def _memkv_kernel(mem_ref, g_ref, w_ref, kg_ref, k_ref, v_ref):
    mw = k_ref.shape[2]
    hd = kg_ref.shape[1]
    memn = _rms(mem_ref[0], g_ref[...]).astype(BF16)
    kv = _dot(memn, w_ref[...])
    for hh in range(mw // hd):
        cols = slice(hh * hd, (hh + 1) * hd)
        k_ref[0, :, cols] = _rms(kv[:, cols], kg_ref[...]).astype(BF16)
    v_ref[0] = kv[:, mw:].astype(BF16)


def mem_kv(mem, mem_gain, w_kv, k_gain):
    nb, M, D = mem.shape
    mw = w_kv.shape[1] // 2
    hd = k_gain.shape[0]
    out = jax.ShapeDtypeStruct((nb, M, mw), BF16)
    return pl.pallas_call(
        _memkv_kernel,
        grid=(nb,),
        in_specs=[pl.BlockSpec((1, M, D), lambda b: (b, 0, 0)),
                  pl.BlockSpec((1, D), lambda b: (0, 0)),
                  pl.BlockSpec((D, 2 * mw), lambda b: (0, 0)),
                  pl.BlockSpec((1, hd), lambda b: (0, 0))],
        out_specs=[pl.BlockSpec((1, M, mw), lambda b: (b, 0, 0))] * 2,
        out_shape=[out, out],
        compiler_params=_cparams("parallel"),
        name="mem_kv",
    )(mem, mem_gain.reshape(1, D), w_kv, k_gain.reshape(1, hd))


def _memattn_kernel(q_ref, k_ref, v_ref, o_ref, *, heads):
    hd = q_ref.shape[2] // heads
    scale = hd ** -0.5
    for hh in range(heads):
        cols = slice(hh * hd, (hh + 1) * hd)
        s = _dot_nt(q_ref[0, :, cols], k_ref[0, :, cols]) * scale
        e = jnp.exp(s - jnp.max(s, axis=-1, keepdims=True))
        p = e / jnp.sum(e, axis=-1, keepdims=True)
        o_ref[0, :, cols] = _dot(p.astype(BF16), v_ref[0, :, cols]).astype(o_ref.dtype)


def mem_attention(q, k, v, tq):
    nb, S, mw = q.shape
    M = k.shape[1]
    return pl.pallas_call(
        functools.partial(_memattn_kernel, heads=MEM_HEADS),
        grid=(nb, S // tq),
        in_specs=[pl.BlockSpec((1, tq, mw), lambda b, i: (b, i, 0)),
                  pl.BlockSpec((1, M, mw), lambda b, i: (b, 0, 0)),
                  pl.BlockSpec((1, M, mw), lambda b, i: (b, 0, 0))],
        out_specs=pl.BlockSpec((1, tq, mw), lambda b, i: (b, i, 0)),
        out_shape=jax.ShapeDtypeStruct((nb, S, mw), BF16),
        compiler_params=_cparams("parallel", "parallel"),
        name="mem_attn",
    )(q, k, v)


def _conv_kernel(prev_ref, cur_ref, next_ref, w_ref, b_ref, o_ref, ext_ref):
    i = pl.program_id(1)
    ts = cur_ref.shape[1]
    halo = prev_ref.shape[1]
    pad = SSD_CONV // 2
    ext_ref[0:halo] = jnp.where(i > 0, prev_ref[0], 0.0)
    ext_ref[halo:halo + ts] = cur_ref[0]
    ext_ref[halo + ts:] = jnp.where(i < pl.num_programs(1) - 1, next_ref[0], 0.0)
    acc = b_ref[...] + w_ref[0:1, :] * ext_ref[halo - pad:halo - pad + ts, :]
    for k in range(1, SSD_CONV):
        acc = acc + w_ref[k:k + 1, :] * ext_ref[halo - pad + k:halo - pad + k + ts, :]
    o_ref[0] = jax.nn.silu(acc)


def ssd_conv(p1, col0, conv_w, conv_b, ts, tc):
    nb, S, _ = p1.shape
    C = conv_w.shape[1]
    halo = 8
    c0 = col0 // tc
    nsb = S // halo
    return pl.pallas_call(
        _conv_kernel,
        grid=(nb, S // ts, C // tc),
        in_specs=[pl.BlockSpec((1, halo, tc), lambda b, i, j: (b, jnp.maximum(i * (ts // halo) - 1, 0), c0 + j)),
                  pl.BlockSpec((1, ts, tc), lambda b, i, j: (b, i, c0 + j)),
                  pl.BlockSpec((1, halo, tc), lambda b, i, j: (b, jnp.minimum((i + 1) * (ts // halo), nsb - 1), c0 + j)),
                  pl.BlockSpec((SSD_CONV, tc), lambda b, i, j: (0, j)),
                  pl.BlockSpec((1, tc), lambda b, i, j: (0, j))],
        out_specs=pl.BlockSpec((1, ts, tc), lambda b, i, j: (b, i, j)),
        out_shape=jax.ShapeDtypeStruct((nb, S, C), F32),
        scratch_shapes=[pltpu.VMEM((ts + 2 * halo, tc), F32)],
        compiler_params=_cparams("parallel", "parallel", "parallel"),
        name="ssd_conv",
    )(p1, p1, p1, conv_w, conv_b.reshape(1, C))


def _ssd_prep_kernel(dtr_ref, bias_ref, alog_ref, cs_ref, eo_ref, wn_ref, csT_ref, dtT_ref, cdT_ref):
    Q = dtr_ref.shape[1]
    x = dtr_ref[0] + bias_ref[...]
    dt = jnp.maximum(x, 0.0) + jnp.log1p(jnp.exp(-jnp.abs(x)))
    a = dt * (-jnp.exp(alog_ref[...]))
    ii = lax.broadcasted_iota(jnp.int32, (Q, Q), 0)
    jj = lax.broadcasted_iota(jnp.int32, (Q, Q), 1)
    hi = lax.Precision.HIGHEST
    cs_fwd = jnp.dot((jj <= ii).astype(F32), a, precision=hi, preferred_element_type=F32)
    cs_bwd = jnp.dot((jj >= ii).astype(F32), a, precision=hi, preferred_element_type=F32)
    lane = lax.broadcasted_iota(jnp.int32, (Q, LANES), 1)
    is_bwd = (lane % GROUP_LANES) >= GROUP_LANES // 2
    cs = jnp.where(is_bwd, cs_bwd, cs_fwd)
    last = jnp.where(is_bwd[0:1], cs[0:1], cs[Q - 1:Q])
    cs_ref[0] = cs
    eo_ref[0] = jnp.exp(cs)
    wn_ref[0] = jnp.exp(last - cs) * dt
    csT_ref[0, 0] = cs.T
    dtT_ref[0, 0] = dt.T
    cdT_ref[0, 0] = jnp.broadcast_to(jnp.exp(last), (Q, LANES)).T


def ssd_prep(dtr, bias_l, alog_l):
    nb, S, _ = dtr.shape
    Q = SSD_CHUNK
    nc = S // Q
    nat = jax.ShapeDtypeStruct((nb, S, LANES), F32)
    tr = jax.ShapeDtypeStruct((nb, nc, LANES, Q), F32)
    nat_spec = pl.BlockSpec((1, Q, LANES), lambda b, c: (b, c, 0))
    tr_spec = pl.BlockSpec((1, 1, LANES, Q), lambda b, c: (b, c, 0, 0))
    vec_spec = pl.BlockSpec((1, LANES), lambda b, c: (0, 0))
    return pl.pallas_call(
        _ssd_prep_kernel,
        grid=(nb, nc),
        in_specs=[nat_spec, vec_spec, vec_spec],
        out_specs=[nat_spec] * 3 + [tr_spec] * 3,
        out_shape=[nat] * 3 + [tr] * 3,
        compiler_params=_cparams("parallel", "parallel"),
        name="ssd_prep",
    )(dtr, bias_l, alog_l)


def _ssd_direction(d, g, xs_ref, b_ref, c_ref, cs_ref, eo_ref, wn_ref, csT_ref, dtT_ref, cdT_ref, st_ref):
    Q = xs_ref.shape[1]
    P = SSD_HEAD_DIM
    xs = xs_ref[0]
    b_bf = b_ref[0].astype(BF16)
    c_bf = c_ref[0].astype(BF16)
    cb = _dot_nt(c_bf, b_bf)
    bt_bf = b_ref[0].T.astype(BF16)
    shift = (LANES - GROUP_LANES * g) % LANES
    cs = pltpu.roll(cs_ref[0], shift, 1)
    eo = pltpu.roll(eo_ref[0], shift, 1)
    wn = pltpu.roll(wn_ref[0], shift, 1)
    rows = pl.ds(pl.multiple_of(GROUP_LANES * g, GROUP_LANES), GROUP_LANES)
    csT = csT_ref[0, 0, rows, :]
    dtT = dtT_ref[0, 0, rows, :]
    cdT = cdT_ref[0, 0, rows, :]
    ii = lax.broadcasted_iota(jnp.int32, (Q, Q), 0)
    jj = lax.broadcasted_iota(jnp.int32, (Q, Q), 1)
    causal = (jj <= ii) if d == 0 else (jj >= ii)
    lo = lax.broadcasted_iota(jnp.int32, (Q, LANES), 1) < P
    st = st_ref[d]
    y_off = _dot(c_bf, st.astype(BF16))
    ys, xws, cds = [], [], []
    for p in range(SSD_HEADS_PER_GROUP // 2):
        k0 = d * (GROUP_LANES // 2) + 2 * p
        k1 = k0 + 1
        cols = slice(p * LANES, (p + 1) * LANES)
        ms = []
        for k in (k0, k1):
            decay = jnp.exp(jnp.where(causal, cs[:, k:k + 1] - csT[k:k + 1, :], -jnp.inf))
            ms.append((cb * decay * dtT[k:k + 1, :]).astype(BF16))
        xp = xs[:, cols]
        rhs = jnp.concatenate([jnp.where(lo, xp, 0.0), jnp.where(lo, 0.0, xp)], axis=0).astype(BF16)
        y_diag = _dot(jnp.concatenate(ms, axis=1), rhs)
        ys.append(y_diag + y_off[:, cols] * jnp.where(lo, eo[:, k0:k0 + 1], eo[:, k1:k1 + 1]))
        xws.append((xp * jnp.where(lo, wn[:, k0:k0 + 1], wn[:, k1:k1 + 1])).astype(BF16))
        cds.append(jnp.where(lo[0:1], cdT[k0:k0 + 1, :], cdT[k1:k1 + 1, :]))
    st_ref[d] = jnp.concatenate(cds, axis=1) * st + _dot(bt_bf, jnp.concatenate(xws, axis=1))
    return jnp.concatenate(ys, axis=1)


def _ssd_scan_kernel(xsf, xsb, bf, bb, cf, cb, zf, zb, csf, csb, eof, eob, wnf, wnb,
                     csTf, csTb, dtTf, dtTb, cdTf, cdTb, dskip_ref, ng_ref, y_ref, st_ref, ysum_ref):
    g = pl.program_id(1)
    c = pl.program_id(2)
    nc = pl.num_programs(2)
    Q = xsf.shape[1]

    @pl.when(c == 0)
    def _():
        st_ref[...] = jnp.zeros_like(st_ref)

    y_f = _ssd_direction(0, g, xsf, bf, cf, csf, eof, wnf, csTf, dtTf, cdTf, st_ref)
    y_b = _ssd_direction(1, g, xsb, bb, cb, csb, eob, wnb, csTb, dtTb, cdTb, st_ref)
    first = c < nc // 2

    for y_dir, chunk, xs_ref, z_ref in ((y_f, c, xsf, zf), (y_b, nc - 1 - c, xsb, zb)):
        rows = pl.ds(pl.multiple_of(chunk * Q, Q), Q)

        @pl.when(first)
        def _():
            ysum_ref[rows, :] = y_dir

        @pl.when(jnp.logical_not(first))
        def _():
            tot = ysum_ref[rows, :] + y_dir + dskip_ref[...] * xs_ref[0]
            gated = tot * jax.nn.silu(z_ref[0])
            y_ref[0, rows, :] = _rms(gated, ng_ref[...]).astype(y_ref.dtype)


def ssd_scan(xc, p1, nat, tr, dskip, norm_g):
    nb, S, _ = xc.shape
    Q = SSD_CHUNK
    nc = S // Q
    assert nc % 2 == 0
    G, N = SSD_GROUPS, SSD_STATE
    gw = SSD_HEADS_PER_GROUP * SSD_HEAD_DIM
    inner = G * gw
    b0 = inner // N
    c0 = b0 + G
    fwd = lambda f: (lambda b, g, c: f(b, g, c))
    bwd = lambda f: (lambda b, g, c: f(b, g, nc - 1 - c))

    def both(shape, f):
        return [pl.BlockSpec(shape, fwd(f)), pl.BlockSpec(shape, bwd(f))]

    in_specs = (both((1, Q, gw), lambda b, g, c: (b, c, g))
                + both((1, Q, N), lambda b, g, c: (b, c, b0 + g))
                + both((1, Q, N), lambda b, g, c: (b, c, c0 + g))
                + both((1, Q, gw), lambda b, g, c: (b, c, g))
                + both((1, Q, LANES), lambda b, g, c: (b, c, 0)) * 1
                + both((1, Q, LANES), lambda b, g, c: (b, c, 0))
                + both((1, Q, LANES), lambda b, g, c: (b, c, 0))
                + both((1, 1, LANES, Q), lambda b, g, c: (b, c, 0, 0))
                + both((1, 1, LANES, Q), lambda b, g, c: (b, c, 0, 0))
                + both((1, 1, LANES, Q), lambda b, g, c: (b, c, 0, 0))
                + [pl.BlockSpec((1, gw), lambda b, g, c: (0, g)),
                   pl.BlockSpec((1, gw), lambda b, g, c: (0, g))])
    cs, eo, wn = nat
    csT, dtT, cdT = tr
    return pl.pallas_call(
        _ssd_scan_kernel,
        grid=(nb, G, nc),
        in_specs=in_specs,
        out_specs=pl.BlockSpec((1, S, gw), lambda b, g, c: (b, 0, g)),
        out_shape=jax.ShapeDtypeStruct((nb, S, inner), BF16),
        scratch_shapes=[pltpu.VMEM((2, N, gw), F32), pltpu.VMEM((S, gw), F32)],
        compiler_params=_cparams("parallel", "parallel", "arbitrary"),
        name="ssd_scan",
    )(xc, xc, xc, xc, xc, xc, p1, p1, cs, cs, eo, eo, wn, wn, csT, csT, dtT, dtT, cdT, cdT,
      dskip.reshape(1, inner), norm_g.reshape(1, inner))


def _t5_bucket_np(rel):
    half = REL_BUCKETS // 2
    exact = half // 2
    n = np.abs(rel)
    far = exact + (np.log(np.maximum(n, 1).astype(np.float32) / np.float32(exact))
                   / np.float32(math.log(REL_MAX_DISTANCE / exact)) * np.float32(half - exact)).astype(np.int32)
    far = np.minimum(far, half - 1)
    return np.where(rel > 0, half, 0) + np.where(n < exact, n, far)


def _dil_bucket_maps(tq):
    qi = np.arange(tq)[:, None]
    ki = np.arange(2 * tq)[None, :]
    rel = ki - DIL_HALF - qi
    maps = []
    for _, dilation in DIL_CONFIGS:
        maps.append(np.where(np.abs(rel) <= DIL_HALF, _t5_bucket_np(rel * dilation), -1))
    return np.stack(maps).astype(np.int32)


def _dil_kernel(tbl_ref, bm_ref, *refs, seq, seg, tq):
    ngroups = len(DIL_CONFIGS)
    in_refs = refs[:7 * ngroups]
    o_ref = refs[7 * ngroups]
    bias_ref, m_ref, l_ref, a_ref, kext_ref, vext_ref = refs[7 * ngroups + 1:]
    h = pl.program_id(1)
    t = pl.program_id(2)
    scale = DIL_HEAD_DIM ** -0.5

    @pl.when(t == 0)
    def _():
        m_ref[...] = jnp.full_like(m_ref, -jnp.inf)
        l_ref[...] = jnp.zeros_like(l_ref)
        a_ref[...] = jnp.zeros_like(a_ref)
        for gi in range(ngroups):
            bm = bm_ref[gi]
            bias = jnp.zeros(bm.shape, F32)
            for u in range(REL_BUCKETS):
                bias = jnp.where(bm == u, tbl_ref[u, gi * DIL_HEADS + h], bias)
            bias_ref[gi] = jnp.where(bm < 0, -jnp.inf, bias)

    kcol = lax.broadcasted_iota(jnp.int32, (tq, 2 * tq), 1)
    for gi, (_, r) in enumerate(DIL_CONFIGS):
        q_ref, kp, kc, kn, vp, vc, vn = in_refs[7 * gi:7 * gi + 7]
        sub_len = seq // r
        segs_per_sub = sub_len // seg
        m_res = t // segs_per_sub
        j0 = (t % segs_per_sub) * seg
        for ext, prev, cur, nxt in ((kext_ref, kp, kc, kn), (vext_ref, vp, vc, vn)):
            ext[0:DIL_HALF] = prev[0]
            ext[DIL_HALF:DIL_HALF + seg] = cur[0]
            ext[DIL_HALF + seg:] = nxt[0]
        for jb in range(seg // tq):
            q = q_ref[0, jb * tq:(jb + 1) * tq, :]
            kw = kext_ref[jb * tq:jb * tq + 2 * tq, :]
            vw = vext_ref[jb * tq:jb * tq + 2 * tq, :]
            s = _dot_nt(q, kw) * scale + bias_ref[gi]
            kj = kcol + (j0 + jb * tq - DIL_HALF)
            s = jnp.where((kj >= 0) & (kj < sub_len), s, -jnp.inf)
            mb = jnp.max(s, axis=-1, keepdims=True)
            e = jnp.exp(s - mb)
            lb = jnp.sum(e, axis=-1, keepdims=True)
            acc = _dot(e.astype(BF16), vw)
            start = (j0 + jb * tq) * r + m_res
            rows = pl.ds(start, tq, stride=r) if r > 1 else pl.ds(start, tq)
            m_old = m_ref[rows, :]
            m_new = jnp.maximum(m_old, mb)
            alpha = jnp.exp(m_old - m_new)
            beta = jnp.exp(mb - m_new)
            l_new = alpha * l_ref[rows, :] + beta * lb
            a_new = alpha * a_ref[rows, :] + beta * acc
            m_ref[rows, :] = m_new
            l_ref[rows, :] = l_new
            a_ref[rows, :] = a_new

    @pl.when(t == pl.num_programs(2) - 1)
    def _():
        o_ref[0] = (a_ref[...] / l_ref[...]).astype(o_ref.dtype)


def dilated_attention(qkvs, rel_bias):
    nb, S, _ = qkvs[0].shape
    hd = DIL_HEAD_DIM
    tq = 2 * DIL_HALF
    seg = S // DIL_SEGS
    assert seg % tq == 0 and all(w == 2 * DIL_HALF * r and (S // r) % seg == 0 for w, r in DIL_CONFIGS)
    nhb = seg // DIL_HALF
    last_hb = S // DIL_HALF - 1
    in_specs = [pl.BlockSpec(memory_space=pltpu.SMEM),
                pl.BlockSpec((len(DIL_CONFIGS), tq, 2 * tq), lambda b, h, t: (0, 0, 0))]
    args = [rel_bias, jnp.asarray(_dil_bucket_maps(tq))]
    for g in range(len(DIL_CONFIGS)):
        in_specs.append(pl.BlockSpec((1, seg, hd), lambda b, h, t: (b, t, h)))
        for part in (1, 2):
            off = part * DIL_HEADS
            in_specs += [
                pl.BlockSpec((1, DIL_HALF, hd), lambda b, h, t, off=off: (b, jnp.maximum(t * nhb - 1, 0), off + h)),
                pl.BlockSpec((1, seg, hd), lambda b, h, t, off=off: (b, t, off + h)),
                pl.BlockSpec((1, DIL_HALF, hd), lambda b, h, t, off=off: (b, jnp.minimum((t + 1) * nhb, last_hb), off + h)),
            ]
        args += [qkvs[g]] * 7
    kern = functools.partial(_dil_kernel, seq=S, seg=seg, tq=tq)
    return pl.pallas_call(
        kern,
        grid=(nb, DIL_HEADS, DIL_SEGS),
        in_specs=in_specs,
        out_specs=pl.BlockSpec((1, S, hd), lambda b, h, t: (b, 0, h)),
        out_shape=jax.ShapeDtypeStruct((nb, S, DIL_HEADS * hd), BF16),
        scratch_shapes=[pltpu.VMEM((len(DIL_CONFIGS), tq, 2 * tq), F32),
                        pltpu.VMEM((S, hd), F32), pltpu.VMEM((S, hd), F32), pltpu.VMEM((S, hd), F32),
                        pltpu.VMEM((seg + 2 * DIL_HALF, hd), BF16),
                        pltpu.VMEM((seg + 2 * DIL_HALF, hd), BF16)],
        compiler_params=_cparams("parallel", "parallel", "arbitrary"),
        name="dil_attn",
    )(*args)


def _tile(n, pref):
    return pref if n % pref == 0 else n


def _ssd_lane_layout(v):
    v = v.reshape(2, SSD_GROUPS, SSD_HEADS_PER_GROUP).transpose(1, 0, 2)
    v = jnp.pad(v, ((0, 0), (0, 0), (0, GROUP_LANES // 2 - SSD_HEADS_PER_GROUP)))
    return v.reshape(1, LANES)


def _ssd_mixer(x2, h3, kmem, vmem_, w_in, conv_w, conv_b, dt_bias, a_log, d_skip, norm_g, w_out, mem_q_gain):
    nb, S, D = h3.shape
    inner = SSD_GROUPS * SSD_HEADS_PER_GROUP * SSD_HEAD_DIM
    conv_ch = inner + 2 * SSD_GROUPS * SSD_STATE
    nheads = SSD_GROUPS * SSD_HEADS_PER_GROUP
    mw = kmem.shape[2]
    tm = _tile(S, 1024)
    w_bf = w_in.astype(BF16)
    p1 = proj(h3, w_bf, 0, inner + conv_ch, None, out_dtype=F32, tm=tm, tn=512)[:, 0]
    w_dt = w_in[:, inner + conv_ch:inner + conv_ch + 2 * nheads]
    w_dt = w_dt.reshape(D, 2, SSD_GROUPS, SSD_HEADS_PER_GROUP).transpose(0, 2, 1, 3)
    w_dt = jnp.pad(w_dt, ((0, 0), (0, 0), (0, 0), (0, GROUP_LANES // 2 - SSD_HEADS_PER_GROUP)))
    dtr = proj(h3, w_dt.reshape(D, LANES).astype(BF16), 0, LANES, None, out_dtype=F32, tm=tm, tn=LANES)[:, 0]
    q_off = inner + conv_ch + 2 * nheads
    hd = mem_q_gain.shape[0]
    qg = jnp.tile(mem_q_gain, mw // hd).reshape(1, 1, mw)
    q_mem = proj(h3, w_bf[:, q_off:], 0, mw, qg, out_dtype=BF16, tm=tm, tn=mw, norm_width=hd)[:, 0]

    nat_tr = ssd_prep(dtr, _ssd_lane_layout(dt_bias), _ssd_lane_layout(a_log))
    xc = ssd_conv(p1, inner, conv_w, conv_b, ts=tm, tc=512)
    dskip = jnp.repeat(d_skip, SSD_HEAD_DIM)
    y = ssd_scan(xc, p1, nat_tr[:3], nat_tr[3:], dskip, norm_g)
    o_mem = mem_attention(q_mem, kmem, vmem_, tq=tm)
    w_out_bf = w_out.astype(BF16)
    return outproj(x2, y.reshape(nb * S, inner), o_mem.reshape(nb * S, mw),
                   w_out_bf[:inner], w_out_bf[inner:], tm=_tile(nb * S, 512), tn=1024)


def _dil_mixer(x2, h3, kmem, vmem_, w_in, q_gain, k_gain, w_out, rel_bias, mem_q_gain):
    nb, S, D = h3.shape
    width = DIL_HEADS * DIL_HEAD_DIM
    mw = kmem.shape[2]
    tm = _tile(S, 1024)
    w_bf = w_in.astype(BF16)
    qkvs = []
    for g, (_, r) in enumerate(DIL_CONFIGS):
        gains = jnp.stack([jnp.tile(q_gain[g], DIL_HEADS), jnp.tile(k_gain[g], DIL_HEADS),
                           jnp.ones((width,), F32)]).reshape(3, 1, width)
        out = proj(h3, w_bf, g * 3 * width, 3 * width, gains, out_dtype=BF16, tm=tm, tn=width,
                   norm_width=DIL_HEAD_DIM, norm_tiles=2, r=r)
        qkvs.append(out.reshape(nb, S, 3 * width))
    q_off = len(DIL_CONFIGS) * 3 * width
    hd = mem_q_gain.shape[0]
    qg = jnp.tile(mem_q_gain, mw // hd).reshape(1, 1, mw)
    q_mem = proj(h3, w_bf, q_off, mw, qg, out_dtype=BF16, tm=tm, tn=mw, norm_width=hd)[:, 0]
    o = dilated_attention(qkvs, rel_bias)
    o_mem = mem_attention(q_mem, kmem, vmem_, tq=tm)
    w_out_bf = w_out.astype(BF16)
    return outproj(x2, o.reshape(nb * S, width), o_mem.reshape(nb * S, mw),
                   w_out_bf[:width], w_out_bf[width:], tm=_tile(nb * S, 512), tn=1024)


def kernel(x, mem, rel_bias, ffn_norm, ffn_w_in, ffn_w_out, mix_norm, mem_norm, mem_w_kv, mem_q_gain, mem_k_gain, ssd_w_in, ssd_conv_w, ssd_conv_b, ssd_dt_bias, ssd_A_log, ssd_D, ssd_norm, ssd_w_out, dil_w_in, dil_q_gain, dil_k_gain, dil_w_out):
    nb, S, D = x.shape
    depth = ffn_norm.shape[0]
    T = nb * S
    x2 = x.reshape(T, D)
    tm_ffn = _tile(T, 512)
    tf = _tile(ffn_w_out.shape[2], 512)
    for i in range(depth):
        x2 = ffn(x2, ffn_norm[i, 0], ffn_w_in[i, 0].astype(BF16), ffn_w_out[i, 0].astype(BF16), tm_ffn, tf)
        h3 = rmsnorm_bf16(x2, mix_norm[i], _tile(T, 1024)).reshape(nb, S, D)
        kmem, vmem_ = mem_kv(mem, mem_norm[i], mem_w_kv[i].astype(BF16), mem_k_gain[i])
        j = i // 2
        if i % 2 == 0:
            x2 = _ssd_mixer(x2, h3, kmem, vmem_, ssd_w_in[j], ssd_conv_w[j], ssd_conv_b[j], ssd_dt_bias[j],
                            ssd_A_log[j], ssd_D[j], ssd_norm[j], ssd_w_out[j], mem_q_gain[i])
        else:
            x2 = _dil_mixer(x2, h3, kmem, vmem_, dil_w_in[j], dil_q_gain[j], dil_k_gain[j], dil_w_out[j],
                            rel_bias, mem_q_gain[i])
        x2 = ffn(x2, ffn_norm[i, 1], ffn_w_in[i, 1].astype(BF16), ffn_w_out[i, 1].astype(BF16), tm_ffn, tf)
    return x2.reshape(nb, S, D)
```

```python
import functools
import math

import jax
import jax.numpy as jnp
import numpy as np
from jax import lax
from jax.experimental import pallas as pl
from jax.experimental.pallas import tpu as pltpu

F32 = jnp.float32
BF16 = jnp.bfloat16
EPS = 1e-6

MEM_HEADS = 4
SSD_HEAD_DIM = 64
SSD_GROUPS = 8
SSD_HEADS_PER_GROUP = 6
SSD_STATE = 128
SSD_CONV = 5
SSD_CHUNK = 128
DIL_CONFIGS = ((128, 1), (512, 4), (2048, 16))
DIL_HEADS = 8
DIL_HEAD_DIM = 128
DIL_HALF = 64
DIL_SEGS = 16
REL_BUCKETS = 32
REL_MAX_DISTANCE = 1024

LANES = 128
VMEM_LIMIT_BYTES = 56 * 1024 * 1024
GROUP_LANES = 16


def _cparams(*sem):
    return pltpu.CompilerParams(dimension_semantics=sem, vmem_limit_bytes=VMEM_LIMIT_BYTES)


def _rms(x, gain):
    ms = jnp.mean(x * x, axis=-1, keepdims=True)
    return x * lax.rsqrt(ms + EPS) * gain


def _dot(a, b):
    return jnp.dot(a, b, preferred_element_type=F32)


def _dot_nt(a, b):
    return lax.dot_general(a, b, (((1,), (1,)), ((), ())), preferred_element_type=F32)


def _rmsnorm_kernel(x_ref, g_ref, o_ref):
    o_ref[...] = _rms(x_ref[...], g_ref[...]).astype(o_ref.dtype)


def rmsnorm_bf16(x2, gain, tm):
    T, D = x2.shape
    return pl.pallas_call(
        _rmsnorm_kernel,
        grid=(T // tm,),
        in_specs=[pl.BlockSpec((tm, D), lambda i: (i, 0)),
                  pl.BlockSpec((1, D), lambda i: (0, 0))],
        out_specs=pl.BlockSpec((tm, D), lambda i: (i, 0)),
        out_shape=jax.ShapeDtypeStruct((T, D), BF16),
        compiler_params=_cparams("parallel"),
        name="rmsnorm",
    )(x2, gain.reshape(1, D))


def _ffn_kernel(x_ref, g_ref, wg_ref, wu_ref, wo_ref, o_ref, h_ref, acc_ref):
    j = pl.program_id(1)

    @pl.when(j == 0)
    def _():
        h_ref[...] = _rms(x_ref[...], g_ref[...]).astype(BF16)
        acc_ref[...] = jnp.zeros_like(acc_ref)

    h = h_ref[...]
    gate = _dot(h, wg_ref[...])
    up = _dot(h, wu_ref[...])
    a = (jax.nn.silu(gate) * up).astype(BF16)
    acc_ref[...] += _dot(a, wo_ref[...])

    @pl.when(j == pl.num_programs(1) - 1)
    def _():
        o_ref[...] = x_ref[...] + 0.5 * acc_ref[...]


def ffn(x2, gain, w_in, w_out, layer, which, tm, tf):
    T, D = x2.shape
    F = w_out.shape[2]
    nf = F // tf
    return pl.pallas_call(
        _ffn_kernel,
        grid=(T // tm, nf),
        in_specs=[pl.BlockSpec((tm, D), lambda i, j: (i, 0)),
                  pl.BlockSpec((1, D), lambda i, j: (0, 0)),
                  pl.BlockSpec((None, None, D, tf), lambda i, j: (layer, which, 0, j)),
                  pl.BlockSpec((None, None, D, tf), lambda i, j: (layer, which, 0, j + nf)),
                  pl.BlockSpec((None, None, tf, D), lambda i, j: (layer, which, j, 0))],
        out_specs=pl.BlockSpec((tm, D), lambda i, j: (i, 0)),
        out_shape=jax.ShapeDtypeStruct((T, D), F32),
        scratch_shapes=[pltpu.VMEM((tm, D), BF16), pltpu.VMEM((tm, D), F32)],
        compiler_params=_cparams("parallel", "arbitrary"),
        name="ffn",
    )(x2, gain.reshape(1, D), w_in, w_in, w_out)


def _proj_kernel(h_ref, w_ref, g_ref, o_ref, *scratch, norm_width, norm_tiles, r):
    tm, tn = h_ref.shape[1], w_ref.shape[1]
    res = _dot(h_ref[0], w_ref[...])
    if norm_width == 0 and r == 1:
        o_ref[0, 0] = res.astype(o_ref.dtype)
        return
    acc_ref, = scratch
    rows_per = tm // r
    if r > 1:
        assert norm_width in (0, LANES)
        for c in range(tn // LANES):
            acc_ref[c] = res[:, c * LANES:(c + 1) * LANES]
    else:
        acc_ref[...] = res

    def emit(normed):
        if r == 1 and not normed:
            o_ref[0, 0] = acc_ref[...].astype(o_ref.dtype)
            return
        if r == 1:
            for c in range(tn // norm_width):
                cols = slice(c * norm_width, (c + 1) * norm_width)
                o_ref[0, 0, :, cols] = _rms(acc_ref[:, cols], g_ref[0, :, cols]).astype(o_ref.dtype)
            return
        for m in range(r):
            rows = pl.ds(m, rows_per, stride=r)
            for c in range(tn // LANES):
                cols = slice(c * LANES, (c + 1) * LANES)
                blk = acc_ref[c, rows, :]
                if normed:
                    blk = _rms(blk, g_ref[0, :, cols])
                o_ref[0, m, :, cols] = blk.astype(o_ref.dtype)

    if norm_width == 0:
        emit(False)
    elif norm_tiles is None:
        emit(True)
    else:
        j = pl.program_id(2)
        pl.when(j < norm_tiles)(lambda: emit(True))
        pl.when(j >= norm_tiles)(lambda: emit(False))


def proj(h3, w, col0, ncols, gains, *, out_dtype, tm, tn, norm_width=0, norm_tiles=None, r=1):
    nb, S, K = h3.shape
    assert col0 % tn == 0 and ncols % tn == 0 and S % tm == 0 and tm % (8 * r) == 0
    j0 = col0 // tn
    nj = ncols // tn
    if gains is None:
        gains = jnp.ones((nj, 1, tn), F32)
    if norm_width == 0 and r == 1:
        scratch = []
    elif r == 1:
        scratch = [pltpu.VMEM((tm, tn), F32)]
    else:
        scratch = [pltpu.VMEM((tn // LANES, tm, LANES), F32)]
    kern = functools.partial(_proj_kernel, norm_width=norm_width, norm_tiles=norm_tiles, r=r)
    return pl.pallas_call(
        kern,
        grid=(nb, S // tm, nj),
        in_specs=[pl.BlockSpec((1, tm, K), lambda b, i, j: (b, i, 0)),
                  pl.BlockSpec((K, tn), lambda b, i, j: (0, j0 + j)),
                  pl.BlockSpec((1, 1, tn), lambda b, i, j: (j, 0, 0))],
        out_specs=pl.BlockSpec((1, r, tm // r, tn), lambda b, i, j: (b, 0, i, j)),
        out_shape=jax.ShapeDtypeStruct((nb, r, S // r, ncols), out_dtype),
        scratch_shapes=scratch,
        compiler_params=_cparams("parallel", "parallel", "arbitrary"),
        name="proj",
    )(h3, w, gains)


def _outproj_kernel(x_ref, a1_ref, a2_ref, w1_ref, w2_ref, o_ref):
    o_ref[...] = x_ref[...] + _dot(a1_ref[...], w1_ref[...]) + _dot(a2_ref[...], w2_ref[...])


def outproj(x2, a1, a2, w1, w2, tm, tn):
    T, D = x2.shape
    k1, k2 = a1.shape[1], a2.shape[1]
    return pl.pallas_call(
        _outproj_kernel,
        grid=(T // tm, D // tn),
        in_specs=[pl.BlockSpec((tm, tn), lambda i, j: (i, j)),
                  pl.BlockSpec((tm, k1), lambda i, j: (i, 0)),
                  pl.BlockSpec((tm, k2), lambda i, j: (i, 0)),
                  pl.BlockSpec((k1, tn), lambda i, j: (0, j)),
                  pl.BlockSpec((k2, tn), lambda i, j: (0, j))],
        out_specs=pl.BlockSpec((tm, tn), lambda i, j: (i, j)),
        out_shape=jax.ShapeDtypeStruct((T, D), F32),
        compiler_params=_cparams("parallel", "arbitrary"),
        name="outproj",
    )(x2, a1, a2, w1, w2)


def _memkv_kernel(mem_ref, g_ref, w_ref, kg_ref, k_ref, v_ref):
    mw = k_ref.shape[2]
    hd = kg_ref.shape[1]
    memn = _rms(mem_ref[0], g_ref[...]).astype(BF16)
    kv = _dot(memn, w_ref[...])
    for hh in range(mw // hd):
        cols = slice(hh * hd, (hh + 1) * hd)
        k_ref[0, :, cols] = _rms(kv[:, cols], kg_ref[...]).astype(BF16)
    v_ref[0] = kv[:, mw:].astype(BF16)


def mem_kv(mem, mem_gain, w_kv, k_gain):
    nb, M, D = mem.shape
    mw = w_kv.shape[1] // 2
    hd = k_gain.shape[0]
    out = jax.ShapeDtypeStruct((nb, M, mw), BF16)
    return pl.pallas_call(
        _memkv_kernel,
        grid=(nb,),
        in_specs=[pl.BlockSpec((1, M, D), lambda b: (b, 0, 0)),
                  pl.BlockSpec((1, D), lambda b: (0, 0)),
                  pl.BlockSpec((D, 2 * mw), lambda b: (0, 0)),
                  pl.BlockSpec((1, hd), lambda b: (0, 0))],
        out_specs=[pl.BlockSpec((1, M, mw), lambda b: (b, 0, 0))] * 2,
        out_shape=[out, out],
        compiler_params=_cparams("parallel"),
        name="mem_kv",
    )(mem, mem_gain.reshape(1, D), w_kv, k_gain.reshape(1, hd))


def _memattn_kernel(q_ref, k_ref, v_ref, o_ref, *, heads):
    hd = q_ref.shape[2] // heads
    scale = hd ** -0.5
    for hh in range(heads):
        cols = slice(hh * hd, (hh + 1) * hd)
        s = _dot_nt(q_ref[0, :, cols], k_ref[0, :, cols]) * scale
        e = jnp.exp(s - jnp.max(s, axis=-1, keepdims=True))
        p = e / jnp.sum(e, axis=-1, keepdims=True)
        o_ref[0, :, cols] = _dot(p.astype(BF16), v_ref[0, :, cols]).astype(o_ref.dtype)


def mem_attention(q, k, v, tq):
    nb, S, mw = q.shape
    M = k.shape[1]
    return pl.pallas_call(
        functools.partial(_memattn_kernel, heads=MEM_HEADS),
        grid=(nb, S // tq),
        in_specs=[pl.BlockSpec((1, tq, mw), lambda b, i: (b, i, 0)),
                  pl.BlockSpec((1, M, mw), lambda b, i: (b, 0, 0)),
                  pl.BlockSpec((1, M, mw), lambda b, i: (b, 0, 0))],
        out_specs=pl.BlockSpec((1, tq, mw), lambda b, i: (b, i, 0)),
        out_shape=jax.ShapeDtypeStruct((nb, S, mw), BF16),
        compiler_params=_cparams("parallel", "parallel"),
        name="mem_attn",
    )(q, k, v)


def _conv_kernel(prev_ref, cur_ref, next_ref, w_ref, b_ref, o_ref, ext_ref):
    i = pl.program_id(1)
    ts = cur_ref.shape[1]
    halo = prev_ref.shape[1]
    pad = SSD_CONV // 2
    ext_ref[0:halo] = jnp.where(i > 0, prev_ref[0], 0.0)
    ext_ref[halo:halo + ts] = cur_ref[0]
    ext_ref[halo + ts:] = jnp.where(i < pl.num_programs(1) - 1, next_ref[0], 0.0)
    acc = b_ref[...] + w_ref[0:1, :] * ext_ref[halo - pad:halo - pad + ts, :]
    for k in range(1, SSD_CONV):
        acc = acc + w_ref[k:k + 1, :] * ext_ref[halo - pad + k:halo - pad + k + ts, :]
    o_ref[0] = jax.nn.silu(acc)


def ssd_conv(p1, col0, conv_w, conv_b, ts, tc):
    nb, S, _ = p1.shape
    C = conv_w.shape[1]
    halo = 8
    c0 = col0 // tc
    nsb = S // halo
    return pl.pallas_call(
        _conv_kernel,
        grid=(nb, S // ts, C // tc),
        in_specs=[pl.BlockSpec((1, halo, tc), lambda b, i, j: (b, jnp.maximum(i * (ts // halo) - 1, 0), c0 + j)),
                  pl.BlockSpec((1, ts, tc), lambda b, i, j: (b, i, c0 + j)),
                  pl.BlockSpec((1, halo, tc), lambda b, i, j: (b, jnp.minimum((i + 1) * (ts // halo), nsb - 1), c0 + j)),
                  pl.BlockSpec((SSD_CONV, tc), lambda b, i, j: (0, j)),
                  pl.BlockSpec((1, tc), lambda b, i, j: (0, j))],
        out_specs=pl.BlockSpec((1, ts, tc), lambda b, i, j: (b, i, j)),
        out_shape=jax.ShapeDtypeStruct((nb, S, C), F32),
        scratch_shapes=[pltpu.VMEM((ts + 2 * halo, tc), F32)],
        compiler_params=_cparams("parallel", "parallel", "parallel"),
        name="ssd_conv",
    )(p1, p1, p1, conv_w, conv_b.reshape(1, C))


LOG2E = math.log2(math.e)


def _ssd_prep_kernel(dtr_ref, bias_ref, alog_ref, tr_ref):
    Q = dtr_ref.shape[1]
    x = dtr_ref[0] + bias_ref[...]
    dt = jnp.maximum(x, 0.0) + jnp.log1p(jnp.exp(-jnp.abs(x)))
    a = dt * (-jnp.exp(alog_ref[...]))
    ii = lax.broadcasted_iota(jnp.int32, (Q, Q), 0)
    jj = lax.broadcasted_iota(jnp.int32, (Q, Q), 1)
    hi = lax.Precision.HIGHEST
    cs_fwd = jnp.dot((jj <= ii).astype(F32), a, precision=hi, preferred_element_type=F32)
    cs_bwd = jnp.dot((jj >= ii).astype(F32), a, precision=hi, preferred_element_type=F32)
    lane = lax.broadcasted_iota(jnp.int32, (Q, LANES), 1)
    is_bwd = (lane % GROUP_LANES) >= GROUP_LANES // 2
    cs = jnp.where(is_bwd, cs_bwd, cs_fwd)
    last = jnp.where(is_bwd[0:1], cs[0:1], cs[Q - 1:Q])
    cs2 = cs * LOG2E
    tr_ref[0, 0, 0 * LANES:1 * LANES] = cs2.T
    tr_ref[0, 0, 1 * LANES:2 * LANES] = (cs2 - jnp.log2(dt)).T
    tr_ref[0, 0, 2 * LANES:3 * LANES] = (jnp.exp(last - cs) * dt).T
    tr_ref[0, 0, 3 * LANES:4 * LANES] = jnp.broadcast_to(jnp.exp(last), (Q, LANES)).T


def ssd_prep(dtr, bias_l, alog_l):
    nb, S, _ = dtr.shape
    Q = SSD_CHUNK
    nc = S // Q
    vec_spec = pl.BlockSpec((1, LANES), lambda b, c: (0, 0))
    return pl.pallas_call(
        _ssd_prep_kernel,
        grid=(nb, nc),
        in_specs=[pl.BlockSpec((1, Q, LANES), lambda b, c: (b, c, 0)), vec_spec, vec_spec],
        out_specs=pl.BlockSpec((1, 1, 4 * LANES, Q), lambda b, c: (b, c, 0, 0)),
        out_shape=jax.ShapeDtypeStruct((nb, nc, 4 * LANES, Q), F32),
        compiler_params=_cparams("parallel", "parallel"),
        name="ssd_prep",
    )(dtr, bias_l, alog_l)


SSD_GROUP_WIDTH = SSD_HEADS_PER_GROUP * SSD_HEAD_DIM
SSD_GROUP_CONV = SSD_GROUP_WIDTH + 2 * SSD_STATE


def _ssd_direction(d, g, sub, xc_ref, tr_ref, st_ref):
    Q = SSD_CHUNK
    P, N, gw = SSD_HEAD_DIM, SSD_STATE, SSD_GROUP_WIDTH
    rows = slice(sub * Q, (sub + 1) * Q)
    xs = xc_ref[0, rows, 0:gw]
    bm = xc_ref[0, rows, gw:gw + N]
    cm = xc_ref[0, rows, gw + N:gw + 2 * N]
    cb = _dot_nt(cm.astype(BF16), bm.astype(BF16))
    bt = bm.T

    def tr_rows(part):
        start = pl.multiple_of(part * LANES + GROUP_LANES * g, GROUP_LANES)
        return tr_ref[0, sub, pl.ds(start, GROUP_LANES), :]

    cs2T, rowT, wT, cdT = (tr_rows(part) for part in range(4))
    ii = lax.broadcasted_iota(jnp.int32, (Q, Q), 0)
    jj = lax.broadcasted_iota(jnp.int32, (Q, Q), 1)
    causal = (jj <= ii) if d == 0 else (jj >= ii)
    lo = lax.broadcasted_iota(jnp.int32, (Q, LANES), 1) < P
    st = st_ref[d]
    ys, sts = [], []
    for p in range(SSD_HEADS_PER_GROUP // 2):
        cols = slice(p * LANES, (p + 1) * LANES)
        ms, ss = [], []
        k0 = d * (GROUP_LANES // 2) + 2 * p
        for k in (k0, k0 + 1):
            col2 = jnp.broadcast_to(cs2T[k:k + 1, :], (Q, Q)).T
            decay_dt = jnp.exp2(jnp.where(causal, col2 - rowT[k:k + 1, :], -jnp.inf))
            ms.append((cb * decay_dt).astype(BF16))
            ms.append((cm * jnp.exp2(col2)).astype(BF16))
            ss.append((bt * wT[k:k + 1, :]).astype(BF16))
        cd = jnp.where(lo[0:1], cdT[k0:k0 + 1, :], cdT[k0 + 1:k0 + 2, :])
        xp, sp = xs[:, cols], st[:, cols]
        x_lo, x_hi = jnp.where(lo, xp, 0.0).astype(BF16), jnp.where(lo, 0.0, xp).astype(BF16)
        s_lo, s_hi = jnp.where(lo, sp, 0.0).astype(BF16), jnp.where(lo, 0.0, sp).astype(BF16)
        ys.append(_dot(jnp.concatenate(ms, axis=1), jnp.concatenate([x_lo, s_lo, x_hi, s_hi], axis=0)))
        sts.append(cd * sp + _dot(jnp.concatenate(ss, axis=1), jnp.concatenate([x_lo, x_hi], axis=0)))
    st_ref[d] = jnp.concatenate(sts, axis=1)
    return jnp.concatenate(ys, axis=1)


def _ssd_scan_kernel(xcf, xcb, zf, zb, trf, trb, dskip_ref, ng_ref, y_ref, st_ref, ysum_ref, *, kc):
    g = pl.program_id(1)
    c = pl.program_id(2)
    nsteps = pl.num_programs(2)
    Q = SSD_CHUNK

    @pl.when(c == 0)
    def _():
        st_ref[...] = jnp.zeros_like(st_ref)
        ysum_ref[...] = jnp.zeros_like(ysum_ref)

    def finish(y_dir, row0, xc_ref, z_ref, sub):
        rows = pl.ds(pl.multiple_of(row0, Q), Q)
        blk = slice(sub * Q, (sub + 1) * Q)
        tot = ysum_ref[rows, :] + y_dir + dskip_ref[...] * xc_ref[0, blk, 0:SSD_GROUP_WIDTH]
        ysum_ref[rows, :] = y_dir
        gated = tot * jax.nn.silu(z_ref[0, blk, :])
        y_ref[0, rows, :] = _rms(gated, ng_ref[...]).astype(y_ref.dtype)

    for i in range(kc):
        y_f = _ssd_direction(0, g, i, xcf, trf, st_ref)
        finish(y_f, (c * kc + i) * Q, xcf, zf, i)
        y_b = _ssd_direction(1, g, kc - 1 - i, xcb, trb, st_ref)
        finish(y_b, ((nsteps - 1 - c) * kc + kc - 1 - i) * Q, xcb, zb, kc - 1 - i)


def ssd_scan(xc, p1, tr, dskip, norm_g, kc):
    nb, S, _ = xc.shape
    Q = SSD_CHUNK
    nsteps = S // (Q * kc)
    assert nsteps % 2 == 0 and Q == LANES
    G, N, gw, gc = SSD_GROUPS, SSD_STATE, SSD_GROUP_WIDTH, SSD_GROUP_CONV
    inner = G * gw

    def both(shape, f):
        return [pl.BlockSpec(shape, lambda b, g, c: f(b, g, c)),
                pl.BlockSpec(shape, lambda b, g, c: f(b, g, nsteps - 1 - c))]

    in_specs = (both((1, kc * Q, gc), lambda b, g, c: (b, c, g))
                + both((1, kc * Q, gw), lambda b, g, c: (b, c, g))
                + both((1, kc, 4 * LANES, Q), lambda b, g, c: (b, c, 0, 0))
                + [pl.BlockSpec((1, gw), lambda b, g, c: (0, g)),
                   pl.BlockSpec((1, gw), lambda b, g, c: (0, g))])
    return pl.pallas_call(
        functools.partial(_ssd_scan_kernel, kc=kc),
        grid=(nb, G, nsteps),
        in_specs=in_specs,
        out_specs=pl.BlockSpec((1, S, gw), lambda b, g, c: (b, 0, g)),
        out_shape=jax.ShapeDtypeStruct((nb, S, inner), BF16),
        scratch_shapes=[pltpu.VMEM((2, N, gw), F32), pltpu.VMEM((S, gw), F32)],
        compiler_params=_cparams("parallel", "parallel", "arbitrary"),
        name="ssd_scan",
    )(xc, xc, p1, p1, tr, tr, dskip.reshape(1, inner), norm_g.reshape(1, inner))


def _t5_bucket_np(rel):
    half = REL_BUCKETS // 2
    exact = half // 2
    n = np.abs(rel)
    far = exact + (np.log(np.maximum(n, 1).astype(np.float32) / np.float32(exact))
                   / np.float32(math.log(REL_MAX_DISTANCE / exact)) * np.float32(half - exact)).astype(np.int32)
    far = np.minimum(far, half - 1)
    return np.where(rel > 0, half, 0) + np.where(n < exact, n, far)


def _dil_bucket_maps(tq):
    qi = np.arange(tq)[:, None]
    ki = np.arange(2 * tq)[None, :]
    rel = ki - DIL_HALF - qi
    maps = []
    for _, dilation in DIL_CONFIGS:
        maps.append(np.where(np.abs(rel) <= DIL_HALF, _t5_bucket_np(rel * dilation), -1))
    return np.stack(maps).astype(np.int32)


def _dil_kernel(tbl_ref, bm_ref, *refs, seq, seg, tq):
    ngroups = len(DIL_CONFIGS)
    in_refs = refs[:7 * ngroups]
    o_ref = refs[7 * ngroups]
    bias_ref, m_ref, l_ref, a_ref, kext_ref, vext_ref = refs[7 * ngroups + 1:]
    h = pl.program_id(1)
    t = pl.program_id(2)
    scale = DIL_HEAD_DIM ** -0.5

    @pl.when(t == 0)
    def _():
        m_ref[...] = jnp.full_like(m_ref, -jnp.inf)
        l_ref[...] = jnp.zeros_like(l_ref)
        a_ref[...] = jnp.zeros_like(a_ref)
        for gi in range(ngroups):
            bm = bm_ref[gi]
            bias = jnp.zeros(bm.shape, F32)
            for u in range(REL_BUCKETS):
                bias = jnp.where(bm == u, tbl_ref[u, gi * DIL_HEADS + h], bias)
            bias_ref[gi] = jnp.where(bm < 0, -jnp.inf, bias)

    kcol = lax.broadcasted_iota(jnp.int32, (tq, 2 * tq), 1)
    for gi, (_, r) in enumerate(DIL_CONFIGS):
        q_ref, kp, kc, kn, vp, vc, vn = in_refs[7 * gi:7 * gi + 7]
        sub_len = seq // r
        segs_per_sub = sub_len // seg
        m_res = t // segs_per_sub
        j0 = (t % segs_per_sub) * seg
        for ext, prev, cur, nxt in ((kext_ref, kp, kc, kn), (vext_ref, vp, vc, vn)):
            ext[0:DIL_HALF] = prev[0]
            ext[DIL_HALF:DIL_HALF + seg] = cur[0]
            ext[DIL_HALF + seg:] = nxt[0]
        for jb in range(seg // tq):
            q = q_ref[0, jb * tq:(jb + 1) * tq, :]
            kw = kext_ref[jb * tq:jb * tq + 2 * tq, :]
            vw = vext_ref[jb * tq:jb * tq + 2 * tq, :]
            s = _dot_nt(q, kw) * scale + bias_ref[gi]
            kj = kcol + (j0 + jb * tq - DIL_HALF)
            s = jnp.where((kj >= 0) & (kj < sub_len), s, -jnp.inf)
            mb = jnp.max(s, axis=-1, keepdims=True)
            e = jnp.exp(s - mb)
            lb = jnp.sum(e, axis=-1, keepdims=True)
            acc = _dot(e.astype(BF16), vw)
            start = (j0 + jb * tq) * r + m_res
            rows = pl.ds(start, tq, stride=r) if r > 1 else pl.ds(start, tq)
            m_old = m_ref[rows, :]
            m_new = jnp.maximum(m_old, mb)
            alpha = jnp.exp(m_old - m_new)
            beta = jnp.exp(mb - m_new)
            l_new = alpha * l_ref[rows, :] + beta * lb
            a_new = alpha * a_ref[rows, :] + beta * acc
            m_ref[rows, :] = m_new
            l_ref[rows, :] = l_new
            a_ref[rows, :] = a_new

    @pl.when(t == pl.num_programs(2) - 1)
    def _():
        o_ref[0] = (a_ref[...] / l_ref[...]).astype(o_ref.dtype)


def dilated_attention(qkvs, rel_bias):
    nb, S, _ = qkvs[0].shape
    hd = DIL_HEAD_DIM
    tq = 2 * DIL_HALF
    seg = S // DIL_SEGS
    assert seg % tq == 0 and all(w == 2 * DIL_HALF * r and (S // r) % seg == 0 for w, r in DIL_CONFIGS)
    nhb = seg // DIL_HALF
    last_hb = S // DIL_HALF - 1
    in_specs = [pl.BlockSpec(memory_space=pltpu.SMEM),
                pl.BlockSpec((len(DIL_CONFIGS), tq, 2 * tq), lambda b, h, t: (0, 0, 0))]
    args = [rel_bias, jnp.asarray(_dil_bucket_maps(tq))]
    for g in range(len(DIL_CONFIGS)):
        in_specs.append(pl.BlockSpec((1, seg, hd), lambda b, h, t: (b, t, h)))
        for part in (1, 2):
            off = part * DIL_HEADS
            in_specs += [
                pl.BlockSpec((1, DIL_HALF, hd), lambda b, h, t, off=off: (b, jnp.maximum(t * nhb - 1, 0), off + h)),
                pl.BlockSpec((1, seg, hd), lambda b, h, t, off=off: (b, t, off + h)),
                pl.BlockSpec((1, DIL_HALF, hd), lambda b, h, t, off=off: (b, jnp.minimum((t + 1) * nhb, last_hb), off + h)),
            ]
        args += [qkvs[g]] * 7
    kern = functools.partial(_dil_kernel, seq=S, seg=seg, tq=tq)
    return pl.pallas_call(
        kern,
        grid=(nb, DIL_HEADS, DIL_SEGS),
        in_specs=in_specs,
        out_specs=pl.BlockSpec((1, S, hd), lambda b, h, t: (b, 0, h)),
        out_shape=jax.ShapeDtypeStruct((nb, S, DIL_HEADS * hd), BF16),
        scratch_shapes=[pltpu.VMEM((len(DIL_CONFIGS), tq, 2 * tq), F32),
                        pltpu.VMEM((S, hd), F32), pltpu.VMEM((S, hd), F32), pltpu.VMEM((S, hd), F32),
                        pltpu.VMEM((seg + 2 * DIL_HALF, hd), BF16),
                        pltpu.VMEM((seg + 2 * DIL_HALF, hd), BF16)],
        compiler_params=_cparams("parallel", "parallel", "arbitrary"),
        name="dil_attn",
    )(*args)


def _tile(n, pref):
    return pref if n % pref == 0 else n


def _ssd_lane_layout(v):
    v = v.reshape(2, SSD_GROUPS, SSD_HEADS_PER_GROUP).transpose(1, 0, 2)
    v = jnp.pad(v, ((0, 0), (0, 0), (0, GROUP_LANES // 2 - SSD_HEADS_PER_GROUP)))
    return v.reshape(1, LANES)


def _ssd_mixer(x2, h3, kmem, vmem_, w_in, conv_w, conv_b, dt_bias, a_log, d_skip, norm_g, w_out, mem_q_gain):
    nb, S, D = h3.shape
    inner = SSD_GROUPS * SSD_HEADS_PER_GROUP * SSD_HEAD_DIM
    conv_ch = inner + 2 * SSD_GROUPS * SSD_STATE
    nheads = SSD_GROUPS * SSD_HEADS_PER_GROUP
    mw = kmem.shape[2]
    tm = _tile(S, 1024)
    w_bf = w_in.astype(BF16)

    def grouped(t):
        lead = t.shape[:-1]
        parts = [t[..., :inner].reshape(*lead, SSD_GROUPS, SSD_GROUP_WIDTH),
                 t[..., inner:inner + SSD_GROUPS * SSD_STATE].reshape(*lead, SSD_GROUPS, SSD_STATE),
                 t[..., inner + SSD_GROUPS * SSD_STATE:].reshape(*lead, SSD_GROUPS, SSD_STATE)]
        return jnp.concatenate(parts, axis=-1).reshape(*lead, conv_ch)

    w_p1 = jnp.concatenate([w_bf[:, :inner], grouped(w_bf[:, inner:inner + conv_ch])], axis=1)
    conv_w, conv_b = grouped(conv_w), grouped(conv_b)
    p1 = proj(h3, w_p1, 0, inner + conv_ch, None, out_dtype=F32, tm=tm, tn=512)[:, 0]
    w_dt = w_in[:, inner + conv_ch:inner + conv_ch + 2 * nheads]
    w_dt = w_dt.reshape(D, 2, SSD_GROUPS, SSD_HEADS_PER_GROUP).transpose(0, 2, 1, 3)
    w_dt = jnp.pad(w_dt, ((0, 0), (0, 0), (0, 0), (0, GROUP_LANES // 2 - SSD_HEADS_PER_GROUP)))
    dtr = proj(h3, w_dt.reshape(D, LANES).astype(BF16), 0, LANES, None, out_dtype=F32, tm=tm, tn=LANES)[:, 0]
    q_off = inner + conv_ch + 2 * nheads
    hd = mem_q_gain.shape[0]
    qg = jnp.tile(mem_q_gain, mw // hd).reshape(1, 1, mw)
    q_mem = proj(h3, w_bf[:, q_off:], 0, mw, qg, out_dtype=BF16, tm=tm, tn=mw, norm_width=hd)[:, 0]

    tr = ssd_prep(dtr, _ssd_lane_layout(dt_bias), _ssd_lane_layout(a_log))
    xc = ssd_conv(p1, inner, conv_w, conv_b, ts=tm, tc=512)
    dskip = jnp.repeat(d_skip, SSD_HEAD_DIM)
    kc = 2 if (S // SSD_CHUNK) % 4 == 0 else 1
    y = ssd_scan(xc, p1, tr, dskip, norm_g, kc)
    o_mem = mem_attention(q_mem, kmem, vmem_, tq=tm)
    w_out_bf = w_out.astype(BF16)
    return outproj(x2, y.reshape(nb * S, inner), o_mem.reshape(nb * S, mw),
                   w_out_bf[:inner], w_out_bf[inner:], tm=_tile(nb * S, 512), tn=1024)


def _dil_mixer(x2, h3, kmem, vmem_, w_in, q_gain, k_gain, w_out, rel_bias, mem_q_gain):
    nb, S, D = h3.shape
    width = DIL_HEADS * DIL_HEAD_DIM
    mw = kmem.shape[2]
    tm = _tile(S, 1024)
    w_bf = w_in.astype(BF16)
    qkvs = []
    for g, (_, r) in enumerate(DIL_CONFIGS):
        gains = jnp.stack([jnp.tile(q_gain[g], DIL_HEADS), jnp.tile(k_gain[g], DIL_HEADS),
                           jnp.ones((width,), F32)]).reshape(3, 1, width)
        out = proj(h3, w_bf, g * 3 * width, 3 * width, gains, out_dtype=BF16, tm=tm, tn=width,
                   norm_width=DIL_HEAD_DIM, norm_tiles=2, r=r)
        qkvs.append(out.reshape(nb, S, 3 * width))
    q_off = len(DIL_CONFIGS) * 3 * width
    hd = mem_q_gain.shape[0]
    qg = jnp.tile(mem_q_gain, mw // hd).reshape(1, 1, mw)
    q_mem = proj(h3, w_bf, q_off, mw, qg, out_dtype=BF16, tm=tm, tn=mw, norm_width=hd)[:, 0]
    o = dilated_attention(qkvs, rel_bias)
    o_mem = mem_attention(q_mem, kmem, vmem_, tq=tm)
    w_out_bf = w_out.astype(BF16)
    return outproj(x2, o.reshape(nb * S, width), o_mem.reshape(nb * S, mw),
                   w_out_bf[:width], w_out_bf[width:], tm=_tile(nb * S, 512), tn=1024)


def kernel(x, mem, rel_bias, ffn_norm, ffn_w_in, ffn_w_out, mix_norm, mem_norm, mem_w_kv, mem_q_gain, mem_k_gain, ssd_w_in, ssd_conv_w, ssd_conv_b, ssd_dt_bias, ssd_A_log, ssd_D, ssd_norm, ssd_w_out, dil_w_in, dil_q_gain, dil_k_gain, dil_w_out):
    nb, S, D = x.shape
    depth = ffn_norm.shape[0]
    T = nb * S
    x2 = x.reshape(T, D)
    tm_ffn = _tile(T, 512)
    tf = _tile(ffn_w_out.shape[2], 512)
    ffn_w_in = ffn_w_in.astype(BF16)
    ffn_w_out = ffn_w_out.astype(BF16)
    for i in range(depth):
        x2 = ffn(x2, ffn_norm[i, 0], ffn_w_in, ffn_w_out, i, 0, tm_ffn, tf)
        h3 = rmsnorm_bf16(x2, mix_norm[i], _tile(T, 1024)).reshape(nb, S, D)
        kmem, vmem_ = mem_kv(mem, mem_norm[i], mem_w_kv[i].astype(BF16), mem_k_gain[i])
        j = i // 2
        if i % 2 == 0:
            x2 = _ssd_mixer(x2, h3, kmem, vmem_, ssd_w_in[j], ssd_conv_w[j], ssd_conv_b[j], ssd_dt_bias[j],
                            ssd_A_log[j], ssd_D[j], ssd_norm[j], ssd_w_out[j], mem_q_gain[i])
        else:
            x2 = _dil_mixer(x2, h3, kmem, vmem_, dil_w_in[j], dil_q_gain[j], dil_k_gain[j], dil_w_out[j],
                            rel_bias, mem_q_gain[i])
        x2 = ffn(x2, ffn_norm[i, 1], ffn_w_in, ffn_w_out, i, 1, tm_ffn, tf)
    return x2.reshape(nb, S, D)
```

```python
import functools
import math

import jax
import jax.numpy as jnp
import numpy as np
from jax import lax
from jax.experimental import pallas as pl
from jax.experimental.pallas import tpu as pltpu

F32 = jnp.float32
BF16 = jnp.bfloat16
EPS = 1e-6

MEM_HEADS = 4
SSD_HEAD_DIM = 64
SSD_GROUPS = 8
SSD_HEADS_PER_GROUP = 6
SSD_STATE = 128
SSD_CONV = 5
SSD_CHUNK = 128
DIL_CONFIGS = ((128, 1), (512, 4), (2048, 16))
DIL_HEADS = 8
DIL_HEAD_DIM = 128
DIL_HALF = 64
DIL_SEGS = 16
REL_BUCKETS = 32
REL_MAX_DISTANCE = 1024

LANES = 128
VMEM_LIMIT_BYTES = 56 * 1024 * 1024
GROUP_LANES = 16


def _cparams(*sem):
    return pltpu.CompilerParams(dimension_semantics=sem, vmem_limit_bytes=VMEM_LIMIT_BYTES)


def _rms(x, gain):
    ms = jnp.mean(x * x, axis=-1, keepdims=True)
    return x * lax.rsqrt(ms + EPS) * gain


def _dot(a, b):
    return jnp.dot(a, b, preferred_element_type=F32)


def _dot_nt(a, b):
    return lax.dot_general(a, b, (((1,), (1,)), ((), ())), preferred_element_type=F32)


def _ffn_kernel(x_ref, g_ref, wg_ref, wu_ref, wo_ref, *rest, emit_norm):
    if emit_norm:
        g2_ref, o_ref, hn_ref, h_ref = rest
    else:
        o_ref, h_ref = rest
    j = pl.program_id(1)

    @pl.when(j == 0)
    def _():
        h_ref[...] = _rms(x_ref[...], g_ref[...]).astype(BF16)
        o_ref[...] = x_ref[...]

    h = h_ref[...]
    gate = _dot(h, wg_ref[...])
    up = _dot(h, wu_ref[...])
    a = (jax.nn.silu(gate) * up * 0.5).astype(BF16)
    o_ref[...] += _dot(a, wo_ref[...])

    if emit_norm:
        @pl.when(j == pl.num_programs(1) - 1)
        def _():
            hn_ref[...] = _rms(o_ref[...], g2_ref[...]).astype(hn_ref.dtype)


def ffn(x2, gain, w_in, w_out, layer, which, tm, tf, next_gain=None):
    T, D = x2.shape
    F = w_out.shape[2]
    nf = F // tf
    emit_norm = next_gain is not None
    row_spec = pl.BlockSpec((tm, D), lambda i, j: (i, 0))
    vec_spec = pl.BlockSpec((1, D), lambda i, j: (0, 0))
    in_specs = [row_spec, vec_spec,
                pl.BlockSpec((None, None, D, tf), lambda i, j: (layer, which, 0, j)),
                pl.BlockSpec((None, None, D, tf), lambda i, j: (layer, which, 0, j + nf)),
                pl.BlockSpec((None, None, tf, D), lambda i, j: (layer, which, j, 0))]
    args = [x2, gain.reshape(1, D), w_in, w_in, w_out]
    out_specs, out_shape = row_spec, jax.ShapeDtypeStruct((T, D), F32)
    if emit_norm:
        in_specs.append(vec_spec)
        args.append(next_gain.reshape(1, D))
        out_specs, out_shape = [row_spec, row_spec], [out_shape, jax.ShapeDtypeStruct((T, D), BF16)]
    return pl.pallas_call(
        functools.partial(_ffn_kernel, emit_norm=emit_norm),
        grid=(T // tm, nf),
        in_specs=in_specs,
        out_specs=out_specs,
        out_shape=out_shape,
        scratch_shapes=[pltpu.VMEM((tm, D), BF16)],
        compiler_params=_cparams("parallel", "arbitrary"),
        name="ffn",
    )(*args)


def _proj_kernel(h_ref, w_ref, g_ref, o_ref, *scratch, norm_width, norm_tiles, r):
    tm, tn = h_ref.shape[1], w_ref.shape[1]
    res = _dot(h_ref[0], w_ref[...])
    if norm_width == 0 and r == 1:
        o_ref[0, 0] = res.astype(o_ref.dtype)
        return
    acc_ref, = scratch
    rows_per = tm // r
    if r > 1:
        assert norm_width in (0, LANES)
        for c in range(tn // LANES):
            acc_ref[c] = res[:, c * LANES:(c + 1) * LANES]
    else:
        acc_ref[...] = res

    def emit(normed):
        if r == 1 and not normed:
            o_ref[0, 0] = acc_ref[...].astype(o_ref.dtype)
            return
        if r == 1:
            for c in range(tn // norm_width):
                cols = slice(c * norm_width, (c + 1) * norm_width)
                o_ref[0, 0, :, cols] = _rms(acc_ref[:, cols], g_ref[0, :, cols]).astype(o_ref.dtype)
            return
        for m in range(r):
            rows = pl.ds(m, rows_per, stride=r)
            for c in range(tn // LANES):
                cols = slice(c * LANES, (c + 1) * LANES)
                blk = acc_ref[c, rows, :]
                if normed:
                    blk = _rms(blk, g_ref[0, :, cols])
                o_ref[0, m, :, cols] = blk.astype(o_ref.dtype)

    if norm_width == 0:
        emit(False)
    elif norm_tiles is None:
        emit(True)
    else:
        j = pl.program_id(2)
        pl.when(j < norm_tiles)(lambda: emit(True))
        pl.when(j >= norm_tiles)(lambda: emit(False))


def proj(h3, w, col0, ncols, gains, *, out_dtype, tm, tn, norm_width=0, norm_tiles=None, r=1):
    nb, S, K = h3.shape
    assert col0 % tn == 0 and ncols % tn == 0 and S % tm == 0 and tm % (8 * r) == 0
    j0 = col0 // tn
    nj = ncols // tn
    if gains is None:
        gains = jnp.ones((nj, 1, tn), F32)
    if norm_width == 0 and r == 1:
        scratch = []
    elif r == 1:
        scratch = [pltpu.VMEM((tm, tn), F32)]
    else:
        scratch = [pltpu.VMEM((tn // LANES, tm, LANES), F32)]
    kern = functools.partial(_proj_kernel, norm_width=norm_width, norm_tiles=norm_tiles, r=r)
    return pl.pallas_call(
        kern,
        grid=(nb, S // tm, nj),
        in_specs=[pl.BlockSpec((1, tm, K), lambda b, i, j: (b, i, 0)),
                  pl.BlockSpec((K, tn), lambda b, i, j: (0, j0 + j)),
                  pl.BlockSpec((1, 1, tn), lambda b, i, j: (j, 0, 0))],
        out_specs=pl.BlockSpec((1, r, tm // r, tn), lambda b, i, j: (b, 0, i, j)),
        out_shape=jax.ShapeDtypeStruct((nb, r, S // r, ncols), out_dtype),
        scratch_shapes=scratch,
        compiler_params=_cparams("parallel", "parallel", "arbitrary"),
        name="proj",
    )(h3, w, gains)


def _side_proj_kernel(h_ref, w_ref, g_ref, q_ref, dt_ref, *, head_dim):
    mw = q_ref.shape[2]
    res = _dot(h_ref[0], w_ref[...])
    for c in range(mw // head_dim):
        cols = slice(c * head_dim, (c + 1) * head_dim)
        q_ref[0, :, cols] = _rms(res[:, cols], g_ref[...]).astype(q_ref.dtype)
    dt_ref[0] = res[:, mw:]


def side_proj(h3, w, q_gain, tm):
    nb, S, K = h3.shape
    head_dim = q_gain.shape[0]
    n = w.shape[1]
    mw = n - LANES
    return pl.pallas_call(
        functools.partial(_side_proj_kernel, head_dim=head_dim),
        grid=(nb, S // tm),
        in_specs=[pl.BlockSpec((1, tm, K), lambda b, i: (b, i, 0)),
                  pl.BlockSpec((K, n), lambda b, i: (0, 0)),
                  pl.BlockSpec((1, head_dim), lambda b, i: (0, 0))],
        out_specs=[pl.BlockSpec((1, tm, mw), lambda b, i: (b, i, 0)),
                   pl.BlockSpec((1, tm, LANES), lambda b, i: (b, i, 0))],
        out_shape=[jax.ShapeDtypeStruct((nb, S, mw), BF16), jax.ShapeDtypeStruct((nb, S, LANES), F32)],
        compiler_params=_cparams("parallel", "parallel"),
        name="side_proj",
    )(h3, w, q_gain.reshape(1, head_dim))


def _outproj_kernel(x_ref, a1_ref, a2_ref, w1_ref, w2_ref, o_ref):
    o_ref[...] = x_ref[...] + _dot(a1_ref[...], w1_ref[...]) + _dot(a2_ref[...], w2_ref[...])


def outproj(x2, a1, a2, w1, w2, tm, tn):
    T, D = x2.shape
    k1, k2 = a1.shape[1], a2.shape[1]
    return pl.pallas_call(
        _outproj_kernel,
        grid=(T // tm, D // tn),
        in_specs=[pl.BlockSpec((tm, tn), lambda i, j: (i, j)),
                  pl.BlockSpec((tm, k1), lambda i, j: (i, 0)),
                  pl.BlockSpec((tm, k2), lambda i, j: (i, 0)),
                  pl.BlockSpec((k1, tn), lambda i, j: (0, j)),
                  pl.BlockSpec((k2, tn), lambda i, j: (0, j))],
        out_specs=pl.BlockSpec((tm, tn), lambda i, j: (i, j)),
        out_shape=jax.ShapeDtypeStruct((T, D), F32),
        compiler_params=_cparams("parallel", "arbitrary"),
        name="outproj",
    )(x2, a1, a2, w1, w2)


def _memkv_kernel(mem_ref, g_ref, w_ref, kg_ref, k_ref, v_ref):
    mw = k_ref.shape[2]
    hd = kg_ref.shape[1]
    memn = _rms(mem_ref[0], g_ref[...]).astype(BF16)
    kv = _dot(memn, w_ref[...])
    for hh in range(mw // hd):
        cols = slice(hh * hd, (hh + 1) * hd)
        k_ref[0, :, cols] = _rms(kv[:, cols], kg_ref[...]).astype(BF16)
    v_ref[0] = kv[:, mw:].astype(BF16)


def mem_kv(mem, mem_gain, w_kv, k_gain):
    nb, M, D = mem.shape
    mw = w_kv.shape[1] // 2
    hd = k_gain.shape[0]
    out = jax.ShapeDtypeStruct((nb, M, mw), BF16)
    return pl.pallas_call(
        _memkv_kernel,
        grid=(nb,),
        in_specs=[pl.BlockSpec((1, M, D), lambda b: (b, 0, 0)),
                  pl.BlockSpec((1, D), lambda b: (0, 0)),
                  pl.BlockSpec((D, 2 * mw), lambda b: (0, 0)),
                  pl.BlockSpec((1, hd), lambda b: (0, 0))],
        out_specs=[pl.BlockSpec((1, M, mw), lambda b: (b, 0, 0))] * 2,
        out_shape=[out, out],
        compiler_params=_cparams("parallel"),
        name="mem_kv",
    )(mem, mem_gain.reshape(1, D), w_kv, k_gain.reshape(1, hd))


def _memattn_kernel(q_ref, k_ref, v_ref, o_ref, *, heads):
    hd = q_ref.shape[2] // heads
    scale = hd ** -0.5
    for hh in range(heads):
        cols = slice(hh * hd, (hh + 1) * hd)
        s = _dot_nt(q_ref[0, :, cols], k_ref[0, :, cols]) * scale
        e = jnp.exp(s - jnp.max(s, axis=-1, keepdims=True))
        p = e / jnp.sum(e, axis=-1, keepdims=True)
        o_ref[0, :, cols] = _dot(p.astype(BF16), v_ref[0, :, cols]).astype(o_ref.dtype)


def mem_attention(q, k, v, tq):
    nb, S, mw = q.shape
    M = k.shape[1]
    return pl.pallas_call(
        functools.partial(_memattn_kernel, heads=MEM_HEADS),
        grid=(nb, S // tq),
        in_specs=[pl.BlockSpec((1, tq, mw), lambda b, i: (b, i, 0)),
                  pl.BlockSpec((1, M, mw), lambda b, i: (b, 0, 0)),
                  pl.BlockSpec((1, M, mw), lambda b, i: (b, 0, 0))],
        out_specs=pl.BlockSpec((1, tq, mw), lambda b, i: (b, i, 0)),
        out_shape=jax.ShapeDtypeStruct((nb, S, mw), BF16),
        compiler_params=_cparams("parallel", "parallel"),
        name="mem_attn",
    )(q, k, v)


def _conv_kernel(prev_ref, cur_ref, next_ref, w_ref, b_ref, o_ref, ext_ref):
    i = pl.program_id(1)
    ts = cur_ref.shape[1]
    halo = prev_ref.shape[1]
    pad = SSD_CONV // 2
    for c in range(cur_ref.shape[2] // LANES):
        cols = slice(c * LANES, (c + 1) * LANES)
        ext_ref[c, 0:halo] = jnp.where(i > 0, prev_ref[0, :, cols], 0.0)
        ext_ref[c, halo:halo + ts] = cur_ref[0, :, cols]
        ext_ref[c, halo + ts:] = jnp.where(i < pl.num_programs(1) - 1, next_ref[0, :, cols], 0.0)
        acc = b_ref[:, cols] + w_ref[0:1, cols] * ext_ref[c, halo - pad:halo - pad + ts, :]
        for k in range(1, SSD_CONV):
            acc = acc + w_ref[k:k + 1, cols] * ext_ref[c, halo - pad + k:halo - pad + k + ts, :]
        o_ref[0, :, cols] = jax.nn.silu(acc)


def ssd_conv(p1, col0, conv_w, conv_b, ts, tc):
    nb, S, _ = p1.shape
    C = conv_w.shape[1]
    halo = 8
    c0 = col0 // tc
    nsb = S // halo
    return pl.pallas_call(
        _conv_kernel,
        grid=(nb, S // ts, C // tc),
        in_specs=[pl.BlockSpec((1, halo, tc), lambda b, i, j: (b, jnp.maximum(i * (ts // halo) - 1, 0), c0 + j)),
                  pl.BlockSpec((1, ts, tc), lambda b, i, j: (b, i, c0 + j)),
                  pl.BlockSpec((1, halo, tc), lambda b, i, j: (b, jnp.minimum((i + 1) * (ts // halo), nsb - 1), c0 + j)),
                  pl.BlockSpec((SSD_CONV, tc), lambda b, i, j: (0, j)),
                  pl.BlockSpec((1, tc), lambda b, i, j: (0, j))],
        out_specs=pl.BlockSpec((1, ts, tc), lambda b, i, j: (b, i, j)),
        out_shape=jax.ShapeDtypeStruct((nb, S, C), F32),
        scratch_shapes=[pltpu.VMEM((tc // LANES, ts + 2 * halo, LANES), F32)],
        compiler_params=_cparams("parallel", "parallel", "parallel"),
        name="ssd_conv",
    )(p1, p1, p1, conv_w, conv_b.reshape(1, C))


LOG2E = math.log2(math.e)


def _ssd_prep_kernel(dtr_ref, bias_ref, alog_ref, tr_ref):
    Q = dtr_ref.shape[1]
    x = dtr_ref[0] + bias_ref[...]
    dt = jnp.maximum(x, 0.0) + jnp.log1p(jnp.exp(-jnp.abs(x)))
    a = dt * (-jnp.exp(alog_ref[...]))
    ii = lax.broadcasted_iota(jnp.int32, (Q, Q), 0)
    jj = lax.broadcasted_iota(jnp.int32, (Q, Q), 1)
    hi = lax.Precision.HIGHEST
    cs_fwd = jnp.dot((jj <= ii).astype(F32), a, precision=hi, preferred_element_type=F32)
    cs_bwd = jnp.dot((jj >= ii).astype(F32), a, precision=hi, preferred_element_type=F32)
    lane = lax.broadcasted_iota(jnp.int32, (Q, LANES), 1)
    is_bwd = (lane % GROUP_LANES) >= GROUP_LANES // 2
    cs = jnp.where(is_bwd, cs_bwd, cs_fwd)
    last = jnp.where(is_bwd[0:1], cs[0:1], cs[Q - 1:Q])
    cs2 = cs * LOG2E
    tr_ref[0, 0, 0 * LANES:1 * LANES] = cs2.T
    tr_ref[0, 0, 1 * LANES:2 * LANES] = (cs2 - jnp.log2(dt)).T
    tr_ref[0, 0, 2 * LANES:3 * LANES] = (jnp.exp(last - cs) * dt).T
    tr_ref[0, 0, 3 * LANES:4 * LANES] = jnp.broadcast_to(jnp.exp(last), (Q, LANES)).T


def ssd_prep(dtr, bias_l, alog_l):
    nb, S, _ = dtr.shape
    Q = SSD_CHUNK
    nc = S // Q
    vec_spec = pl.BlockSpec((1, LANES), lambda b, c: (0, 0))
    return pl.pallas_call(
        _ssd_prep_kernel,
        grid=(nb, nc),
        in_specs=[pl.BlockSpec((1, Q, LANES), lambda b, c: (b, c, 0)), vec_spec, vec_spec],
        out_specs=pl.BlockSpec((1, 1, 4 * LANES, Q), lambda b, c: (b, c, 0, 0)),
        out_shape=jax.ShapeDtypeStruct((nb, nc, 4 * LANES, Q), F32),
        compiler_params=_cparams("parallel", "parallel"),
        name="ssd_prep",
    )(dtr, bias_l, alog_l)


SSD_GROUP_WIDTH = SSD_HEADS_PER_GROUP * SSD_HEAD_DIM
SSD_GROUP_CONV = SSD_GROUP_WIDTH + 2 * SSD_STATE


def _ssd_direction(d, g, sub, xc_ref, tr_ref, st_ref):
    Q = SSD_CHUNK
    P, N, gw = SSD_HEAD_DIM, SSD_STATE, SSD_GROUP_WIDTH
    rows = slice(sub * Q, (sub + 1) * Q)
    xs = xc_ref[0, rows, 0:gw]
    bm = xc_ref[0, rows, gw:gw + N]
    cm = xc_ref[0, rows, gw + N:gw + 2 * N]
    cb = _dot_nt(cm.astype(BF16), bm.astype(BF16))
    bt = bm.T

    def tr_rows(part):
        start = pl.multiple_of(part * LANES + GROUP_LANES * g, GROUP_LANES)
        return tr_ref[0, sub, pl.ds(start, GROUP_LANES), :]

    cs2T, rowT, wT, cdT = (tr_rows(part) for part in range(4))
    ii = lax.broadcasted_iota(jnp.int32, (Q, Q), 0)
    jj = lax.broadcasted_iota(jnp.int32, (Q, Q), 1)
    causal = (jj <= ii) if d == 0 else (jj >= ii)
    lo = lax.broadcasted_iota(jnp.int32, (Q, LANES), 1) < P
    st = st_ref[d]
    ys, sts = [], []
    for p in range(SSD_HEADS_PER_GROUP // 2):
        cols = slice(p * LANES, (p + 1) * LANES)
        ms, ss = [], []
        k0 = d * (GROUP_LANES // 2) + 2 * p
        for k in (k0, k0 + 1):
            col2 = jnp.broadcast_to(cs2T[k:k + 1, :], (Q, Q)).T
            decay_dt = jnp.exp2(jnp.where(causal, col2 - rowT[k:k + 1, :], -jnp.inf))
            ms.append((cb * decay_dt).astype(BF16))
            ms.append((cm * jnp.exp2(col2)).astype(BF16))
            ss.append((bt * wT[k:k + 1, :]).astype(BF16))
        cd = jnp.where(lo[0:1], cdT[k0:k0 + 1, :], cdT[k0 + 1:k0 + 2, :])
        xp, sp = xs[:, cols], st[:, cols]
        x_lo, x_hi = jnp.where(lo, xp, 0.0).astype(BF16), jnp.where(lo, 0.0, xp).astype(BF16)
        s_lo, s_hi = jnp.where(lo, sp, 0.0).astype(BF16), jnp.where(lo, 0.0, sp).astype(BF16)
        ys.append(_dot(jnp.concatenate(ms, axis=1), jnp.concatenate([x_lo, s_lo, x_hi, s_hi], axis=0)))
        sts.append(cd * sp + _dot(jnp.concatenate(ss, axis=1), jnp.concatenate([x_lo, x_hi], axis=0)))
    st_ref[d] = jnp.concatenate(sts, axis=1)
    return jnp.concatenate(ys, axis=1)


def _ssd_scan_kernel(xcf, xcb, zf, zb, trf, trb, dskip_ref, ng_ref, y_ref, st_ref, ysum_ref, *, kc):
    g = pl.program_id(1)
    c = pl.program_id(2)
    nsteps = pl.num_programs(2)
    Q = SSD_CHUNK

    @pl.when(c == 0)
    def _():
        st_ref[...] = jnp.zeros_like(st_ref)
        ysum_ref[...] = jnp.zeros_like(ysum_ref)

    def finish(y_dir, row0, xc_ref, z_ref, sub):
        rows = pl.ds(pl.multiple_of(row0, Q), Q)
        blk = slice(sub * Q, (sub + 1) * Q)
        tot = ysum_ref[rows, :] + y_dir + dskip_ref[...] * xc_ref[0, blk, 0:SSD_GROUP_WIDTH]
        ysum_ref[rows, :] = y_dir
        gated = tot * jax.nn.silu(z_ref[0, blk, :])
        y_ref[0, rows, :] = _rms(gated, ng_ref[...]).astype(y_ref.dtype)

    for i in range(kc):
        y_f = _ssd_direction(0, g, i, xcf, trf, st_ref)
        finish(y_f, (c * kc + i) * Q, xcf, zf, i)
        y_b = _ssd_direction(1, g, kc - 1 - i, xcb, trb, st_ref)
        finish(y_b, ((nsteps - 1 - c) * kc + kc - 1 - i) * Q, xcb, zb, kc - 1 - i)


def ssd_scan(xc, p1, tr, dskip, norm_g, kc):
    nb, S, _ = xc.shape
    Q = SSD_CHUNK
    nsteps = S // (Q * kc)
    assert nsteps % 2 == 0 and Q == LANES
    G, N, gw, gc = SSD_GROUPS, SSD_STATE, SSD_GROUP_WIDTH, SSD_GROUP_CONV
    inner = G * gw

    def both(shape, f):
        return [pl.BlockSpec(shape, lambda b, g, c: f(b, g, c)),
                pl.BlockSpec(shape, lambda b, g, c: f(b, g, nsteps - 1 - c))]

    in_specs = (both((1, kc * Q, gc), lambda b, g, c: (b, c, g))
                + both((1, kc * Q, gw), lambda b, g, c: (b, c, g))
                + both((1, kc, 4 * LANES, Q), lambda b, g, c: (b, c, 0, 0))
                + [pl.BlockSpec((1, gw), lambda b, g, c: (0, g)),
                   pl.BlockSpec((1, gw), lambda b, g, c: (0, g))])
    return pl.pallas_call(
        functools.partial(_ssd_scan_kernel, kc=kc),
        grid=(nb, G, nsteps),
        in_specs=in_specs,
        out_specs=pl.BlockSpec((1, S, gw), lambda b, g, c: (b, 0, g)),
        out_shape=jax.ShapeDtypeStruct((nb, S, inner), BF16),
        scratch_shapes=[pltpu.VMEM((2, N, gw), F32), pltpu.VMEM((S, gw), F32)],
        compiler_params=_cparams("parallel", "parallel", "arbitrary"),
        name="ssd_scan",
    )(xc, xc, p1, p1, tr, tr, dskip.reshape(1, inner), norm_g.reshape(1, inner))


def _t5_bucket_np(rel):
    half = REL_BUCKETS // 2
    exact = half // 2
    n = np.abs(rel)
    far = exact + (np.log(np.maximum(n, 1).astype(np.float32) / np.float32(exact))
                   / np.float32(math.log(REL_MAX_DISTANCE / exact)) * np.float32(half - exact)).astype(np.int32)
    far = np.minimum(far, half - 1)
    return np.where(rel > 0, half, 0) + np.where(n < exact, n, far)


def _dil_bucket_rows(tq):
    d = np.arange(2 * tq)
    rel = d - DIL_HALF
    rows = []
    for _, dilation in DIL_CONFIGS:
        rows.append(np.where(d <= 2 * DIL_HALF, _t5_bucket_np(rel * dilation), -1))
    return np.broadcast_to(np.stack(rows)[:, None, :], (len(DIL_CONFIGS), 8, 2 * tq)).astype(np.int32)


def _dil_kernel(tbl_ref, bm_ref, *refs, seq, seg, tq):
    ngroups = len(DIL_CONFIGS)
    in_refs = refs[:7 * ngroups]
    o_ref = refs[7 * ngroups]
    bias_ref, m_ref, l_ref, a_ref, kext_ref, vext_ref = refs[7 * ngroups + 1:]
    h = pl.program_id(1)
    t = pl.program_id(2)
    scale = DIL_HEAD_DIM ** -0.5

    @pl.when(t == 0)
    def _():
        m_ref[...] = jnp.full_like(m_ref, -jnp.inf)
        l_ref[...] = jnp.zeros_like(l_ref)
        a_ref[...] = jnp.zeros_like(a_ref)
        kcol = lax.broadcasted_iota(jnp.int32, (tq, 2 * tq), 1)
        for gi in range(ngroups):
            ids = bm_ref[gi]
            row = jnp.zeros(ids.shape, F32)
            for u in range(REL_BUCKETS):
                row = jnp.where(ids == u, tbl_ref[u, gi * DIL_HEADS + h], row)
            row = jnp.where(ids < 0, -jnp.inf, row)
            band = pltpu.roll(jnp.broadcast_to(row[0:1], (tq, 2 * tq)), 0, 1, stride=1, stride_axis=0)
            no_left = jnp.where(kcol >= DIL_HALF, band, -jnp.inf)
            bias_ref[gi, 0] = band
            bias_ref[gi, 1] = no_left
            bias_ref[gi, 2] = jnp.where(kcol < 2 * tq - DIL_HALF, band, -jnp.inf)
            bias_ref[gi, 3] = jnp.where(kcol < 2 * tq - DIL_HALF, no_left, -jnp.inf)

    nblk = seg // tq
    for gi, (_, r) in enumerate(DIL_CONFIGS):
        q_ref, kp, kc, kn, vp, vc, vn = in_refs[7 * gi:7 * gi + 7]
        sub_len = seq // r
        segs_per_sub = sub_len // seg
        m_res = t // segs_per_sub
        j0 = (t % segs_per_sub) * seg
        for ext, prev, cur, nxt in ((kext_ref, kp, kc, kn), (vext_ref, vp, vc, vn)):
            ext[gi, 0:DIL_HALF] = prev[0]
            ext[gi, DIL_HALF:DIL_HALF + seg] = cur[0]
            ext[gi, DIL_HALF + seg:] = nxt[0]
        seg_in_sub = t % segs_per_sub
        for jb in range(nblk):
            q = q_ref[0, jb * tq:(jb + 1) * tq, :]
            kw = kext_ref[gi, jb * tq:jb * tq + 2 * tq, :]
            vw = vext_ref[gi, jb * tq:jb * tq + 2 * tq, :]
            variant = 0
            if jb == 0:
                variant = variant + (seg_in_sub == 0).astype(jnp.int32)
            if jb == nblk - 1:
                variant = variant + 2 * (seg_in_sub == segs_per_sub - 1).astype(jnp.int32)
            s = _dot_nt(q, kw) * scale + bias_ref[gi, variant]
            mb = jnp.max(s, axis=-1, keepdims=True)
            e = jnp.exp(s - mb)
            lb = jnp.sum(e, axis=-1, keepdims=True)
            acc = _dot(e.astype(BF16), vw)
            start = (j0 + jb * tq) * r + m_res
            rows = pl.ds(start, tq, stride=r) if r > 1 else pl.ds(start, tq)
            m_old = m_ref[rows, :]
            m_new = jnp.maximum(m_old, mb)
            alpha = jnp.exp(m_old - m_new)
            beta = jnp.exp(mb - m_new)
            l_new = alpha * l_ref[rows, :] + beta * lb
            a_new = alpha * a_ref[rows, :] + beta * acc
            m_ref[rows, :] = m_new
            l_ref[rows, :] = l_new
            a_ref[rows, :] = a_new

    @pl.when(t == pl.num_programs(2) - 1)
    def _():
        o_ref[0] = (a_ref[...] / l_ref[...]).astype(o_ref.dtype)


def dilated_attention(qkvs, rel_bias):
    nb, S, _ = qkvs[0].shape
    hd = DIL_HEAD_DIM
    tq = 2 * DIL_HALF
    seg = S // DIL_SEGS
    assert seg % tq == 0 and all(w == 2 * DIL_HALF * r and (S // r) % seg == 0 for w, r in DIL_CONFIGS)
    nhb = seg // DIL_HALF
    last_hb = S // DIL_HALF - 1
    in_specs = [pl.BlockSpec(memory_space=pltpu.SMEM),
                pl.BlockSpec((len(DIL_CONFIGS), 8, 2 * tq), lambda b, h, t: (0, 0, 0))]
    args = [rel_bias, jnp.asarray(_dil_bucket_rows(tq))]
    for g in range(len(DIL_CONFIGS)):
        in_specs.append(pl.BlockSpec((1, seg, hd), lambda b, h, t: (b, t, h)))
        for part in (1, 2):
            off = part * DIL_HEADS
            in_specs += [
                pl.BlockSpec((1, DIL_HALF, hd), lambda b, h, t, off=off: (b, jnp.maximum(t * nhb - 1, 0), off + h)),
                pl.BlockSpec((1, seg, hd), lambda b, h, t, off=off: (b, t, off + h)),
                pl.BlockSpec((1, DIL_HALF, hd), lambda b, h, t, off=off: (b, jnp.minimum((t + 1) * nhb, last_hb), off + h)),
            ]
        args += [qkvs[g]] * 7
    kern = functools.partial(_dil_kernel, seq=S, seg=seg, tq=tq)
    return pl.pallas_call(
        kern,
        grid=(nb, DIL_HEADS, DIL_SEGS),
        in_specs=in_specs,
        out_specs=pl.BlockSpec((1, S, hd), lambda b, h, t: (b, 0, h)),
        out_shape=jax.ShapeDtypeStruct((nb, S, DIL_HEADS * hd), BF16),
        scratch_shapes=[pltpu.VMEM((len(DIL_CONFIGS), 4, tq, 2 * tq), F32),
                        pltpu.VMEM((S, hd), F32), pltpu.VMEM((S, hd), F32), pltpu.VMEM((S, hd), F32),
                        pltpu.VMEM((len(DIL_CONFIGS), seg + 2 * DIL_HALF, hd), BF16),
                        pltpu.VMEM((len(DIL_CONFIGS), seg + 2 * DIL_HALF, hd), BF16)],
        compiler_params=_cparams("parallel", "parallel", "arbitrary"),
        name="dil_attn",
    )(*args)


def _tile(n, pref):
    return pref if n % pref == 0 else n


def _ssd_lane_layout(v):
    v = v.reshape(2, SSD_GROUPS, SSD_HEADS_PER_GROUP).transpose(1, 0, 2)
    v = jnp.pad(v, ((0, 0), (0, 0), (0, GROUP_LANES // 2 - SSD_HEADS_PER_GROUP)))
    return v.reshape(1, LANES)


def _ssd_mixer(x2, h3, kmem, vmem_, w_in, conv_w, conv_b, dt_bias, a_log, d_skip, norm_g, w_out, mem_q_gain):
    nb, S, D = h3.shape
    inner = SSD_GROUPS * SSD_HEADS_PER_GROUP * SSD_HEAD_DIM
    conv_ch = inner + 2 * SSD_GROUPS * SSD_STATE
    nheads = SSD_GROUPS * SSD_HEADS_PER_GROUP
    mw = kmem.shape[2]
    tm = _tile(S, 1024)
    w_bf = w_in.astype(BF16)

    def grouped(t):
        lead = t.shape[:-1]
        parts = [t[..., :inner].reshape(*lead, SSD_GROUPS, SSD_GROUP_WIDTH),
                 t[..., inner:inner + SSD_GROUPS * SSD_STATE].reshape(*lead, SSD_GROUPS, SSD_STATE),
                 t[..., inner + SSD_GROUPS * SSD_STATE:].reshape(*lead, SSD_GROUPS, SSD_STATE)]
        return jnp.concatenate(parts, axis=-1).reshape(*lead, conv_ch)

    w_p1 = jnp.concatenate([w_bf[:, :inner], grouped(w_bf[:, inner:inner + conv_ch])], axis=1)
    conv_w, conv_b = grouped(conv_w), grouped(conv_b)
    p1 = proj(h3, w_p1, 0, inner + conv_ch, None, out_dtype=F32, tm=tm, tn=512)[:, 0]
    w_dt = w_in[:, inner + conv_ch:inner + conv_ch + 2 * nheads]
    w_dt = w_dt.reshape(D, 2, SSD_GROUPS, SSD_HEADS_PER_GROUP).transpose(0, 2, 1, 3)
    w_dt = jnp.pad(w_dt, ((0, 0), (0, 0), (0, 0), (0, GROUP_LANES // 2 - SSD_HEADS_PER_GROUP)))
    q_off = inner + conv_ch + 2 * nheads
    w_side = jnp.concatenate([w_bf[:, q_off:], w_dt.reshape(D, LANES).astype(BF16)], axis=1)
    q_mem, dtr = side_proj(h3, w_side, mem_q_gain, tm)

    tr = ssd_prep(dtr, _ssd_lane_layout(dt_bias), _ssd_lane_layout(a_log))
    xc = ssd_conv(p1, inner, conv_w, conv_b, ts=tm, tc=512)
    dskip = jnp.repeat(d_skip, SSD_HEAD_DIM)
    nchunks = S // SSD_CHUNK
    kc = next(k for k in (4, 2, 1) if nchunks % (2 * k) == 0)
    y = ssd_scan(xc, p1, tr, dskip, norm_g, kc)
    o_mem = mem_attention(q_mem, kmem, vmem_, tq=tm)
    w_out_bf = w_out.astype(BF16)
    return outproj(x2, y.reshape(nb * S, inner), o_mem.reshape(nb * S, mw),
                   w_out_bf[:inner], w_out_bf[inner:], tm=_tile(nb * S, 512), tn=1024)


def _dil_mixer(x2, h3, kmem, vmem_, w_in, q_gain, k_gain, w_out, rel_bias, mem_q_gain):
    nb, S, D = h3.shape
    width = DIL_HEADS * DIL_HEAD_DIM
    mw = kmem.shape[2]
    tm = _tile(S, 1024)
    w_bf = w_in.astype(BF16)
    qkvs = []
    for g, (_, r) in enumerate(DIL_CONFIGS):
        gains = jnp.stack([jnp.tile(q_gain[g], DIL_HEADS), jnp.tile(k_gain[g], DIL_HEADS),
                           jnp.ones((width,), F32)]).reshape(3, 1, width)
        out = proj(h3, w_bf, g * 3 * width, 3 * width, gains, out_dtype=BF16, tm=tm, tn=width,
                   norm_width=DIL_HEAD_DIM, norm_tiles=2, r=r)
        qkvs.append(out.reshape(nb, S, 3 * width))
    q_off = len(DIL_CONFIGS) * 3 * width
    hd = mem_q_gain.shape[0]
    qg = jnp.tile(mem_q_gain, mw // hd).reshape(1, 1, mw)
    q_mem = proj(h3, w_bf, q_off, mw, qg, out_dtype=BF16, tm=tm, tn=mw, norm_width=hd)[:, 0]
    o = dilated_attention(qkvs, rel_bias)
    o_mem = mem_attention(q_mem, kmem, vmem_, tq=tm)
    w_out_bf = w_out.astype(BF16)
    return outproj(x2, o.reshape(nb * S, width), o_mem.reshape(nb * S, mw),
                   w_out_bf[:width], w_out_bf[width:], tm=_tile(nb * S, 512), tn=1024)


def kernel(x, mem, rel_bias, ffn_norm, ffn_w_in, ffn_w_out, mix_norm, mem_norm, mem_w_kv, mem_q_gain, mem_k_gain, ssd_w_in, ssd_conv_w, ssd_conv_b, ssd_dt_bias, ssd_A_log, ssd_D, ssd_norm, ssd_w_out, dil_w_in, dil_q_gain, dil_k_gain, dil_w_out):
    nb, S, D = x.shape
    depth = ffn_norm.shape[0]
    T = nb * S
    x2 = x.reshape(T, D)
    tm_ffn = _tile(T, 512)
    tf = _tile(ffn_w_out.shape[2], 512)
    ffn_w_in = ffn_w_in.astype(BF16)
    ffn_w_out = ffn_w_out.astype(BF16)
    for i in range(depth):
        x2, h2 = ffn(x2, ffn_norm[i, 0], ffn_w_in, ffn_w_out, i, 0, tm_ffn, tf, next_gain=mix_norm[i])
        h3 = h2.reshape(nb, S, D)
        kmem, vmem_ = mem_kv(mem, mem_norm[i], mem_w_kv[i].astype(BF16), mem_k_gain[i])
        j = i // 2
        if i % 2 == 0:
            x2 = _ssd_mixer(x2, h3, kmem, vmem_, ssd_w_in[j], ssd_conv_w[j], ssd_conv_b[j], ssd_dt_bias[j],
                            ssd_A_log[j], ssd_D[j], ssd_norm[j], ssd_w_out[j], mem_q_gain[i])
        else:
            x2 = _dil_mixer(x2, h3, kmem, vmem_, dil_w_in[j], dil_q_gain[j], dil_k_gain[j], dil_w_out[j],
                            rel_bias, mem_q_gain[i])
        x2 = ffn(x2, ffn_norm[i, 1], ffn_w_in, ffn_w_out, i, 1, tm_ffn, tf)
    return x2.reshape(nb, S, D)
```

```python
import functools
import math

import jax
import jax.numpy as jnp
import numpy as np
from jax import lax
from jax.experimental import pallas as pl
from jax.experimental.pallas import tpu as pltpu

F32 = jnp.float32
BF16 = jnp.bfloat16
EPS = 1e-6

MEM_HEADS = 4
SSD_HEAD_DIM = 64
SSD_GROUPS = 8
SSD_HEADS_PER_GROUP = 6
SSD_STATE = 128
SSD_CONV = 5
SSD_CHUNK = 128
DIL_CONFIGS = ((128, 1), (512, 4), (2048, 16))
DIL_HEADS = 8
DIL_HEAD_DIM = 128
DIL_HALF = 64
DIL_SEGS = 16
REL_BUCKETS = 32
REL_MAX_DISTANCE = 1024

LANES = 128
VMEM_LIMIT_BYTES = 62 * 1024 * 1024
GROUP_LANES = 16


def _cparams(*sem):
    return pltpu.CompilerParams(dimension_semantics=sem, vmem_limit_bytes=VMEM_LIMIT_BYTES)


def _rms(x, gain):
    ms = jnp.mean(x * x, axis=-1, keepdims=True)
    return x * lax.rsqrt(ms + EPS) * gain


def _dot(a, b):
    return jnp.dot(a, b, preferred_element_type=F32)


def _dot_nt(a, b):
    return lax.dot_general(a, b, (((1,), (1,)), ((), ())), preferred_element_type=F32)


FFN_SUB_ROWS = 512


def _ffn_kernel(x_ref, g_ref, wg_ref, wu_ref, wo_ref, *rest, emit_norm):
    if emit_norm:
        g2_ref, o_ref, hn_ref, h_ref = rest
    else:
        o_ref, h_ref = rest
    j = pl.program_id(1)

    tm = x_ref.shape[0]
    sub = min(tm, FFN_SUB_ROWS)

    @pl.when(j == 0)
    def _():
        for r0 in range(0, tm, sub // 2):
            rows = slice(r0, r0 + sub // 2)
            h_ref[rows, :] = _rms(x_ref[rows, :], g_ref[...]).astype(BF16)
            o_ref[rows, :] = x_ref[rows, :]

    for r0 in range(0, tm, sub):
        h = h_ref[r0:r0 + sub, :]
        gate = _dot(h, wg_ref[...])
        up = _dot(h, wu_ref[...])
        a = (jax.nn.silu(gate) * up * 0.5).astype(BF16)
        o_ref[r0:r0 + sub, :] += _dot(a, wo_ref[...])

    if emit_norm:
        @pl.when(j == pl.num_programs(1) - 1)
        def _():
            for r0 in range(0, tm, sub // 2):
                rows = slice(r0, r0 + sub // 2)
                hn_ref[rows, :] = _rms(o_ref[rows, :], g2_ref[...]).astype(hn_ref.dtype)


def ffn(x2, gain, w_in, w_out, layer, which, tm, tf, next_gain=None):
    T, D = x2.shape
    F = w_out.shape[2]
    nf = F // tf
    emit_norm = next_gain is not None
    row_spec = pl.BlockSpec((tm, D), lambda i, j: (i, 0))
    vec_spec = pl.BlockSpec((1, D), lambda i, j: (0, 0))
    in_specs = [row_spec, vec_spec,
                pl.BlockSpec((None, None, D, tf), lambda i, j: (layer, which, 0, j)),
                pl.BlockSpec((None, None, D, tf), lambda i, j: (layer, which, 0, j + nf)),
                pl.BlockSpec((None, None, tf, D), lambda i, j: (layer, which, j, 0))]
    args = [x2, gain.reshape(1, D), w_in, w_in, w_out]
    out_specs, out_shape = row_spec, jax.ShapeDtypeStruct((T, D), F32)
    if emit_norm:
        in_specs.append(vec_spec)
        args.append(next_gain.reshape(1, D))
        out_specs, out_shape = [row_spec, row_spec], [out_shape, jax.ShapeDtypeStruct((T, D), BF16)]
    return pl.pallas_call(
        functools.partial(_ffn_kernel, emit_norm=emit_norm),
        grid=(T // tm, nf),
        in_specs=in_specs,
        out_specs=out_specs,
        out_shape=out_shape,
        scratch_shapes=[pltpu.VMEM((tm, D), BF16)],
        compiler_params=_cparams("parallel", "arbitrary"),
        name="ffn",
    )(*args)


PROJ_CHUNK = 256


def _proj_kernel(h_ref, w_ref, g_ref, o_ref, *scratch, norm_width, norm_tiles, r):
    tm, tn = h_ref.shape[1], w_ref.shape[1]
    rows_per = tm // r
    nw = norm_width if norm_width else PROJ_CHUNK
    assert PROJ_CHUNK % nw == 0 or nw % PROJ_CHUNK == 0
    chunk = max(PROJ_CHUNK, nw)

    def emit(normed):
        for c0 in range(0, tn, chunk):
            res = _dot(h_ref[0], w_ref[:, c0:c0 + chunk])
            if r == 1:
                if not normed:
                    o_ref[0, 0, :, c0:c0 + chunk] = res.astype(o_ref.dtype)
                    continue
                for c in range(c0, c0 + chunk, nw):
                    blk = res[:, c - c0:c - c0 + nw]
                    o_ref[0, 0, :, c:c + nw] = _rms(blk, g_ref[0, :, c:c + nw]).astype(o_ref.dtype)
                continue
            acc_ref, = scratch
            for c in range(c0, c0 + chunk, LANES):
                acc_ref[c // LANES] = res[:, c - c0:c - c0 + LANES]
            for m in range(r):
                rows = pl.ds(m, rows_per, stride=r)
                for c in range(c0, c0 + chunk, LANES):
                    blk = acc_ref[c // LANES, rows, :]
                    if normed:
                        blk = _rms(blk, g_ref[0, :, c:c + LANES])
                    o_ref[0, m, :, c:c + LANES] = blk.astype(o_ref.dtype)

    if r > 1:
        assert norm_width in (0, LANES)
    if norm_width == 0:
        emit(False)
    elif norm_tiles is None:
        emit(True)
    else:
        j = pl.program_id(2)
        pl.when(j < norm_tiles)(lambda: emit(True))
        pl.when(j >= norm_tiles)(lambda: emit(False))


def proj(h3, w, col0, ncols, gains, *, out_dtype, tm, tn, norm_width=0, norm_tiles=None, r=1):
    nb, S, K = h3.shape
    assert col0 % tn == 0 and ncols % tn == 0 and S % tm == 0 and tm % (8 * r) == 0
    j0 = col0 // tn
    nj = ncols // tn
    if gains is None:
        gains = jnp.ones((nj, 1, tn), F32)
    scratch = [] if r == 1 else [pltpu.VMEM((tn // LANES, tm, LANES), F32)]
    kern = functools.partial(_proj_kernel, norm_width=norm_width, norm_tiles=norm_tiles, r=r)
    return pl.pallas_call(
        kern,
        grid=(nb, S // tm, nj),
        in_specs=[pl.BlockSpec((1, tm, K), lambda b, i, j: (b, i, 0)),
                  pl.BlockSpec((K, tn), lambda b, i, j: (0, j0 + j)),
                  pl.BlockSpec((1, 1, tn), lambda b, i, j: (j, 0, 0))],
        out_specs=pl.BlockSpec((1, r, tm // r, tn), lambda b, i, j: (b, 0, i, j)),
        out_shape=jax.ShapeDtypeStruct((nb, r, S // r, ncols), out_dtype),
        scratch_shapes=scratch,
        compiler_params=_cparams("parallel", "parallel", "arbitrary"),
        name="proj",
    )(h3, w, gains)


def _side_proj_kernel(h_ref, w_ref, g_ref, q_ref, dt_ref, *, head_dim):
    mw = q_ref.shape[2]
    for c in range(mw // head_dim):
        cols = slice(c * head_dim, (c + 1) * head_dim)
        q_ref[0, :, cols] = _rms(_dot(h_ref[0], w_ref[:, cols]), g_ref[...]).astype(q_ref.dtype)
    dt_ref[0] = _dot(h_ref[0], w_ref[:, mw:])


def side_proj(h3, w, q_gain, tm):
    nb, S, K = h3.shape
    head_dim = q_gain.shape[0]
    n = w.shape[1]
    mw = n - LANES
    return pl.pallas_call(
        functools.partial(_side_proj_kernel, head_dim=head_dim),
        grid=(nb, S // tm),
        in_specs=[pl.BlockSpec((1, tm, K), lambda b, i: (b, i, 0)),
                  pl.BlockSpec((K, n), lambda b, i: (0, 0)),
                  pl.BlockSpec((1, head_dim), lambda b, i: (0, 0))],
        out_specs=[pl.BlockSpec((1, tm, mw), lambda b, i: (b, i, 0)),
                   pl.BlockSpec((1, tm, LANES), lambda b, i: (b, i, 0))],
        out_shape=[jax.ShapeDtypeStruct((nb, S, mw), BF16), jax.ShapeDtypeStruct((nb, S, LANES), F32)],
        compiler_params=_cparams("parallel", "parallel"),
        name="side_proj",
    )(h3, w, q_gain.reshape(1, head_dim))


def _outproj_kernel(x_ref, a1_ref, a2_ref, w1_ref, w2_ref, o_ref):
    o_ref[...] = x_ref[...] + _dot(a1_ref[...], w1_ref[...]) + _dot(a2_ref[...], w2_ref[...])


def outproj(x2, a1, a2, w1, w2, tm, tn):
    T, D = x2.shape
    k1, k2 = a1.shape[1], a2.shape[1]
    return pl.pallas_call(
        _outproj_kernel,
        grid=(T // tm, D // tn),
        in_specs=[pl.BlockSpec((tm, tn), lambda i, j: (i, j)),
                  pl.BlockSpec((tm, k1), lambda i, j: (i, 0)),
                  pl.BlockSpec((tm, k2), lambda i, j: (i, 0)),
                  pl.BlockSpec((k1, tn), lambda i, j: (0, j)),
                  pl.BlockSpec((k2, tn), lambda i, j: (0, j))],
        out_specs=pl.BlockSpec((tm, tn), lambda i, j: (i, j)),
        out_shape=jax.ShapeDtypeStruct((T, D), F32),
        compiler_params=_cparams("parallel", "arbitrary"),
        name="outproj",
    )(x2, a1, a2, w1, w2)


def _memkv_kernel(mem_ref, g_ref, w_ref, kg_ref, k_ref, v_ref):
    mw = k_ref.shape[2]
    hd = kg_ref.shape[1]
    memn = _rms(mem_ref[0], g_ref[...]).astype(BF16)
    kv = _dot(memn, w_ref[...])
    for hh in range(mw // hd):
        cols = slice(hh * hd, (hh + 1) * hd)
        k_ref[0, :, cols] = _rms(kv[:, cols], kg_ref[...]).astype(BF16)
    v_ref[0] = kv[:, mw:].astype(BF16)


def mem_kv(mem, mem_gain, w_kv, k_gain):
    nb, M, D = mem.shape
    mw = w_kv.shape[1] // 2
    hd = k_gain.shape[0]
    out = jax.ShapeDtypeStruct((nb, M, mw), BF16)
    return pl.pallas_call(
        _memkv_kernel,
        grid=(nb,),
        in_specs=[pl.BlockSpec((1, M, D), lambda b: (b, 0, 0)),
                  pl.BlockSpec((1, D), lambda b: (0, 0)),
                  pl.BlockSpec((D, 2 * mw), lambda b: (0, 0)),
                  pl.BlockSpec((1, hd), lambda b: (0, 0))],
        out_specs=[pl.BlockSpec((1, M, mw), lambda b: (b, 0, 0))] * 2,
        out_shape=[out, out],
        compiler_params=_cparams("parallel"),
        name="mem_kv",
    )(mem, mem_gain.reshape(1, D), w_kv, k_gain.reshape(1, hd))


def _memattn_kernel(q_ref, k_ref, v_ref, o_ref, *, heads):
    hd = q_ref.shape[2] // heads
    scale = hd ** -0.5
    for hh in range(heads):
        cols = slice(hh * hd, (hh + 1) * hd)
        s = _dot_nt(q_ref[0, :, cols], k_ref[0, :, cols]) * scale
        e = jnp.exp(s - jnp.max(s, axis=-1, keepdims=True))
        p = e / jnp.sum(e, axis=-1, keepdims=True)
        o_ref[0, :, cols] = _dot(p.astype(BF16), v_ref[0, :, cols]).astype(o_ref.dtype)


def mem_attention(q, k, v, tq):
    nb, S, mw = q.shape
    M = k.shape[1]
    return pl.pallas_call(
        functools.partial(_memattn_kernel, heads=MEM_HEADS),
        grid=(nb, S // tq),
        in_specs=[pl.BlockSpec((1, tq, mw), lambda b, i: (b, i, 0)),
                  pl.BlockSpec((1, M, mw), lambda b, i: (b, 0, 0)),
                  pl.BlockSpec((1, M, mw), lambda b, i: (b, 0, 0))],
        out_specs=pl.BlockSpec((1, tq, mw), lambda b, i: (b, i, 0)),
        out_shape=jax.ShapeDtypeStruct((nb, S, mw), BF16),
        compiler_params=_cparams("parallel", "parallel"),
        name="mem_attn",
    )(q, k, v)


def _conv_kernel(prev_ref, cur_ref, next_ref, w_ref, b_ref, o_ref, ext_ref):
    i = pl.program_id(1)
    ts = cur_ref.shape[1]
    halo = prev_ref.shape[1]
    pad = SSD_CONV // 2
    for c in range(cur_ref.shape[2] // LANES):
        cols = slice(c * LANES, (c + 1) * LANES)
        ext_ref[c, 0:halo] = jnp.where(i > 0, prev_ref[0, :, cols], 0.0)
        ext_ref[c, halo:halo + ts] = cur_ref[0, :, cols]
        ext_ref[c, halo + ts:] = jnp.where(i < pl.num_programs(1) - 1, next_ref[0, :, cols], 0.0)
        acc = b_ref[:, cols] + w_ref[0:1, cols] * ext_ref[c, halo - pad:halo - pad + ts, :]
        for k in range(1, SSD_CONV):
            acc = acc + w_ref[k:k + 1, cols] * ext_ref[c, halo - pad + k:halo - pad + k + ts, :]
        o_ref[0, :, cols] = jax.nn.silu(acc)


def ssd_conv(p1, col0, conv_w, conv_b, ts, tc):
    nb, S, _ = p1.shape
    C = conv_w.shape[1]
    halo = 8
    c0 = col0 // tc
    nsb = S // halo
    return pl.pallas_call(
        _conv_kernel,
        grid=(nb, S // ts, C // tc),
        in_specs=[pl.BlockSpec((1, halo, tc), lambda b, i, j: (b, jnp.maximum(i * (ts // halo) - 1, 0), c0 + j)),
                  pl.BlockSpec((1, ts, tc), lambda b, i, j: (b, i, c0 + j)),
                  pl.BlockSpec((1, halo, tc), lambda b, i, j: (b, jnp.minimum((i + 1) * (ts // halo), nsb - 1), c0 + j)),
                  pl.BlockSpec((SSD_CONV, tc), lambda b, i, j: (0, j)),
                  pl.BlockSpec((1, tc), lambda b, i, j: (0, j))],
        out_specs=pl.BlockSpec((1, ts, tc), lambda b, i, j: (b, i, j)),
        out_shape=jax.ShapeDtypeStruct((nb, S, C), F32),
        scratch_shapes=[pltpu.VMEM((tc // LANES, ts + 2 * halo, LANES), F32)],
        compiler_params=_cparams("parallel", "parallel", "parallel"),
        name="ssd_conv",
    )(p1, p1, p1, conv_w, conv_b.reshape(1, C))


LOG2E = math.log2(math.e)


def _ssd_prep_kernel(dtr_ref, bias_ref, alog_ref, tr_ref):
    Q = SSD_CHUNK
    ii = lax.broadcasted_iota(jnp.int32, (Q, Q), 0)
    jj = lax.broadcasted_iota(jnp.int32, (Q, Q), 1)
    tril = (jj <= ii).astype(F32)
    triu = (jj >= ii).astype(F32)
    lane = lax.broadcasted_iota(jnp.int32, (Q, LANES), 1)
    is_bwd = (lane % GROUP_LANES) >= GROUP_LANES // 2
    hi = lax.Precision.HIGHEST
    for i in range(tr_ref.shape[1]):
        x = dtr_ref[0, i * Q:(i + 1) * Q, :] + bias_ref[...]
        dt = jnp.maximum(x, 0.0) + jnp.log1p(jnp.exp(-jnp.abs(x)))
        a = dt * (-jnp.exp(alog_ref[...]))
        cs_fwd = jnp.dot(tril, a, precision=hi, preferred_element_type=F32)
        cs_bwd = jnp.dot(triu, a, precision=hi, preferred_element_type=F32)
        cs = jnp.where(is_bwd, cs_bwd, cs_fwd)
        last = jnp.where(is_bwd[0:1], cs[0:1], cs[Q - 1:Q])
        cs2 = cs * LOG2E
        tr_ref[0, i, 0 * LANES:1 * LANES] = cs2.T
        tr_ref[0, i, 1 * LANES:2 * LANES] = (cs2 - jnp.log2(dt)).T
        tr_ref[0, i, 2 * LANES:3 * LANES] = (jnp.exp(last - cs) * dt).T
        tr_ref[0, i, 3 * LANES:4 * LANES] = jnp.broadcast_to(jnp.exp(last), (Q, LANES)).T


def ssd_prep(dtr, bias_l, alog_l):
    nb, S, _ = dtr.shape
    Q = SSD_CHUNK
    nc = S // Q
    kp = next(k for k in (8, 4, 2, 1) if nc % k == 0)
    vec_spec = pl.BlockSpec((1, LANES), lambda b, c: (0, 0))
    return pl.pallas_call(
        _ssd_prep_kernel,
        grid=(nb, nc // kp),
        in_specs=[pl.BlockSpec((1, kp * Q, LANES), lambda b, c: (b, c, 0)), vec_spec, vec_spec],
        out_specs=pl.BlockSpec((1, kp, 4 * LANES, Q), lambda b, c: (b, c, 0, 0)),
        out_shape=jax.ShapeDtypeStruct((nb, nc, 4 * LANES, Q), F32),
        compiler_params=_cparams("parallel", "parallel"),
        name="ssd_prep",
    )(dtr, bias_l, alog_l)


SSD_GROUP_WIDTH = SSD_HEADS_PER_GROUP * SSD_HEAD_DIM
SSD_GROUP_CONV = SSD_GROUP_WIDTH + 2 * SSD_STATE


def _ssd_direction(d, g, sub, xc_ref, tr_ref, st_ref):
    Q = SSD_CHUNK
    P, N, gw = SSD_HEAD_DIM, SSD_STATE, SSD_GROUP_WIDTH
    rows = slice(sub * Q, (sub + 1) * Q)
    xs = xc_ref[0, rows, 0:gw]
    bm = xc_ref[0, rows, gw:gw + N]
    cm = xc_ref[0, rows, gw + N:gw + 2 * N]
    cb = _dot_nt(cm.astype(BF16), bm.astype(BF16))
    bt = bm.T

    def tr_rows(part):
        start = pl.multiple_of(part * LANES + GROUP_LANES * g, GROUP_LANES)
        return tr_ref[0, sub, pl.ds(start, GROUP_LANES), :]

    cs2T, rowT, wT, cdT = (tr_rows(part) for part in range(4))
    ii = lax.broadcasted_iota(jnp.int32, (Q, Q), 0)
    jj = lax.broadcasted_iota(jnp.int32, (Q, Q), 1)
    causal = (jj <= ii) if d == 0 else (jj >= ii)
    lo = lax.broadcasted_iota(jnp.int32, (Q, LANES), 1) < P
    st = st_ref[d]
    ys, sts = [], []
    for p in range(SSD_HEADS_PER_GROUP // 2):
        cols = slice(p * LANES, (p + 1) * LANES)
        ms, ss = [], []
        k0 = d * (GROUP_LANES // 2) + 2 * p
        for k in (k0, k0 + 1):
            col2 = jnp.broadcast_to(cs2T[k:k + 1, :], (Q, Q)).T
            decay_dt = jnp.exp2(jnp.where(causal, col2 - rowT[k:k + 1, :], -jnp.inf))
            ms.append((cb * decay_dt).astype(BF16))
            ms.append((cm * jnp.exp2(col2)).astype(BF16))
            ss.append((bt * wT[k:k + 1, :]).astype(BF16))
        cd = jnp.where(lo[0:1], cdT[k0:k0 + 1, :], cdT[k0 + 1:k0 + 2, :])
        xp, sp = xs[:, cols], st[:, cols]
        x_lo, x_hi = jnp.where(lo, xp, 0.0).astype(BF16), jnp.where(lo, 0.0, xp).astype(BF16)
        s_lo, s_hi = jnp.where(lo, sp, 0.0).astype(BF16), jnp.where(lo, 0.0, sp).astype(BF16)
        ys.append(_dot(jnp.concatenate(ms, axis=1), jnp.concatenate([x_lo, s_lo, x_hi, s_hi], axis=0)))
        sts.append(cd * sp + _dot(jnp.concatenate(ss, axis=1), jnp.concatenate([x_lo, x_hi], axis=0)))
    st_ref[d] = jnp.concatenate(sts, axis=1)
    return jnp.concatenate(ys, axis=1)


def _ssd_scan_kernel(xcf, xcb, zf, zb, trf, trb, dskip_ref, ng_ref, y_ref, st_ref, ysum_ref, *, kc):
    g = pl.program_id(1)
    c = pl.program_id(2)
    nsteps = pl.num_programs(2)
    Q = SSD_CHUNK

    @pl.when(c == 0)
    def _():
        st_ref[...] = jnp.zeros_like(st_ref)
        ysum_ref[...] = jnp.zeros_like(ysum_ref)

    def finish(y_dir, row0, xc_ref, z_ref, sub):
        rows = pl.ds(pl.multiple_of(row0, Q), Q)
        blk = slice(sub * Q, (sub + 1) * Q)
        tot = ysum_ref[rows, :] + y_dir + dskip_ref[...] * xc_ref[0, blk, 0:SSD_GROUP_WIDTH]
        ysum_ref[rows, :] = y_dir
        gated = tot * jax.nn.silu(z_ref[0, blk, :])
        y_ref[0, rows, :] = _rms(gated, ng_ref[...]).astype(y_ref.dtype)

    for i in range(kc):
        y_f = _ssd_direction(0, g, i, xcf, trf, st_ref)
        finish(y_f, (c * kc + i) * Q, xcf, zf, i)
        y_b = _ssd_direction(1, g, kc - 1 - i, xcb, trb, st_ref)
        finish(y_b, ((nsteps - 1 - c) * kc + kc - 1 - i) * Q, xcb, zb, kc - 1 - i)


def ssd_scan(xc, p1, tr, dskip, norm_g, kc):
    nb, S, _ = xc.shape
    Q = SSD_CHUNK
    nsteps = S // (Q * kc)
    assert nsteps % 2 == 0 and Q == LANES
    G, N, gw, gc = SSD_GROUPS, SSD_STATE, SSD_GROUP_WIDTH, SSD_GROUP_CONV
    inner = G * gw

    def both(shape, f):
        return [pl.BlockSpec(shape, lambda b, g, c: f(b, g, c)),
                pl.BlockSpec(shape, lambda b, g, c: f(b, g, nsteps - 1 - c))]

    in_specs = (both((1, kc * Q, gc), lambda b, g, c: (b, c, g))
                + both((1, kc * Q, gw), lambda b, g, c: (b, c, g))
                + both((1, kc, 4 * LANES, Q), lambda b, g, c: (b, c, 0, 0))
                + [pl.BlockSpec((1, gw), lambda b, g, c: (0, g)),
                   pl.BlockSpec((1, gw), lambda b, g, c: (0, g))])
    return pl.pallas_call(
        functools.partial(_ssd_scan_kernel, kc=kc),
        grid=(nb, G, nsteps),
        in_specs=in_specs,
        out_specs=pl.BlockSpec((1, S, gw), lambda b, g, c: (b, 0, g)),
        out_shape=jax.ShapeDtypeStruct((nb, S, inner), BF16),
        scratch_shapes=[pltpu.VMEM((2, N, gw), F32), pltpu.VMEM((S, gw), F32)],
        compiler_params=_cparams("parallel", "parallel", "arbitrary"),
        name="ssd_scan",
    )(xc, xc, p1, p1, tr, tr, dskip.reshape(1, inner), norm_g.reshape(1, inner))


def _t5_bucket_np(rel):
    half = REL_BUCKETS // 2
    exact = half // 2
    n = np.abs(rel)
    far = exact + (np.log(np.maximum(n, 1).astype(np.float32) / np.float32(exact))
                   / np.float32(math.log(REL_MAX_DISTANCE / exact)) * np.float32(half - exact)).astype(np.int32)
    far = np.minimum(far, half - 1)
    return np.where(rel > 0, half, 0) + np.where(n < exact, n, far)


def _dil_bucket_rows(tq):
    d = np.arange(2 * tq)
    rel = d - DIL_HALF
    rows = []
    for _, dilation in DIL_CONFIGS:
        rows.append(np.where(d <= 2 * DIL_HALF, _t5_bucket_np(rel * dilation), -1))
    return np.broadcast_to(np.stack(rows)[:, None, :], (len(DIL_CONFIGS), 8, 2 * tq)).astype(np.int32)


def _dil_kernel(tbl_ref, bm_ref, *refs, seq, seg, tq):
    ngroups = len(DIL_CONFIGS)
    in_refs = refs[:7 * ngroups]
    o_ref = refs[7 * ngroups]
    bias_ref, m_ref, l_ref, a_ref, kext_ref, vext_ref = refs[7 * ngroups + 1:]
    h = pl.program_id(1)
    t = pl.program_id(2)
    scale = DIL_HEAD_DIM ** -0.5

    @pl.when(t == 0)
    def _():
        m_ref[...] = jnp.full_like(m_ref, -jnp.inf)
        l_ref[...] = jnp.zeros_like(l_ref)
        a_ref[...] = jnp.zeros_like(a_ref)
        kcol = lax.broadcasted_iota(jnp.int32, (tq, 2 * tq), 1)
        for gi in range(ngroups):
            ids = bm_ref[gi]
            row = jnp.zeros(ids.shape, F32)
            for u in range(REL_BUCKETS):
                row = jnp.where(ids == u, tbl_ref[u, gi * DIL_HEADS + h], row)
            row = jnp.where(ids < 0, -jnp.inf, row)
            band = pltpu.roll(jnp.broadcast_to(row[0:1], (tq, 2 * tq)), 0, 1, stride=1, stride_axis=0)
            no_left = jnp.where(kcol >= DIL_HALF, band, -jnp.inf)
            bias_ref[gi, 0] = band
            bias_ref[gi, 1] = no_left
            bias_ref[gi, 2] = jnp.where(kcol < 2 * tq - DIL_HALF, band, -jnp.inf)
            bias_ref[gi, 3] = jnp.where(kcol < 2 * tq - DIL_HALF, no_left, -jnp.inf)

    nblk = seg // tq
    for gi, (_, r) in enumerate(DIL_CONFIGS):
        q_ref, kp, kc, kn, vp, vc, vn = in_refs[7 * gi:7 * gi + 7]
        sub_len = seq // r
        segs_per_sub = sub_len // seg
        m_res = t // segs_per_sub
        j0 = (t % segs_per_sub) * seg
        for ext, prev, cur, nxt in ((kext_ref, kp, kc, kn), (vext_ref, vp, vc, vn)):
            ext[gi, 0:DIL_HALF] = prev[0]
            ext[gi, DIL_HALF:DIL_HALF + seg] = cur[0]
            ext[gi, DIL_HALF + seg:] = nxt[0]
        seg_in_sub = t % segs_per_sub
        for jb in range(nblk):
            q = q_ref[0, jb * tq:(jb + 1) * tq, :]
            kw = kext_ref[gi, jb * tq:jb * tq + 2 * tq, :]
            vw = vext_ref[gi, jb * tq:jb * tq + 2 * tq, :]
            variant = 0
            if jb == 0:
                variant = variant + (seg_in_sub == 0).astype(jnp.int32)
            if jb == nblk - 1:
                variant = variant + 2 * (seg_in_sub == segs_per_sub - 1).astype(jnp.int32)
            s = _dot_nt(q, kw) * scale + bias_ref[gi, variant]
            mb = jnp.max(s, axis=-1, keepdims=True)
            e = jnp.exp(s - mb)
            lb = jnp.sum(e, axis=-1, keepdims=True)
            acc = _dot(e.astype(BF16), vw)
            start = (j0 + jb * tq) * r + m_res
            rows = pl.ds(start, tq, stride=r) if r > 1 else pl.ds(start, tq)
            m_old = m_ref[rows, :]
            m_new = jnp.maximum(m_old, mb)
            alpha = jnp.exp(m_old - m_new)
            beta = jnp.exp(mb - m_new)
            l_new = alpha * l_ref[rows, :] + beta * lb
            a_new = alpha * a_ref[rows, :] + beta * acc
            m_ref[rows, :] = m_new
            l_ref[rows, :] = l_new
            a_ref[rows, :] = a_new

    @pl.when(t == pl.num_programs(2) - 1)
    def _():
        o_ref[0] = (a_ref[...] / l_ref[...]).astype(o_ref.dtype)


def dilated_attention(qkvs, rel_bias):
    nb, S, _ = qkvs[0].shape
    hd = DIL_HEAD_DIM
    tq = 2 * DIL_HALF
    seg = S // DIL_SEGS
    assert seg % tq == 0 and all(w == 2 * DIL_HALF * r and (S // r) % seg == 0 for w, r in DIL_CONFIGS)
    nhb = seg // DIL_HALF
    last_hb = S // DIL_HALF - 1
    in_specs = [pl.BlockSpec(memory_space=pltpu.SMEM),
                pl.BlockSpec((len(DIL_CONFIGS), 8, 2 * tq), lambda b, h, t: (0, 0, 0))]
    args = [rel_bias, jnp.asarray(_dil_bucket_rows(tq))]
    for g in range(len(DIL_CONFIGS)):
        in_specs.append(pl.BlockSpec((1, seg, hd), lambda b, h, t: (b, t, h)))
        for part in (1, 2):
            off = part * DIL_HEADS
            in_specs += [
                pl.BlockSpec((1, DIL_HALF, hd), lambda b, h, t, off=off: (b, jnp.maximum(t * nhb - 1, 0), off + h)),
                pl.BlockSpec((1, seg, hd), lambda b, h, t, off=off: (b, t, off + h)),
                pl.BlockSpec((1, DIL_HALF, hd), lambda b, h, t, off=off: (b, jnp.minimum((t + 1) * nhb, last_hb), off + h)),
            ]
        args += [qkvs[g]] * 7
    kern = functools.partial(_dil_kernel, seq=S, seg=seg, tq=tq)
    return pl.pallas_call(
        kern,
        grid=(nb, DIL_HEADS, DIL_SEGS),
        in_specs=in_specs,
        out_specs=pl.BlockSpec((1, S, hd), lambda b, h, t: (b, 0, h)),
        out_shape=jax.ShapeDtypeStruct((nb, S, DIL_HEADS * hd), BF16),
        scratch_shapes=[pltpu.VMEM((len(DIL_CONFIGS), 4, tq, 2 * tq), F32),
                        pltpu.VMEM((S, hd), F32), pltpu.VMEM((S, hd), F32), pltpu.VMEM((S, hd), F32),
                        pltpu.VMEM((len(DIL_CONFIGS), seg + 2 * DIL_HALF, hd), BF16),
                        pltpu.VMEM((len(DIL_CONFIGS), seg + 2 * DIL_HALF, hd), BF16)],
        compiler_params=_cparams("parallel", "parallel", "arbitrary"),
        name="dil_attn",
    )(*args)


def _tile(n, pref):
    return pref if n % pref == 0 else n


def _ssd_lane_layout(v):
    v = v.reshape(2, SSD_GROUPS, SSD_HEADS_PER_GROUP).transpose(1, 0, 2)
    v = jnp.pad(v, ((0, 0), (0, 0), (0, GROUP_LANES // 2 - SSD_HEADS_PER_GROUP)))
    return v.reshape(1, LANES)


def _ssd_mixer(x2, h3, kmem, vmem_, w_in, conv_w, conv_b, dt_bias, a_log, d_skip, norm_g, w_out, mem_q_gain):
    nb, S, D = h3.shape
    inner = SSD_GROUPS * SSD_HEADS_PER_GROUP * SSD_HEAD_DIM
    conv_ch = inner + 2 * SSD_GROUPS * SSD_STATE
    nheads = SSD_GROUPS * SSD_HEADS_PER_GROUP
    mw = kmem.shape[2]
    tm = _tile(S, 1024)
    w_bf = w_in.astype(BF16)

    def grouped(t):
        lead = t.shape[:-1]
        parts = [t[..., :inner].reshape(*lead, SSD_GROUPS, SSD_GROUP_WIDTH),
                 t[..., inner:inner + SSD_GROUPS * SSD_STATE].reshape(*lead, SSD_GROUPS, SSD_STATE),
                 t[..., inner + SSD_GROUPS * SSD_STATE:].reshape(*lead, SSD_GROUPS, SSD_STATE)]
        return jnp.concatenate(parts, axis=-1).reshape(*lead, conv_ch)

    w_p1 = jnp.concatenate([w_bf[:, :inner], grouped(w_bf[:, inner:inner + conv_ch])], axis=1)
    conv_w, conv_b = grouped(conv_w), grouped(conv_b)
    p1 = proj(h3, w_p1, 0, inner + conv_ch, None, out_dtype=F32, tm=tm, tn=1024)[:, 0]
    w_dt = w_in[:, inner + conv_ch:inner + conv_ch + 2 * nheads]
    w_dt = w_dt.reshape(D, 2, SSD_GROUPS, SSD_HEADS_PER_GROUP).transpose(0, 2, 1, 3)
    w_dt = jnp.pad(w_dt, ((0, 0), (0, 0), (0, 0), (0, GROUP_LANES // 2 - SSD_HEADS_PER_GROUP)))
    q_off = inner + conv_ch + 2 * nheads
    w_side = jnp.concatenate([w_bf[:, q_off:], w_dt.reshape(D, LANES).astype(BF16)], axis=1)
    q_mem, dtr = side_proj(h3, w_side, mem_q_gain, tm)

    tr = ssd_prep(dtr, _ssd_lane_layout(dt_bias), _ssd_lane_layout(a_log))
    xc = ssd_conv(p1, inner, conv_w, conv_b, ts=tm, tc=512)
    dskip = jnp.repeat(d_skip, SSD_HEAD_DIM)
    nchunks = S // SSD_CHUNK
    kc = next(k for k in (4, 2, 1) if nchunks % (2 * k) == 0)
    y = ssd_scan(xc, p1, tr, dskip, norm_g, kc)
    o_mem = mem_attention(q_mem, kmem, vmem_, tq=tm)
    w_out_bf = w_out.astype(BF16)
    return outproj(x2, y.reshape(nb * S, inner), o_mem.reshape(nb * S, mw),
                   w_out_bf[:inner], w_out_bf[inner:], tm=_tile(nb * S, 512), tn=1024)


def _dil_mixer(x2, h3, kmem, vmem_, w_in, q_gain, k_gain, w_out, rel_bias, mem_q_gain):
    nb, S, D = h3.shape
    width = DIL_HEADS * DIL_HEAD_DIM
    mw = kmem.shape[2]
    tm = _tile(S, 1024)
    w_bf = w_in.astype(BF16)
    qkvs = []
    for g, (_, r) in enumerate(DIL_CONFIGS):
        gains = jnp.stack([jnp.tile(q_gain[g], DIL_HEADS), jnp.tile(k_gain[g], DIL_HEADS),
                           jnp.ones((width,), F32)]).reshape(3, 1, width)
        out = proj(h3, w_bf, g * 3 * width, 3 * width, gains, out_dtype=BF16, tm=tm, tn=width,
                   norm_width=DIL_HEAD_DIM, norm_tiles=2, r=r)
        qkvs.append(out.reshape(nb, S, 3 * width))
    q_off = len(DIL_CONFIGS) * 3 * width
    hd = mem_q_gain.shape[0]
    qg = jnp.tile(mem_q_gain, mw // hd).reshape(1, 1, mw)
    q_mem = proj(h3, w_bf, q_off, mw, qg, out_dtype=BF16, tm=tm, tn=mw, norm_width=hd)[:, 0]
    o = dilated_attention(qkvs, rel_bias)
    o_mem = mem_attention(q_mem, kmem, vmem_, tq=tm)
    w_out_bf = w_out.astype(BF16)
    return outproj(x2, o.reshape(nb * S, width), o_mem.reshape(nb * S, mw),
                   w_out_bf[:width], w_out_bf[width:], tm=_tile(nb * S, 512), tn=1024)


def kernel(x, mem, rel_bias, ffn_norm, ffn_w_in, ffn_w_out, mix_norm, mem_norm, mem_w_kv, mem_q_gain, mem_k_gain, ssd_w_in, ssd_conv_w, ssd_conv_b, ssd_dt_bias, ssd_A_log, ssd_D, ssd_norm, ssd_w_out, dil_w_in, dil_q_gain, dil_k_gain, dil_w_out):
    nb, S, D = x.shape
    depth = ffn_norm.shape[0]
    T = nb * S
    x2 = x.reshape(T, D)
    d_ff = ffn_w_out.shape[2]
    tm_ffn, tf = _tile(T, 1024), _tile(d_ff, 512)
    tm_ffn_norm, tf_norm = _tile(T, 512), tf
    ffn_w_in = ffn_w_in.astype(BF16)
    ffn_w_out = ffn_w_out.astype(BF16)
    for i in range(depth):
        x2, h2 = ffn(x2, ffn_norm[i, 0], ffn_w_in, ffn_w_out, i, 0, tm_ffn_norm, tf_norm, next_gain=mix_norm[i])
        h3 = h2.reshape(nb, S, D)
        kmem, vmem_ = mem_kv(mem, mem_norm[i], mem_w_kv[i].astype(BF16), mem_k_gain[i])
        j = i // 2
        if i % 2 == 0:
            x2 = _ssd_mixer(x2, h3, kmem, vmem_, ssd_w_in[j], ssd_conv_w[j], ssd_conv_b[j], ssd_dt_bias[j],
                            ssd_A_log[j], ssd_D[j], ssd_norm[j], ssd_w_out[j], mem_q_gain[i])
        else:
            x2 = _dil_mixer(x2, h3, kmem, vmem_, dil_w_in[j], dil_q_gain[j], dil_k_gain[j], dil_w_out[j],
                            rel_bias, mem_q_gain[i])
        x2 = ffn(x2, ffn_norm[i, 1], ffn_w_in, ffn_w_out, i, 1, tm_ffn, tf)
    return x2.reshape(nb, S, D)
```

```python
import functools
import math

import jax
import jax.numpy as jnp
import numpy as np
from jax import lax
from jax.experimental import pallas as pl
from jax.experimental.pallas import tpu as pltpu

F32 = jnp.float32
BF16 = jnp.bfloat16
EPS = 1e-6

MEM_HEADS = 4
SSD_HEAD_DIM = 64
SSD_GROUPS = 8
SSD_HEADS_PER_GROUP = 6
SSD_STATE = 128
SSD_CONV = 5
SSD_CHUNK = 128
DIL_CONFIGS = ((128, 1), (512, 4), (2048, 16))
DIL_HEADS = 8
DIL_HEAD_DIM = 128
DIL_HALF = 64
DIL_SEGS = 16
REL_BUCKETS = 32
REL_MAX_DISTANCE = 1024

LANES = 128
VMEM_LIMIT_BYTES = 62 * 1024 * 1024
GROUP_LANES = 16


def _cparams(*sem):
    return pltpu.CompilerParams(dimension_semantics=sem, vmem_limit_bytes=VMEM_LIMIT_BYTES)


def _rms(x, gain):
    ms = jnp.mean(x * x, axis=-1, keepdims=True)
    return x * lax.rsqrt(ms + EPS) * gain


def _dot(a, b):
    return jnp.dot(a, b, preferred_element_type=F32)


def _dot_nt(a, b):
    return lax.dot_general(a, b, (((1,), (1,)), ((), ())), preferred_element_type=F32)


FFN_SUB_ROWS = 512


def _ffn_kernel(x_ref, g_ref, wg_ref, wu_ref, wo_ref, *rest, emit_norm):
    if emit_norm:
        g2_ref, o_ref, hn_ref = rest
        h_ref = hn_ref
    else:
        o_ref, h_ref = rest
    j = pl.program_id(1)

    tm = x_ref.shape[0]
    sub = min(tm, FFN_SUB_ROWS)

    @pl.when(j == 0)
    def _():
        for r0 in range(0, tm, sub // 2):
            rows = slice(r0, r0 + sub // 2)
            h_ref[rows, :] = _rms(x_ref[rows, :], g_ref[...]).astype(BF16)
            o_ref[rows, :] = x_ref[rows, :]

    for r0 in range(0, tm, sub):
        h = h_ref[r0:r0 + sub, :]
        gate = _dot(h, wg_ref[...])
        up = _dot(h, wu_ref[...])
        a = (jax.nn.silu(gate) * up * 0.5).astype(BF16)
        o_ref[r0:r0 + sub, :] += _dot(a, wo_ref[...])

    if emit_norm:
        @pl.when(j == pl.num_programs(1) - 1)
        def _():
            for r0 in range(0, tm, sub // 2):
                rows = slice(r0, r0 + sub // 2)
                hn_ref[rows, :] = _rms(o_ref[rows, :], g2_ref[...]).astype(hn_ref.dtype)


def ffn(x2, gain, w_in, w_out, layer, which, tm, tf, next_gain=None):
    T, D = x2.shape
    F = w_out.shape[2]
    nf = F // tf
    emit_norm = next_gain is not None
    row_spec = pl.BlockSpec((tm, D), lambda i, j: (i, 0))
    vec_spec = pl.BlockSpec((1, D), lambda i, j: (0, 0))
    in_specs = [row_spec, vec_spec,
                pl.BlockSpec((None, None, D, tf), lambda i, j: (layer, which, 0, j)),
                pl.BlockSpec((None, None, D, tf), lambda i, j: (layer, which, 0, j + nf)),
                pl.BlockSpec((None, None, tf, D), lambda i, j: (layer, which, j, 0))]
    args = [x2, gain.reshape(1, D), w_in, w_in, w_out]
    out_specs, out_shape = row_spec, jax.ShapeDtypeStruct((T, D), F32)
    if emit_norm:
        in_specs.append(vec_spec)
        args.append(next_gain.reshape(1, D))
        out_specs, out_shape = [row_spec, row_spec], [out_shape, jax.ShapeDtypeStruct((T, D), BF16)]
    return pl.pallas_call(
        functools.partial(_ffn_kernel, emit_norm=emit_norm),
        grid=(T // tm, nf),
        in_specs=in_specs,
        out_specs=out_specs,
        out_shape=out_shape,
        scratch_shapes=[] if emit_norm else [pltpu.VMEM((tm, D), BF16)],
        compiler_params=_cparams("parallel", "arbitrary"),
        name="ffn",
    )(*args)


PROJ_CHUNK = 256


def _proj_kernel(h_ref, w_ref, g_ref, o_ref, *scratch, norm_width, norm_tiles, r):
    tm, tn = h_ref.shape[1], w_ref.shape[1]
    rows_per = tm // r
    nw = norm_width if norm_width else PROJ_CHUNK
    assert PROJ_CHUNK % nw == 0 or nw % PROJ_CHUNK == 0
    chunk = max(PROJ_CHUNK, nw)

    def emit(normed):
        for c0 in range(0, tn, chunk):
            res = _dot(h_ref[0], w_ref[:, c0:c0 + chunk])
            if r == 1:
                if not normed:
                    o_ref[0, 0, :, c0:c0 + chunk] = res.astype(o_ref.dtype)
                    continue
                for c in range(c0, c0 + chunk, nw):
                    blk = res[:, c - c0:c - c0 + nw]
                    o_ref[0, 0, :, c:c + nw] = _rms(blk, g_ref[0, :, c:c + nw]).astype(o_ref.dtype)
                continue
            acc_ref, = scratch
            for c in range(c0, c0 + chunk, LANES):
                acc_ref[c // LANES] = res[:, c - c0:c - c0 + LANES]
            for m in range(r):
                rows = pl.ds(m, rows_per, stride=r)
                for c in range(c0, c0 + chunk, LANES):
                    blk = acc_ref[c // LANES, rows, :]
                    if normed:
                        blk = _rms(blk, g_ref[0, :, c:c + LANES])
                    o_ref[0, m, :, c:c + LANES] = blk.astype(o_ref.dtype)

    if r > 1:
        assert norm_width in (0, LANES)
    if norm_width == 0:
        emit(False)
    elif norm_tiles is None:
        emit(True)
    else:
        j = pl.program_id(2)
        pl.when(j < norm_tiles)(lambda: emit(True))
        pl.when(j >= norm_tiles)(lambda: emit(False))


def proj(h3, w, col0, ncols, gains, *, out_dtype, tm, tn, norm_width=0, norm_tiles=None, r=1):
    nb, S, K = h3.shape
    assert col0 % tn == 0 and ncols % tn == 0 and S % tm == 0 and tm % (8 * r) == 0
    j0 = col0 // tn
    nj = ncols // tn
    if gains is None:
        gains = jnp.ones((nj, 1, tn), F32)
    scratch = [] if r == 1 else [pltpu.VMEM((tn // LANES, tm, LANES), F32)]
    kern = functools.partial(_proj_kernel, norm_width=norm_width, norm_tiles=norm_tiles, r=r)
    return pl.pallas_call(
        kern,
        grid=(nb, S // tm, nj),
        in_specs=[pl.BlockSpec((1, tm, K), lambda b, i, j: (b, i, 0)),
                  pl.BlockSpec((K, tn), lambda b, i, j: (0, j0 + j)),
                  pl.BlockSpec((1, 1, tn), lambda b, i, j: (j, 0, 0))],
        out_specs=pl.BlockSpec((1, r, tm // r, tn), lambda b, i, j: (b, 0, i, j)),
        out_shape=jax.ShapeDtypeStruct((nb, r, S // r, ncols), out_dtype),
        scratch_shapes=scratch,
        compiler_params=_cparams("parallel", "parallel", "arbitrary"),
        name="proj",
    )(h3, w, gains)


def _side_proj_kernel(h_ref, w_ref, g_ref, q_ref, dt_ref, *, head_dim):
    mw = q_ref.shape[2]
    for c in range(mw // head_dim):
        cols = slice(c * head_dim, (c + 1) * head_dim)
        q_ref[0, :, cols] = _rms(_dot(h_ref[0], w_ref[:, cols]), g_ref[...]).astype(q_ref.dtype)
    dt_ref[0] = _dot(h_ref[0], w_ref[:, mw:])


def side_proj(h3, w, q_gain, tm):
    nb, S, K = h3.shape
    head_dim = q_gain.shape[0]
    n = w.shape[1]
    mw = n - LANES
    return pl.pallas_call(
        functools.partial(_side_proj_kernel, head_dim=head_dim),
        grid=(nb, S // tm),
        in_specs=[pl.BlockSpec((1, tm, K), lambda b, i: (b, i, 0)),
                  pl.BlockSpec((K, n), lambda b, i: (0, 0)),
                  pl.BlockSpec((1, head_dim), lambda b, i: (0, 0))],
        out_specs=[pl.BlockSpec((1, tm, mw), lambda b, i: (b, i, 0)),
                   pl.BlockSpec((1, tm, LANES), lambda b, i: (b, i, 0))],
        out_shape=[jax.ShapeDtypeStruct((nb, S, mw), BF16), jax.ShapeDtypeStruct((nb, S, LANES), F32)],
        compiler_params=_cparams("parallel", "parallel"),
        name="side_proj",
    )(h3, w, q_gain.reshape(1, head_dim))


def _outproj_kernel(x_ref, a1_ref, a2_ref, w1_ref, w2_ref, o_ref):
    for c0 in range(0, o_ref.shape[1], PROJ_CHUNK):
        cols = slice(c0, c0 + PROJ_CHUNK)
        o_ref[:, cols] = x_ref[:, cols] + _dot(a1_ref[...], w1_ref[:, cols]) + _dot(a2_ref[...], w2_ref[:, cols])


def outproj(x2, a1, a2, w1, w2, tm, tn):
    T, D = x2.shape
    k1, k2 = a1.shape[1], a2.shape[1]
    return pl.pallas_call(
        _outproj_kernel,
        grid=(T // tm, D // tn),
        in_specs=[pl.BlockSpec((tm, tn), lambda i, j: (i, j)),
                  pl.BlockSpec((tm, k1), lambda i, j: (i, 0)),
                  pl.BlockSpec((tm, k2), lambda i, j: (i, 0)),
                  pl.BlockSpec((k1, tn), lambda i, j: (0, j)),
                  pl.BlockSpec((k2, tn), lambda i, j: (0, j))],
        out_specs=pl.BlockSpec((tm, tn), lambda i, j: (i, j)),
        out_shape=jax.ShapeDtypeStruct((T, D), F32),
        compiler_params=_cparams("parallel", "arbitrary"),
        name="outproj",
    )(x2, a1, a2, w1, w2)


def _memkv_kernel(mem_ref, g_ref, w_ref, kg_ref, k_ref, v_ref):
    mw = k_ref.shape[2]
    hd = kg_ref.shape[1]
    memn = _rms(mem_ref[0], g_ref[...]).astype(BF16)
    kv = _dot(memn, w_ref[...])
    for hh in range(mw // hd):
        cols = slice(hh * hd, (hh + 1) * hd)
        k_ref[0, :, cols] = _rms(kv[:, cols], kg_ref[...]).astype(BF16)
    v_ref[0] = kv[:, mw:].astype(BF16)


def mem_kv(mem, mem_gain, w_kv, k_gain):
    nb, M, D = mem.shape
    mw = w_kv.shape[1] // 2
    hd = k_gain.shape[0]
    out = jax.ShapeDtypeStruct((nb, M, mw), BF16)
    return pl.pallas_call(
        _memkv_kernel,
        grid=(nb,),
        in_specs=[pl.BlockSpec((1, M, D), lambda b: (b, 0, 0)),
                  pl.BlockSpec((1, D), lambda b: (0, 0)),
                  pl.BlockSpec((D, 2 * mw), lambda b: (0, 0)),
                  pl.BlockSpec((1, hd), lambda b: (0, 0))],
        out_specs=[pl.BlockSpec((1, M, mw), lambda b: (b, 0, 0))] * 2,
        out_shape=[out, out],
        compiler_params=_cparams("parallel"),
        name="mem_kv",
    )(mem, mem_gain.reshape(1, D), w_kv, k_gain.reshape(1, hd))


def _memattn_kernel(q_ref, k_ref, v_ref, o_ref, *, heads):
    hd = q_ref.shape[2] // heads
    scale = hd ** -0.5
    for hh in range(heads):
        cols = slice(hh * hd, (hh + 1) * hd)
        s = _dot_nt(q_ref[0, :, cols], k_ref[0, :, cols]) * scale
        e = jnp.exp(s - jnp.max(s, axis=-1, keepdims=True))
        p = e / jnp.sum(e, axis=-1, keepdims=True)
        o_ref[0, :, cols] = _dot(p.astype(BF16), v_ref[0, :, cols]).astype(o_ref.dtype)


def mem_attention(q, k, v, tq):
    nb, S, mw = q.shape
    M = k.shape[1]
    return pl.pallas_call(
        functools.partial(_memattn_kernel, heads=MEM_HEADS),
        grid=(nb, S // tq),
        in_specs=[pl.BlockSpec((1, tq, mw), lambda b, i: (b, i, 0)),
                  pl.BlockSpec((1, M, mw), lambda b, i: (b, 0, 0)),
                  pl.BlockSpec((1, M, mw), lambda b, i: (b, 0, 0))],
        out_specs=pl.BlockSpec((1, tq, mw), lambda b, i: (b, i, 0)),
        out_shape=jax.ShapeDtypeStruct((nb, S, mw), BF16),
        compiler_params=_cparams("parallel", "parallel"),
        name="mem_attn",
    )(q, k, v)


SSD_GROUP_WIDTH = SSD_HEADS_PER_GROUP * SSD_HEAD_DIM
SSD_GROUP_CONV = SSD_GROUP_WIDTH + 2 * SSD_STATE
SSD_GROUP_COLS = SSD_GROUP_CONV + SSD_GROUP_WIDTH
HALO_ROWS = 16


def _ssd_inproj_kernel(hp_ref, h_ref, hn_ref, w_ref, cw_ref, cb_ref, o_ref, ext_ref):
    i = pl.program_id(1)
    tm = h_ref.shape[1]
    pad = SSD_CONV // 2
    lo = HALO_ROWS - pad
    has_prev = i > 0
    has_next = i < pl.num_programs(1) - 1
    for c0 in range(0, SSD_GROUP_COLS, PROJ_CHUNK):
        wc = w_ref[:, c0:c0 + PROJ_CHUNK]
        res = _dot(h_ref[0], wc)
        conv_cols = min(max(SSD_GROUP_CONV - c0, 0), PROJ_CHUNK)
        if conv_cols:
            res_prev = jnp.where(has_prev, _dot(hp_ref[0], wc), 0.0)
            res_next = jnp.where(has_next, _dot(hn_ref[0], wc), 0.0)
        for c in range(c0, c0 + PROJ_CHUNK, LANES):
            part = slice(c - c0, c - c0 + LANES)
            if c >= SSD_GROUP_CONV:
                o_ref[0, :, c:c + LANES] = jax.nn.silu(res[:, part])
                continue
            s = c // LANES
            ext_ref[s, 0:HALO_ROWS] = res_prev[:, part]
            ext_ref[s, HALO_ROWS:HALO_ROWS + tm] = res[:, part]
            ext_ref[s, HALO_ROWS + tm:] = res_next[:, part]
            acc = cb_ref[:, c:c + LANES] + cw_ref[0:1, c:c + LANES] * ext_ref[s, lo:lo + tm, :]
            for k in range(1, SSD_CONV):
                acc = acc + cw_ref[k:k + 1, c:c + LANES] * ext_ref[s, lo + k:lo + k + tm, :]
            o_ref[0, :, c:c + LANES] = jax.nn.silu(acc)


def ssd_inproj(h3, w, conv_w, conv_b, tm):
    nb, S, K = h3.shape
    G, gc, cols = SSD_GROUPS, SSD_GROUP_CONV, SSD_GROUP_COLS
    assert PROJ_CHUNK % LANES == 0 and gc % LANES == 0 and cols % PROJ_CHUNK == 0 and tm % HALO_ROWS == 0
    nh = tm // HALO_ROWS
    last = S // HALO_ROWS - 1
    return pl.pallas_call(
        _ssd_inproj_kernel,
        grid=(nb, S // tm, G),
        in_specs=[pl.BlockSpec((1, HALO_ROWS, K), lambda b, i, g: (b, jnp.maximum(i * nh - 1, 0), 0)),
                  pl.BlockSpec((1, tm, K), lambda b, i, g: (b, i, 0)),
                  pl.BlockSpec((1, HALO_ROWS, K), lambda b, i, g: (b, jnp.minimum((i + 1) * nh, last), 0)),
                  pl.BlockSpec((K, cols), lambda b, i, g: (0, g)),
                  pl.BlockSpec((SSD_CONV, gc), lambda b, i, g: (0, g)),
                  pl.BlockSpec((1, gc), lambda b, i, g: (0, g))],
        out_specs=pl.BlockSpec((1, tm, cols), lambda b, i, g: (b, i, g)),
        out_shape=jax.ShapeDtypeStruct((nb, S, G * cols), F32),
        scratch_shapes=[pltpu.VMEM((gc // LANES, tm + 2 * HALO_ROWS, LANES), F32)],
        compiler_params=_cparams("parallel", "parallel", "arbitrary"),
        name="ssd_inproj",
    )(h3, h3, h3, w, conv_w, conv_b.reshape(1, G * gc))


LOG2E = math.log2(math.e)


def _ssd_prep_kernel(dtr_ref, bias_ref, alog_ref, tr_ref):
    Q = SSD_CHUNK
    ii = lax.broadcasted_iota(jnp.int32, (Q, Q), 0)
    jj = lax.broadcasted_iota(jnp.int32, (Q, Q), 1)
    tril = (jj <= ii).astype(F32)
    triu = (jj >= ii).astype(F32)
    lane = lax.broadcasted_iota(jnp.int32, (Q, LANES), 1)
    is_bwd = (lane % GROUP_LANES) >= GROUP_LANES // 2
    hi = lax.Precision.HIGHEST
    for i in range(tr_ref.shape[1]):
        x = dtr_ref[0, i * Q:(i + 1) * Q, :] + bias_ref[...]
        dt = jnp.maximum(x, 0.0) + jnp.log1p(jnp.exp(-jnp.abs(x)))
        a = dt * (-jnp.exp(alog_ref[...]))
        cs_fwd = jnp.dot(tril, a, precision=hi, preferred_element_type=F32)
        cs_bwd = jnp.dot(triu, a, precision=hi, preferred_element_type=F32)
        cs = jnp.where(is_bwd, cs_bwd, cs_fwd)
        last = jnp.where(is_bwd[0:1], cs[0:1], cs[Q - 1:Q])
        cs2 = cs * LOG2E
        tr_ref[0, i, 0 * LANES:1 * LANES] = cs2.T
        tr_ref[0, i, 1 * LANES:2 * LANES] = (cs2 - jnp.log2(dt)).T
        tr_ref[0, i, 2 * LANES:3 * LANES] = (jnp.exp(last - cs) * dt).T
        tr_ref[0, i, 3 * LANES:4 * LANES] = jnp.broadcast_to(jnp.exp(last), (Q, LANES)).T


def ssd_prep(dtr, bias_l, alog_l):
    nb, S, _ = dtr.shape
    Q = SSD_CHUNK
    nc = S // Q
    kp = next(k for k in (8, 4, 2, 1) if nc % k == 0)
    vec_spec = pl.BlockSpec((1, LANES), lambda b, c: (0, 0))
    return pl.pallas_call(
        _ssd_prep_kernel,
        grid=(nb, nc // kp),
        in_specs=[pl.BlockSpec((1, kp * Q, LANES), lambda b, c: (b, c, 0)), vec_spec, vec_spec],
        out_specs=pl.BlockSpec((1, kp, 4 * LANES, Q), lambda b, c: (b, c, 0, 0)),
        out_shape=jax.ShapeDtypeStruct((nb, nc, 4 * LANES, Q), F32),
        compiler_params=_cparams("parallel", "parallel"),
        name="ssd_prep",
    )(dtr, bias_l, alog_l)


def _ssd_direction(d, g, sub, xc_ref, tr_ref, st_ref):
    Q = SSD_CHUNK
    P, N, gw = SSD_HEAD_DIM, SSD_STATE, SSD_GROUP_WIDTH
    rows = slice(sub * Q, (sub + 1) * Q)
    xs = xc_ref[0, rows, 0:gw]
    bm = xc_ref[0, rows, gw:gw + N]
    cm = xc_ref[0, rows, gw + N:gw + 2 * N]
    cb = _dot_nt(cm.astype(BF16), bm.astype(BF16))
    bt = bm.T

    def tr_rows(part):
        start = pl.multiple_of(part * LANES + GROUP_LANES * g, GROUP_LANES)
        return tr_ref[0, sub, pl.ds(start, GROUP_LANES), :]

    cs2T, rowT, wT, cdT = (tr_rows(part) for part in range(4))
    ii = lax.broadcasted_iota(jnp.int32, (Q, Q), 0)
    jj = lax.broadcasted_iota(jnp.int32, (Q, Q), 1)
    causal = (jj <= ii) if d == 0 else (jj >= ii)
    lo = lax.broadcasted_iota(jnp.int32, (Q, LANES), 1) < P
    st = st_ref[d]
    ys, sts = [], []
    for p in range(SSD_HEADS_PER_GROUP // 2):
        cols = slice(p * LANES, (p + 1) * LANES)
        ms, ss = [], []
        k0 = d * (GROUP_LANES // 2) + 2 * p
        for k in (k0, k0 + 1):
            col2 = jnp.broadcast_to(cs2T[k:k + 1, :], (Q, Q)).T
            decay_dt = jnp.exp2(jnp.where(causal, col2 - rowT[k:k + 1, :], -jnp.inf))
            ms.append((cb * decay_dt).astype(BF16))
            ms.append((cm * jnp.exp2(col2)).astype(BF16))
            ss.append((bt * wT[k:k + 1, :]).astype(BF16))
        cd = jnp.where(lo[0:1], cdT[k0:k0 + 1, :], cdT[k0 + 1:k0 + 2, :])
        xp, sp = xs[:, cols], st[:, cols]
        x_lo, x_hi = jnp.where(lo, xp, 0.0).astype(BF16), jnp.where(lo, 0.0, xp).astype(BF16)
        s_lo, s_hi = jnp.where(lo, sp, 0.0).astype(BF16), jnp.where(lo, 0.0, sp).astype(BF16)
        ys.append(_dot(jnp.concatenate(ms, axis=1), jnp.concatenate([x_lo, s_lo, x_hi, s_hi], axis=0)))
        sts.append(cd * sp + _dot(jnp.concatenate(ss, axis=1), jnp.concatenate([x_lo, x_hi], axis=0)))
    st_ref[d] = jnp.concatenate(sts, axis=1)
    return jnp.concatenate(ys, axis=1)


def _ssd_scan_kernel(xcf, xcb, trf, trb, dskip_ref, ng_ref, y_ref, st_ref, ysum_ref, *, kc):
    g = pl.program_id(1)
    c = pl.program_id(2)
    nsteps = pl.num_programs(2)
    Q = SSD_CHUNK

    @pl.when(c == 0)
    def _():
        st_ref[...] = jnp.zeros_like(st_ref)
        ysum_ref[...] = jnp.zeros_like(ysum_ref)

    def finish(y_dir, row0, xc_ref, sub):
        rows = pl.ds(pl.multiple_of(row0, Q), Q)
        blk = slice(sub * Q, (sub + 1) * Q)
        tot = ysum_ref[rows, :] + y_dir + dskip_ref[...] * xc_ref[0, blk, 0:SSD_GROUP_WIDTH]
        ysum_ref[rows, :] = y_dir
        gated = tot * xc_ref[0, blk, SSD_GROUP_CONV:SSD_GROUP_COLS]
        y_ref[0, rows, :] = _rms(gated, ng_ref[...]).astype(y_ref.dtype)

    for i in range(kc):
        y_f = _ssd_direction(0, g, i, xcf, trf, st_ref)
        finish(y_f, (c * kc + i) * Q, xcf, i)
        y_b = _ssd_direction(1, g, kc - 1 - i, xcb, trb, st_ref)
        finish(y_b, ((nsteps - 1 - c) * kc + kc - 1 - i) * Q, xcb, kc - 1 - i)


def ssd_scan(xc, tr, dskip, norm_g, kc):
    nb, S, _ = xc.shape
    Q = SSD_CHUNK
    nsteps = S // (Q * kc)
    assert nsteps % 2 == 0 and Q == LANES
    G, N, gw = SSD_GROUPS, SSD_STATE, SSD_GROUP_WIDTH
    inner = G * gw

    def both(shape, f):
        return [pl.BlockSpec(shape, lambda b, g, c: f(b, g, c)),
                pl.BlockSpec(shape, lambda b, g, c: f(b, g, nsteps - 1 - c))]

    in_specs = (both((1, kc * Q, SSD_GROUP_COLS), lambda b, g, c: (b, c, g))
                + both((1, kc, 4 * LANES, Q), lambda b, g, c: (b, c, 0, 0))
                + [pl.BlockSpec((1, gw), lambda b, g, c: (0, g)),
                   pl.BlockSpec((1, gw), lambda b, g, c: (0, g))])
    return pl.pallas_call(
        functools.partial(_ssd_scan_kernel, kc=kc),
        grid=(nb, G, nsteps),
        in_specs=in_specs,
        out_specs=pl.BlockSpec((1, S, gw), lambda b, g, c: (b, 0, g)),
        out_shape=jax.ShapeDtypeStruct((nb, S, inner), BF16),
        scratch_shapes=[pltpu.VMEM((2, N, gw), F32), pltpu.VMEM((S, gw), F32)],
        compiler_params=_cparams("parallel", "parallel", "arbitrary"),
        name="ssd_scan",
    )(xc, xc, tr, tr, dskip.reshape(1, inner), norm_g.reshape(1, inner))


def _t5_bucket_np(rel):
    half = REL_BUCKETS // 2
    exact = half // 2
    n = np.abs(rel)
    far = exact + (np.log(np.maximum(n, 1).astype(np.float32) / np.float32(exact))
                   / np.float32(math.log(REL_MAX_DISTANCE / exact)) * np.float32(half - exact)).astype(np.int32)
    far = np.minimum(far, half - 1)
    return np.where(rel > 0, half, 0) + np.where(n < exact, n, far)


def _dil_bucket_rows(tq):
    d = np.arange(2 * tq)
    rel = d - DIL_HALF
    rows = []
    for _, dilation in DIL_CONFIGS:
        rows.append(np.where(d <= 2 * DIL_HALF, _t5_bucket_np(rel * dilation), -1))
    return np.broadcast_to(np.stack(rows)[:, None, :], (len(DIL_CONFIGS), 8, 2 * tq)).astype(np.int32)


def _dil_kernel(tbl_ref, bm_ref, *refs, seq, seg, tq):
    ngroups = len(DIL_CONFIGS)
    in_refs = refs[:7 * ngroups]
    o_ref = refs[7 * ngroups]
    bias_ref, m_ref, l_ref, a_ref, kext_ref, vext_ref = refs[7 * ngroups + 1:]
    h = pl.program_id(1)
    t = pl.program_id(2)
    scale = DIL_HEAD_DIM ** -0.5

    @pl.when(t == 0)
    def _():
        m_ref[...] = jnp.full_like(m_ref, -jnp.inf)
        l_ref[...] = jnp.zeros_like(l_ref)
        a_ref[...] = jnp.zeros_like(a_ref)
        kcol = lax.broadcasted_iota(jnp.int32, (tq, 2 * tq), 1)
        for gi in range(ngroups):
            ids = bm_ref[gi]
            row = jnp.zeros(ids.shape, F32)
            for u in range(REL_BUCKETS):
                row = jnp.where(ids == u, tbl_ref[u, gi * DIL_HEADS + h], row)
            row = jnp.where(ids < 0, -jnp.inf, row)
            band = pltpu.roll(jnp.broadcast_to(row[0:1], (tq, 2 * tq)), 0, 1, stride=1, stride_axis=0)
            no_left = jnp.where(kcol >= DIL_HALF, band, -jnp.inf)
            bias_ref[gi, 0] = band
            bias_ref[gi, 1] = no_left
            bias_ref[gi, 2] = jnp.where(kcol < 2 * tq - DIL_HALF, band, -jnp.inf)
            bias_ref[gi, 3] = jnp.where(kcol < 2 * tq - DIL_HALF, no_left, -jnp.inf)

    nblk = seg // tq
    for gi, (_, r) in enumerate(DIL_CONFIGS):
        q_ref, kp, kc, kn, vp, vc, vn = in_refs[7 * gi:7 * gi + 7]
        sub_len = seq // r
        segs_per_sub = sub_len // seg
        m_res = t // segs_per_sub
        j0 = (t % segs_per_sub) * seg
        for ext, prev, cur, nxt in ((kext_ref, kp, kc, kn), (vext_ref, vp, vc, vn)):
            ext[gi, 0:DIL_HALF] = prev[0]
            ext[gi, DIL_HALF:DIL_HALF + seg] = cur[0]
            ext[gi, DIL_HALF + seg:] = nxt[0]
        seg_in_sub = t % segs_per_sub
        for jb in range(nblk):
            q = q_ref[0, jb * tq:(jb + 1) * tq, :]
            kw = kext_ref[gi, jb * tq:jb * tq + 2 * tq, :]
            vw = vext_ref[gi, jb * tq:jb * tq + 2 * tq, :]
            variant = 0
            if jb == 0:
                variant = variant + (seg_in_sub == 0).astype(jnp.int32)
            if jb == nblk - 1:
                variant = variant + 2 * (seg_in_sub == segs_per_sub - 1).astype(jnp.int32)
            s = _dot_nt(q, kw) * scale + bias_ref[gi, variant]
            mb = jnp.max(s, axis=-1, keepdims=True)
            e = jnp.exp(s - mb)
            lb = jnp.sum(e, axis=-1, keepdims=True)
            acc = _dot(e.astype(BF16), vw)
            start = (j0 + jb * tq) * r + m_res
            rows = pl.ds(start, tq, stride=r) if r > 1 else pl.ds(start, tq)
            m_old = m_ref[rows, :]
            m_new = jnp.maximum(m_old, mb)
            alpha = jnp.exp(m_old - m_new)
            beta = jnp.exp(mb - m_new)
            l_new = alpha * l_ref[rows, :] + beta * lb
            a_new = alpha * a_ref[rows, :] + beta * acc
            m_ref[rows, :] = m_new
            l_ref[rows, :] = l_new
            a_ref[rows, :] = a_new

    @pl.when(t == pl.num_programs(2) - 1)
    def _():
        o_ref[0] = (a_ref[...] / l_ref[...]).astype(o_ref.dtype)


def dilated_attention(qkvs, rel_bias):
    nb, S, _ = qkvs[0].shape
    hd = DIL_HEAD_DIM
    tq = 2 * DIL_HALF
    seg = S // DIL_SEGS
    assert seg % tq == 0 and all(w == 2 * DIL_HALF * r and (S // r) % seg == 0 for w, r in DIL_CONFIGS)
    nhb = seg // DIL_HALF
    last_hb = S // DIL_HALF - 1
    in_specs = [pl.BlockSpec(memory_space=pltpu.SMEM),
                pl.BlockSpec((len(DIL_CONFIGS), 8, 2 * tq), lambda b, h, t: (0, 0, 0))]
    args = [rel_bias, jnp.asarray(_dil_bucket_rows(tq))]
    for g in range(len(DIL_CONFIGS)):
        in_specs.append(pl.BlockSpec((1, seg, hd), lambda b, h, t: (b, t, h)))
        for part in (1, 2):
            off = part * DIL_HEADS
            in_specs += [
                pl.BlockSpec((1, DIL_HALF, hd), lambda b, h, t, off=off: (b, jnp.maximum(t * nhb - 1, 0), off + h)),
                pl.BlockSpec((1, seg, hd), lambda b, h, t, off=off: (b, t, off + h)),
                pl.BlockSpec((1, DIL_HALF, hd), lambda b, h, t, off=off: (b, jnp.minimum((t + 1) * nhb, last_hb), off + h)),
            ]
        args += [qkvs[g]] * 7
    kern = functools.partial(_dil_kernel, seq=S, seg=seg, tq=tq)
    return pl.pallas_call(
        kern,
        grid=(nb, DIL_HEADS, DIL_SEGS),
        in_specs=in_specs,
        out_specs=pl.BlockSpec((1, S, hd), lambda b, h, t: (b, 0, h)),
        out_shape=jax.ShapeDtypeStruct((nb, S, DIL_HEADS * hd), BF16),
        scratch_shapes=[pltpu.VMEM((len(DIL_CONFIGS), 4, tq, 2 * tq), F32),
                        pltpu.VMEM((S, hd), F32), pltpu.VMEM((S, hd), F32), pltpu.VMEM((S, hd), F32),
                        pltpu.VMEM((len(DIL_CONFIGS), seg + 2 * DIL_HALF, hd), BF16),
                        pltpu.VMEM((len(DIL_CONFIGS), seg + 2 * DIL_HALF, hd), BF16)],
        compiler_params=_cparams("parallel", "parallel", "arbitrary"),
        name="dil_attn",
    )(*args)


def _tile(n, pref):
    return pref if n % pref == 0 else n


def _ssd_lane_layout(v):
    v = v.reshape(2, SSD_GROUPS, SSD_HEADS_PER_GROUP).transpose(1, 0, 2)
    v = jnp.pad(v, ((0, 0), (0, 0), (0, GROUP_LANES // 2 - SSD_HEADS_PER_GROUP)))
    return v.reshape(1, LANES)


def _ssd_mixer(x2, h3, kmem, vmem_, w_in, conv_w, conv_b, dt_bias, a_log, d_skip, norm_g, w_out, mem_q_gain):
    nb, S, D = h3.shape
    inner = SSD_GROUPS * SSD_HEADS_PER_GROUP * SSD_HEAD_DIM
    conv_ch = inner + 2 * SSD_GROUPS * SSD_STATE
    nheads = SSD_GROUPS * SSD_HEADS_PER_GROUP
    mw = kmem.shape[2]
    tm = _tile(S, 1024)
    w_bf = w_in.astype(BF16)

    def grouped(t, z=None):
        lead = t.shape[:-1]
        parts = [t[..., :inner].reshape(*lead, SSD_GROUPS, SSD_GROUP_WIDTH),
                 t[..., inner:inner + SSD_GROUPS * SSD_STATE].reshape(*lead, SSD_GROUPS, SSD_STATE),
                 t[..., inner + SSD_GROUPS * SSD_STATE:].reshape(*lead, SSD_GROUPS, SSD_STATE)]
        if z is not None:
            parts.append(z.reshape(*lead, SSD_GROUPS, SSD_GROUP_WIDTH))
        return jnp.concatenate(parts, axis=-1).reshape(*lead, -1)

    w_p1 = grouped(w_bf[:, inner:inner + conv_ch], z=w_bf[:, :inner])
    xc = ssd_inproj(h3, w_p1, grouped(conv_w), grouped(conv_b), tm)
    w_dt = w_in[:, inner + conv_ch:inner + conv_ch + 2 * nheads]
    w_dt = w_dt.reshape(D, 2, SSD_GROUPS, SSD_HEADS_PER_GROUP).transpose(0, 2, 1, 3)
    w_dt = jnp.pad(w_dt, ((0, 0), (0, 0), (0, 0), (0, GROUP_LANES // 2 - SSD_HEADS_PER_GROUP)))
    q_off = inner + conv_ch + 2 * nheads
    w_side = jnp.concatenate([w_bf[:, q_off:], w_dt.reshape(D, LANES).astype(BF16)], axis=1)
    q_mem, dtr = side_proj(h3, w_side, mem_q_gain, tm)

    tr = ssd_prep(dtr, _ssd_lane_layout(dt_bias), _ssd_lane_layout(a_log))
    dskip = jnp.repeat(d_skip, SSD_HEAD_DIM)
    nchunks = S // SSD_CHUNK
    kc = next(k for k in (4, 2, 1) if nchunks % (2 * k) == 0)
    y = ssd_scan(xc, tr, dskip, norm_g, kc)
    o_mem = mem_attention(q_mem, kmem, vmem_, tq=tm)
    w_out_bf = w_out.astype(BF16)
    return outproj(x2, y.reshape(nb * S, inner), o_mem.reshape(nb * S, mw),
                   w_out_bf[:inner], w_out_bf[inner:], tm=_tile(nb * S, 1024), tn=1024)


def _dil_mixer(x2, h3, kmem, vmem_, w_in, q_gain, k_gain, w_out, rel_bias, mem_q_gain):
    nb, S, D = h3.shape
    width = DIL_HEADS * DIL_HEAD_DIM
    mw = kmem.shape[2]
    tm = _tile(S, 1024)
    w_bf = w_in.astype(BF16)
    qkvs = []
    for g, (_, r) in enumerate(DIL_CONFIGS):
        gains = jnp.stack([jnp.tile(q_gain[g], DIL_HEADS), jnp.tile(k_gain[g], DIL_HEADS),
                           jnp.ones((width,), F32)]).reshape(3, 1, width)
        out = proj(h3, w_bf, g * 3 * width, 3 * width, gains, out_dtype=BF16, tm=tm, tn=width,
                   norm_width=DIL_HEAD_DIM, norm_tiles=2, r=r)
        qkvs.append(out.reshape(nb, S, 3 * width))
    q_off = len(DIL_CONFIGS) * 3 * width
    hd = mem_q_gain.shape[0]
    qg = jnp.tile(mem_q_gain, mw // hd).reshape(1, 1, mw)
    q_mem = proj(h3, w_bf, q_off, mw, qg, out_dtype=BF16, tm=tm, tn=mw, norm_width=hd)[:, 0]
    o = dilated_attention(qkvs, rel_bias)
    o_mem = mem_attention(q_mem, kmem, vmem_, tq=tm)
    w_out_bf = w_out.astype(BF16)
    return outproj(x2, o.reshape(nb * S, width), o_mem.reshape(nb * S, mw),
                   w_out_bf[:width], w_out_bf[width:], tm=_tile(nb * S, 1024), tn=1024)


def kernel(x, mem, rel_bias, ffn_norm, ffn_w_in, ffn_w_out, mix_norm, mem_norm, mem_w_kv, mem_q_gain, mem_k_gain, ssd_w_in, ssd_conv_w, ssd_conv_b, ssd_dt_bias, ssd_A_log, ssd_D, ssd_norm, ssd_w_out, dil_w_in, dil_q_gain, dil_k_gain, dil_w_out):
    nb, S, D = x.shape
    depth = ffn_norm.shape[0]
    T = nb * S
    x2 = x.reshape(T, D)
    d_ff = ffn_w_out.shape[2]
    tm_ffn, tf = _tile(T, 1024), _tile(d_ff, 512)
    tm_ffn_norm, tf_norm = tm_ffn, tf
    ffn_w_in = ffn_w_in.astype(BF16)
    ffn_w_out = ffn_w_out.astype(BF16)
    for i in range(depth):
        x2, h2 = ffn(x2, ffn_norm[i, 0], ffn_w_in, ffn_w_out, i, 0, tm_ffn_norm, tf_norm, next_gain=mix_norm[i])
        h3 = h2.reshape(nb, S, D)
        kmem, vmem_ = mem_kv(mem, mem_norm[i], mem_w_kv[i].astype(BF16), mem_k_gain[i])
        j = i // 2
        if i % 2 == 0:
            x2 = _ssd_mixer(x2, h3, kmem, vmem_, ssd_w_in[j], ssd_conv_w[j], ssd_conv_b[j], ssd_dt_bias[j],
                            ssd_A_log[j], ssd_D[j], ssd_norm[j], ssd_w_out[j], mem_q_gain[i])
        else:
            x2 = _dil_mixer(x2, h3, kmem, vmem_, dil_w_in[j], dil_q_gain[j], dil_k_gain[j], dil_w_out[j],
                            rel_bias, mem_q_gain[i])
        x2 = ffn(x2, ffn_norm[i, 1], ffn_w_in, ffn_w_out, i, 1, tm_ffn, tf)
    return x2.reshape(nb, S, D)
```

```python
import functools
import math

import jax
import jax.numpy as jnp
import numpy as np
from jax import lax
from jax.experimental import pallas as pl
from jax.experimental.pallas import tpu as pltpu

F32 = jnp.float32
BF16 = jnp.bfloat16
EPS = 1e-6

MEM_HEADS = 4
SSD_HEAD_DIM = 64
SSD_GROUPS = 8
SSD_HEADS_PER_GROUP = 6
SSD_STATE = 128
SSD_CONV = 5
SSD_CHUNK = 128
DIL_CONFIGS = ((128, 1), (512, 4), (2048, 16))
DIL_HEADS = 8
DIL_HEAD_DIM = 128
DIL_HALF = 64
DIL_SEGS = 16
DIL_QUERY_BLOCK = 128
REL_BUCKETS = 32
REL_MAX_DISTANCE = 1024

LANES = 128
VMEM_LIMIT_BYTES = 62 * 1024 * 1024
GROUP_LANES = 16


def _cparams(*sem, flags=None):
    return pltpu.CompilerParams(dimension_semantics=sem, vmem_limit_bytes=VMEM_LIMIT_BYTES, flags=flags)


def _rms(x, gain):
    ms = jnp.mean(x * x, axis=-1, keepdims=True)
    return x * lax.rsqrt(ms + EPS) * gain


def _dot(a, b):
    return jnp.dot(a, b, preferred_element_type=F32)


def _dot_nt(a, b):
    return lax.dot_general(a, b, (((1,), (1,)), ((), ())), preferred_element_type=F32)


FFN_SUB_ROWS = 512


def _ffn_kernel(x_ref, g_ref, wg_ref, wu_ref, wo_ref, *rest, emit_norm):
    if emit_norm:
        g2_ref, o_ref, hn_ref = rest
        h_ref = hn_ref
    else:
        o_ref, h_ref = rest
    j = pl.program_id(1)

    tm = x_ref.shape[0]
    sub = min(tm, FFN_SUB_ROWS)

    @pl.when(j == 0)
    def _():
        for r0 in range(0, tm, sub // 2):
            rows = slice(r0, r0 + sub // 2)
            h_ref[rows, :] = _rms(x_ref[rows, :], g_ref[...]).astype(BF16)
            o_ref[rows, :] = x_ref[rows, :]

    for r0 in range(0, tm, sub):
        h = h_ref[r0:r0 + sub, :]
        gate = _dot(h, wg_ref[...])
        up = _dot(h, wu_ref[...])
        a = (jax.nn.silu(gate) * up * 0.5).astype(BF16)
        o_ref[r0:r0 + sub, :] += _dot(a, wo_ref[...])

    if emit_norm:
        @pl.when(j == pl.num_programs(1) - 1)
        def _():
            for r0 in range(0, tm, sub // 2):
                rows = slice(r0, r0 + sub // 2)
                hn_ref[rows, :] = _rms(o_ref[rows, :], g2_ref[...]).astype(hn_ref.dtype)


def ffn(x2, gain, w_in, w_out, layer, which, tm, tf, next_gain=None):
    T, D = x2.shape
    F = w_out.shape[2]
    nf = F // tf
    emit_norm = next_gain is not None
    row_spec = pl.BlockSpec((tm, D), lambda i, j: (i, 0))
    vec_spec = pl.BlockSpec((1, D), lambda i, j: (0, 0))
    in_specs = [row_spec, vec_spec,
                pl.BlockSpec((None, None, D, tf), lambda i, j: (layer, which, 0, j)),
                pl.BlockSpec((None, None, D, tf), lambda i, j: (layer, which, 0, j + nf)),
                pl.BlockSpec((None, None, tf, D), lambda i, j: (layer, which, j, 0))]
    args = [x2, gain.reshape(1, D), w_in, w_in, w_out]
    out_specs, out_shape = row_spec, jax.ShapeDtypeStruct((T, D), F32)
    if emit_norm:
        in_specs.append(vec_spec)
        args.append(next_gain.reshape(1, D))
        out_specs, out_shape = [row_spec, row_spec], [out_shape, jax.ShapeDtypeStruct((T, D), BF16)]
    return pl.pallas_call(
        functools.partial(_ffn_kernel, emit_norm=emit_norm),
        grid=(T // tm, nf),
        in_specs=in_specs,
        out_specs=out_specs,
        out_shape=out_shape,
        scratch_shapes=[] if emit_norm else [pltpu.VMEM((tm, D), BF16)],
        compiler_params=_cparams("parallel", "arbitrary"),
        name="ffn",
    )(*args)


PROJ_CHUNK = 256


def _proj_kernel(h_ref, w_ref, g_ref, o_ref, *scratch, norm_width, norm_tiles, r):
    tm, tn = h_ref.shape[1], w_ref.shape[1]
    rows_per = tm // r
    nw = norm_width if norm_width else PROJ_CHUNK
    assert PROJ_CHUNK % nw == 0 or nw % PROJ_CHUNK == 0
    chunk = max(PROJ_CHUNK, nw)

    def emit(normed):
        for c0 in range(0, tn, chunk):
            res = _dot(h_ref[0], w_ref[:, c0:c0 + chunk])
            if r == 1:
                if not normed:
                    o_ref[0, 0, :, c0:c0 + chunk] = res.astype(o_ref.dtype)
                    continue
                for c in range(c0, c0 + chunk, nw):
                    blk = res[:, c - c0:c - c0 + nw]
                    o_ref[0, 0, :, c:c + nw] = _rms(blk, g_ref[0, :, c:c + nw]).astype(o_ref.dtype)
                continue
            acc_ref, = scratch
            for c in range(c0, c0 + chunk, LANES):
                acc_ref[c // LANES] = res[:, c - c0:c - c0 + LANES]
            for m in range(r):
                rows = pl.ds(m, rows_per, stride=r)
                for c in range(c0, c0 + chunk, LANES):
                    blk = acc_ref[c // LANES, rows, :]
                    if normed:
                        blk = _rms(blk, g_ref[0, :, c:c + LANES])
                    o_ref[0, m, :, c:c + LANES] = blk.astype(o_ref.dtype)

    if r > 1:
        assert norm_width in (0, LANES)
    if norm_width == 0:
        emit(False)
    elif norm_tiles is None:
        emit(True)
    else:
        j = pl.program_id(2)
        pl.when(j < norm_tiles)(lambda: emit(True))
        pl.when(j >= norm_tiles)(lambda: emit(False))


def proj(h3, w, col0, ncols, gains, *, out_dtype, tm, tn, norm_width=0, norm_tiles=None, r=1):
    nb, S, K = h3.shape
    assert col0 % tn == 0 and ncols % tn == 0 and S % tm == 0 and tm % (8 * r) == 0
    j0 = col0 // tn
    nj = ncols // tn
    if gains is None:
        gains = jnp.ones((nj, 1, tn), F32)
    scratch = [] if r == 1 else [pltpu.VMEM((tn // LANES, tm, LANES), F32)]
    kern = functools.partial(_proj_kernel, norm_width=norm_width, norm_tiles=norm_tiles, r=r)
    return pl.pallas_call(
        kern,
        grid=(nb, S // tm, nj),
        in_specs=[pl.BlockSpec((1, tm, K), lambda b, i, j: (b, i, 0)),
                  pl.BlockSpec((K, tn), lambda b, i, j: (0, j0 + j)),
                  pl.BlockSpec((1, 1, tn), lambda b, i, j: (j, 0, 0))],
        out_specs=pl.BlockSpec((1, r, tm // r, tn), lambda b, i, j: (b, 0, i, j)),
        out_shape=jax.ShapeDtypeStruct((nb, r, S // r, ncols), out_dtype),
        scratch_shapes=scratch,
        compiler_params=_cparams("parallel", "parallel", "arbitrary"),
        name="proj",
    )(h3, w, gains)


def _side_proj_kernel(h_ref, w_ref, g_ref, q_ref, dt_ref, *, head_dim):
    mw = q_ref.shape[2]
    for c in range(mw // head_dim):
        cols = slice(c * head_dim, (c + 1) * head_dim)
        q_ref[0, :, cols] = _rms(_dot(h_ref[0], w_ref[:, cols]), g_ref[...]).astype(q_ref.dtype)
    dt_ref[0] = _dot(h_ref[0], w_ref[:, mw:])


def side_proj(h3, w, q_gain, tm):
    nb, S, K = h3.shape
    head_dim = q_gain.shape[0]
    n = w.shape[1]
    mw = n - LANES
    return pl.pallas_call(
        functools.partial(_side_proj_kernel, head_dim=head_dim),
        grid=(nb, S // tm),
        in_specs=[pl.BlockSpec((1, tm, K), lambda b, i: (b, i, 0)),
                  pl.BlockSpec((K, n), lambda b, i: (0, 0)),
                  pl.BlockSpec((1, head_dim), lambda b, i: (0, 0))],
        out_specs=[pl.BlockSpec((1, tm, mw), lambda b, i: (b, i, 0)),
                   pl.BlockSpec((1, tm, LANES), lambda b, i: (b, i, 0))],
        out_shape=[jax.ShapeDtypeStruct((nb, S, mw), BF16), jax.ShapeDtypeStruct((nb, S, LANES), F32)],
        compiler_params=_cparams("parallel", "parallel"),
        name="side_proj",
    )(h3, w, q_gain.reshape(1, head_dim))


def _outproj_kernel(x_ref, a1_ref, a2_ref, w1_ref, w2_ref, o_ref):
    for c0 in range(0, o_ref.shape[1], PROJ_CHUNK):
        cols = slice(c0, c0 + PROJ_CHUNK)
        o_ref[:, cols] = x_ref[:, cols] + _dot(a1_ref[...], w1_ref[:, cols]) + _dot(a2_ref[...], w2_ref[:, cols])


def outproj(x2, a1, a2, w1, w2, tm, tn):
    T, D = x2.shape
    k1, k2 = a1.shape[1], a2.shape[1]
    return pl.pallas_call(
        _outproj_kernel,
        grid=(T // tm, D // tn),
        in_specs=[pl.BlockSpec((tm, tn), lambda i, j: (i, j)),
                  pl.BlockSpec((tm, k1), lambda i, j: (i, 0)),
                  pl.BlockSpec((tm, k2), lambda i, j: (i, 0)),
                  pl.BlockSpec((k1, tn), lambda i, j: (0, j)),
                  pl.BlockSpec((k2, tn), lambda i, j: (0, j))],
        out_specs=pl.BlockSpec((tm, tn), lambda i, j: (i, j)),
        out_shape=jax.ShapeDtypeStruct((T, D), F32),
        compiler_params=_cparams("parallel", "arbitrary"),
        name="outproj",
    )(x2, a1, a2, w1, w2)


def _memkv_kernel(mem_ref, g_ref, w_ref, kg_ref, k_ref, v_ref):
    mw = k_ref.shape[2]
    hd = kg_ref.shape[1]
    memn = _rms(mem_ref[0], g_ref[...]).astype(BF16)
    kv = _dot(memn, w_ref[...])
    for hh in range(mw // hd):
        cols = slice(hh * hd, (hh + 1) * hd)
        k_ref[0, :, cols] = _rms(kv[:, cols], kg_ref[...]).astype(BF16)
    v_ref[0] = kv[:, mw:].astype(BF16)


def mem_kv(mem, mem_gain, w_kv, k_gain):
    nb, M, D = mem.shape
    mw = w_kv.shape[1] // 2
    hd = k_gain.shape[0]
    out = jax.ShapeDtypeStruct((nb, M, mw), BF16)
    return pl.pallas_call(
        _memkv_kernel,
        grid=(nb,),
        in_specs=[pl.BlockSpec((1, M, D), lambda b: (b, 0, 0)),
                  pl.BlockSpec((1, D), lambda b: (0, 0)),
                  pl.BlockSpec((D, 2 * mw), lambda b: (0, 0)),
                  pl.BlockSpec((1, hd), lambda b: (0, 0))],
        out_specs=[pl.BlockSpec((1, M, mw), lambda b: (b, 0, 0))] * 2,
        out_shape=[out, out],
        compiler_params=_cparams("parallel"),
        name="mem_kv",
    )(mem, mem_gain.reshape(1, D), w_kv, k_gain.reshape(1, hd))


def _memattn_kernel(q_ref, k_ref, v_ref, o_ref, *, heads):
    hd = q_ref.shape[2] // heads
    scale = hd ** -0.5
    for hh in range(heads):
        cols = slice(hh * hd, (hh + 1) * hd)
        s = _dot_nt(q_ref[0, :, cols], k_ref[0, :, cols]) * scale
        e = jnp.exp(s - jnp.max(s, axis=-1, keepdims=True))
        p = e / jnp.sum(e, axis=-1, keepdims=True)
        o_ref[0, :, cols] = _dot(p.astype(BF16), v_ref[0, :, cols]).astype(o_ref.dtype)


def mem_attention(q, k, v, tq):
    nb, S, mw = q.shape
    M = k.shape[1]
    return pl.pallas_call(
        functools.partial(_memattn_kernel, heads=MEM_HEADS),
        grid=(nb, S // tq),
        in_specs=[pl.BlockSpec((1, tq, mw), lambda b, i: (b, i, 0)),
                  pl.BlockSpec((1, M, mw), lambda b, i: (b, 0, 0)),
                  pl.BlockSpec((1, M, mw), lambda b, i: (b, 0, 0))],
        out_specs=pl.BlockSpec((1, tq, mw), lambda b, i: (b, i, 0)),
        out_shape=jax.ShapeDtypeStruct((nb, S, mw), BF16),
        compiler_params=_cparams("parallel", "parallel"),
        name="mem_attn",
    )(q, k, v)


SSD_GROUP_WIDTH = SSD_HEADS_PER_GROUP * SSD_HEAD_DIM
SSD_GROUP_CONV = SSD_GROUP_WIDTH + 2 * SSD_STATE
SSD_GROUP_COLS = SSD_GROUP_CONV + SSD_GROUP_WIDTH
HALO_ROWS = 16


def _ssd_inproj_kernel(hp_ref, h_ref, hn_ref, w_ref, cw_ref, cb_ref, o_ref, lhs_ref, ext_ref):
    i = pl.program_id(1)
    tm = h_ref.shape[1]
    lo = HALO_ROWS - SSD_CONV // 2

    @pl.when(pl.program_id(2) == 0)
    def _():
        lhs_ref[0:HALO_ROWS] = jnp.where(i > 0, hp_ref[0], jnp.zeros_like(hp_ref[0]))
        lhs_ref[HALO_ROWS:HALO_ROWS + tm] = h_ref[0]
        lhs_ref[HALO_ROWS + tm:] = jnp.where(i < pl.num_programs(1) - 1, hn_ref[0], jnp.zeros_like(hn_ref[0]))

    for c0 in range(0, SSD_GROUP_COLS, PROJ_CHUNK):
        if c0 >= SSD_GROUP_CONV:
            o_ref[0, :, c0:c0 + PROJ_CHUNK] = jax.nn.silu(_dot(h_ref[0], w_ref[:, c0:c0 + PROJ_CHUNK]))
            continue
        res = _dot(lhs_ref[...], w_ref[:, c0:c0 + PROJ_CHUNK])
        for c in range(c0, c0 + PROJ_CHUNK, LANES):
            part = slice(c - c0, c - c0 + LANES)
            if c >= SSD_GROUP_CONV:
                o_ref[0, :, c:c + LANES] = jax.nn.silu(res[HALO_ROWS:HALO_ROWS + tm, part])
                continue
            s = c // LANES
            ext_ref[s] = res[:, part]
            acc = cb_ref[:, c:c + LANES] + cw_ref[0:1, c:c + LANES] * ext_ref[s, lo:lo + tm, :]
            for k in range(1, SSD_CONV):
                acc = acc + cw_ref[k:k + 1, c:c + LANES] * ext_ref[s, lo + k:lo + k + tm, :]
            o_ref[0, :, c:c + LANES] = jax.nn.silu(acc)


def ssd_inproj(h3, w, conv_w, conv_b, tm):
    nb, S, K = h3.shape
    G, gc, cols = SSD_GROUPS, SSD_GROUP_CONV, SSD_GROUP_COLS
    assert PROJ_CHUNK % LANES == 0 and gc % LANES == 0 and cols % PROJ_CHUNK == 0 and tm % HALO_ROWS == 0
    nh = tm // HALO_ROWS
    last = S // HALO_ROWS - 1
    return pl.pallas_call(
        _ssd_inproj_kernel,
        grid=(nb, S // tm, G),
        in_specs=[pl.BlockSpec((1, HALO_ROWS, K), lambda b, i, g: (b, jnp.maximum(i * nh - 1, 0), 0)),
                  pl.BlockSpec((1, tm, K), lambda b, i, g: (b, i, 0)),
                  pl.BlockSpec((1, HALO_ROWS, K), lambda b, i, g: (b, jnp.minimum((i + 1) * nh, last), 0)),
                  pl.BlockSpec((K, cols), lambda b, i, g: (0, g)),
                  pl.BlockSpec((SSD_CONV, gc), lambda b, i, g: (0, g)),
                  pl.BlockSpec((1, gc), lambda b, i, g: (0, g))],
        out_specs=pl.BlockSpec((1, tm, cols), lambda b, i, g: (b, i, g)),
        out_shape=jax.ShapeDtypeStruct((nb, S, G * cols), F32),
        scratch_shapes=[pltpu.VMEM((tm + 2 * HALO_ROWS, K), BF16),
                        pltpu.VMEM((gc // LANES, tm + 2 * HALO_ROWS, LANES), F32)],
        compiler_params=_cparams("parallel", "parallel", "arbitrary"),
        name="ssd_inproj",
    )(h3, h3, h3, w, conv_w, conv_b.reshape(1, G * gc))


LOG2E = math.log2(math.e)


def _ssd_prep_kernel(dtr_ref, bias_ref, alog_ref, tr_ref):
    Q = SSD_CHUNK
    ii = lax.broadcasted_iota(jnp.int32, (Q, Q), 0)
    jj = lax.broadcasted_iota(jnp.int32, (Q, Q), 1)
    tril = (jj <= ii).astype(F32)
    triu = (jj >= ii).astype(F32)
    lane = lax.broadcasted_iota(jnp.int32, (Q, LANES), 1)
    is_bwd = (lane % GROUP_LANES) >= GROUP_LANES // 2
    hi = lax.Precision.HIGHEST
    for i in range(tr_ref.shape[1]):
        x = dtr_ref[0, i * Q:(i + 1) * Q, :] + bias_ref[...]
        dt = jnp.maximum(x, 0.0) + jnp.log1p(jnp.exp(-jnp.abs(x)))
        a = dt * (-jnp.exp(alog_ref[...]))
        cs_fwd = jnp.dot(tril, a, precision=hi, preferred_element_type=F32)
        cs_bwd = jnp.dot(triu, a, precision=hi, preferred_element_type=F32)
        cs = jnp.where(is_bwd, cs_bwd, cs_fwd)
        last = jnp.where(is_bwd[0:1], cs[0:1], cs[Q - 1:Q])
        cs2 = cs * LOG2E
        parts = (cs2, cs2 - jnp.log2(dt), jnp.exp(last - cs) * dt, jnp.broadcast_to(jnp.exp(last), (Q, LANES)))
        for p, part in enumerate(parts):
            rows = part.T
            for g in range(SSD_GROUPS):
                tr_ref[0, i, g, p * GROUP_LANES:(p + 1) * GROUP_LANES] = rows[g * GROUP_LANES:(g + 1) * GROUP_LANES]


def ssd_prep(dtr, bias_l, alog_l):
    nb, S, _ = dtr.shape
    Q = SSD_CHUNK
    nc = S // Q
    kp = next(k for k in (8, 4, 2, 1) if nc % k == 0)
    vec_spec = pl.BlockSpec((1, LANES), lambda b, c: (0, 0))
    return pl.pallas_call(
        _ssd_prep_kernel,
        grid=(nb, nc // kp),
        in_specs=[pl.BlockSpec((1, kp * Q, LANES), lambda b, c: (b, c, 0)), vec_spec, vec_spec],
        out_specs=pl.BlockSpec((1, kp, SSD_GROUPS, 4 * GROUP_LANES, Q), lambda b, c: (b, c, 0, 0, 0)),
        out_shape=jax.ShapeDtypeStruct((nb, nc, SSD_GROUPS, 4 * GROUP_LANES, Q), F32),
        compiler_params=_cparams("parallel", "parallel"),
        name="ssd_prep",
    )(dtr, bias_l, alog_l)


def _ssd_direction(d, g, sub, xc_ref, tr_ref, st_ref):
    Q = SSD_CHUNK
    P, N, gw = SSD_HEAD_DIM, SSD_STATE, SSD_GROUP_WIDTH
    rows = slice(sub * Q, (sub + 1) * Q)
    xs = xc_ref[0, rows, 0:gw]
    bm = xc_ref[0, rows, gw:gw + N]
    cm = xc_ref[0, rows, gw + N:gw + 2 * N]
    cb = _dot_nt(cm.astype(BF16), bm.astype(BF16))
    bt = bm.T

    cs2T, rowT, wT, cdT = (tr_ref[0, sub, 0, part * GROUP_LANES:(part + 1) * GROUP_LANES, :] for part in range(4))
    ii = lax.broadcasted_iota(jnp.int32, (Q, Q), 0)
    jj = lax.broadcasted_iota(jnp.int32, (Q, Q), 1)
    causal = (jj <= ii) if d == 0 else (jj >= ii)
    lo = lax.broadcasted_iota(jnp.int32, (Q, LANES), 1) < P
    st = st_ref[d]
    ys, sts = [], []
    for p in range(SSD_HEADS_PER_GROUP // 2):
        cols = slice(p * LANES, (p + 1) * LANES)
        ms, ss = [], []
        k0 = d * (GROUP_LANES // 2) + 2 * p
        for k in (k0, k0 + 1):
            col2 = jnp.broadcast_to(cs2T[k:k + 1, :], (Q, Q)).T
            decay_dt = jnp.exp2(jnp.where(causal, col2 - rowT[k:k + 1, :], -jnp.inf))
            ms.append((cb * decay_dt).astype(BF16))
            ms.append((cm * jnp.exp2(col2)).astype(BF16))
            ss.append((bt * wT[k:k + 1, :]).astype(BF16))
        cd = jnp.where(lo[0:1], cdT[k0:k0 + 1, :], cdT[k0 + 1:k0 + 2, :])
        xp, sp = xs[:, cols], st[:, cols]
        x_lo, x_hi = jnp.where(lo, xp, 0.0).astype(BF16), jnp.where(lo, 0.0, xp).astype(BF16)
        s_lo, s_hi = jnp.where(lo, sp, 0.0).astype(BF16), jnp.where(lo, 0.0, sp).astype(BF16)
        ys.append(_dot(jnp.concatenate(ms, axis=1), jnp.concatenate([x_lo, s_lo, x_hi, s_hi], axis=0)))
        sts.append(cd * sp + _dot(jnp.concatenate(ss, axis=1), jnp.concatenate([x_lo, x_hi], axis=0)))
    st_ref[d] = jnp.concatenate(sts, axis=1)
    return jnp.concatenate(ys, axis=1)


def _ssd_scan_kernel(xcf, xcb, trf, trb, dskip_ref, ng_ref, y_ref, st_ref, ysum_ref, *, kc):
    g = pl.program_id(1)
    c = pl.program_id(2)
    nsteps = pl.num_programs(2)
    Q = SSD_CHUNK

    @pl.when(c == 0)
    def _():
        st_ref[...] = jnp.zeros_like(st_ref)
        ysum_ref[...] = jnp.zeros_like(ysum_ref)

    def finish(y_dir, row0, xc_ref, sub):
        rows = pl.ds(pl.multiple_of(row0, Q), Q)
        blk = slice(sub * Q, (sub + 1) * Q)
        tot = ysum_ref[rows, :] + y_dir + dskip_ref[...] * xc_ref[0, blk, 0:SSD_GROUP_WIDTH]
        ysum_ref[rows, :] = y_dir
        gated = tot * xc_ref[0, blk, SSD_GROUP_CONV:SSD_GROUP_COLS]
        y_ref[0, rows, :] = _rms(gated, ng_ref[...]).astype(y_ref.dtype)

    for i in range(kc):
        y_f = _ssd_direction(0, g, i, xcf, trf, st_ref)
        finish(y_f, (c * kc + i) * Q, xcf, i)
        y_b = _ssd_direction(1, g, kc - 1 - i, xcb, trb, st_ref)
        finish(y_b, ((nsteps - 1 - c) * kc + kc - 1 - i) * Q, xcb, kc - 1 - i)


def ssd_scan(xc, tr, dskip, norm_g, kc):
    nb, S, _ = xc.shape
    Q = SSD_CHUNK
    nsteps = S // (Q * kc)
    assert nsteps % 2 == 0 and Q == LANES
    G, N, gw = SSD_GROUPS, SSD_STATE, SSD_GROUP_WIDTH
    inner = G * gw

    def both(shape, f):
        return [pl.BlockSpec(shape, lambda b, g, c: f(b, g, c)),
                pl.BlockSpec(shape, lambda b, g, c: f(b, g, nsteps - 1 - c))]

    in_specs = (both((1, kc * Q, SSD_GROUP_COLS), lambda b, g, c: (b, c, g))
                + both((1, kc, 1, 4 * GROUP_LANES, Q), lambda b, g, c: (b, c, g, 0, 0))
                + [pl.BlockSpec((1, gw), lambda b, g, c: (0, g)),
                   pl.BlockSpec((1, gw), lambda b, g, c: (0, g))])
    return pl.pallas_call(
        functools.partial(_ssd_scan_kernel, kc=kc),
        grid=(nb, G, nsteps),
        in_specs=in_specs,
        out_specs=pl.BlockSpec((1, S, gw), lambda b, g, c: (b, 0, g)),
        out_shape=jax.ShapeDtypeStruct((nb, S, inner), BF16),
        scratch_shapes=[pltpu.VMEM((2, N, gw), F32), pltpu.VMEM((S, gw), F32)],
        compiler_params=_cparams("parallel", "parallel", "arbitrary"),
        name="ssd_scan",
    )(xc, xc, tr, tr, dskip.reshape(1, inner), norm_g.reshape(1, inner))


def _t5_bucket_np(rel):
    half = REL_BUCKETS // 2
    exact = half // 2
    n = np.abs(rel)
    far = exact + (np.log(np.maximum(n, 1).astype(np.float32) / np.float32(exact))
                   / np.float32(math.log(REL_MAX_DISTANCE / exact)) * np.float32(half - exact)).astype(np.int32)
    far = np.minimum(far, half - 1)
    return np.where(rel > 0, half, 0) + np.where(n < exact, n, far)


def _dil_bucket_rows(win):
    d = np.arange(win)
    rel = d - DIL_HALF
    rows = []
    for _, dilation in DIL_CONFIGS:
        rows.append(np.where(d <= 2 * DIL_HALF, _t5_bucket_np(rel * dilation), -1))
    return np.broadcast_to(np.stack(rows)[:, None, :], (len(DIL_CONFIGS), 8, win)).astype(np.int32)


def _dil_kernel(tbl_ref, bm_ref, *refs, seq, seg, tq):
    ngroups = len(DIL_CONFIGS)
    win = tq + 2 * DIL_HALF
    in_refs = refs[:7 * ngroups]
    o_ref = refs[7 * ngroups]
    bias_ref, m_ref, l_ref, a_ref, kext_ref, vext_ref = refs[7 * ngroups + 1:]
    h = pl.program_id(1)
    t = pl.program_id(2)
    scale = DIL_HEAD_DIM ** -0.5

    @pl.when(t == 0)
    def _():
        m_ref[...] = jnp.full_like(m_ref, -jnp.inf)
        l_ref[...] = jnp.zeros_like(l_ref)
        a_ref[...] = jnp.zeros_like(a_ref)
        kcol = lax.broadcasted_iota(jnp.int32, (tq, win), 1)
        for gi in range(ngroups):
            ids = bm_ref[gi]
            row = jnp.zeros(ids.shape, F32)
            for u in range(REL_BUCKETS):
                row = jnp.where(ids == u, tbl_ref[u, gi * DIL_HEADS + h], row)
            row = jnp.where(ids < 0, -jnp.inf, row)
            band = pltpu.roll(jnp.broadcast_to(row[0:1], (tq, win)), 0, 1, stride=1, stride_axis=0)
            no_left = jnp.where(kcol >= DIL_HALF, band, -jnp.inf)
            bias_ref[gi, 0] = band
            bias_ref[gi, 1] = no_left
            bias_ref[gi, 2] = jnp.where(kcol < win - DIL_HALF, band, -jnp.inf)
            bias_ref[gi, 3] = jnp.where(kcol < win - DIL_HALF, no_left, -jnp.inf)

    nblk = seg // tq
    for gi, (_, r) in enumerate(DIL_CONFIGS):
        q_ref, kp, kc, kn, vp, vc, vn = in_refs[7 * gi:7 * gi + 7]
        sub_len = seq // r
        segs_per_sub = sub_len // seg
        m_res = t // segs_per_sub
        j0 = (t % segs_per_sub) * seg
        for ext, prev, cur, nxt in ((kext_ref, kp, kc, kn), (vext_ref, vp, vc, vn)):
            ext[gi, 0:DIL_HALF] = prev[0]
            ext[gi, DIL_HALF:DIL_HALF + seg] = cur[0]
            ext[gi, DIL_HALF + seg:] = nxt[0]
        seg_in_sub = t % segs_per_sub
        for jb in range(nblk):
            q = q_ref[0, jb * tq:(jb + 1) * tq, :]
            kw = kext_ref[gi, jb * tq:jb * tq + win, :]
            vw = vext_ref[gi, jb * tq:jb * tq + win, :]
            variant = 0
            if jb == 0:
                variant = variant + (seg_in_sub == 0).astype(jnp.int32)
            if jb == nblk - 1:
                variant = variant + 2 * (seg_in_sub == segs_per_sub - 1).astype(jnp.int32)
            s = _dot_nt(q, kw) * scale + bias_ref[gi, variant]
            mb = jnp.max(s, axis=-1, keepdims=True)
            e = jnp.exp(s - mb)
            lb = jnp.sum(e, axis=-1, keepdims=True)
            acc = _dot(e.astype(BF16), vw)
            start = (j0 + jb * tq) * r + m_res
            rows = pl.ds(start, tq, stride=r) if r > 1 else pl.ds(start, tq)
            m_old = m_ref[rows, :]
            m_new = jnp.maximum(m_old, mb)
            alpha = jnp.exp(m_old - m_new)
            beta = jnp.exp(mb - m_new)
            l_new = alpha * l_ref[rows, :] + beta * lb
            a_new = alpha * a_ref[rows, :] + beta * acc
            m_ref[rows, :] = m_new
            l_ref[rows, :] = l_new
            a_ref[rows, :] = a_new

    @pl.when(t == pl.num_programs(2) - 1)
    def _():
        o_ref[0] = (a_ref[...] / l_ref[...]).astype(o_ref.dtype)


def dilated_attention(qkvs, rel_bias):
    nb, S, _ = qkvs[0].shape
    hd = DIL_HEAD_DIM
    seg = S // DIL_SEGS
    tq = _tile(seg, DIL_QUERY_BLOCK)
    win = tq + 2 * DIL_HALF
    assert seg % tq == 0 and tq % LANES == 0
    assert all(w == 2 * DIL_HALF * r and (S // r) % seg == 0 for w, r in DIL_CONFIGS)
    nhb = seg // DIL_HALF
    last_hb = S // DIL_HALF - 1
    in_specs = [pl.BlockSpec(memory_space=pltpu.SMEM),
                pl.BlockSpec((len(DIL_CONFIGS), 8, win), lambda b, h, t: (0, 0, 0))]
    args = [rel_bias, jnp.asarray(_dil_bucket_rows(win))]
    for g in range(len(DIL_CONFIGS)):
        in_specs.append(pl.BlockSpec((1, seg, hd), lambda b, h, t: (b, t, h)))
        for part in (1, 2):
            off = part * DIL_HEADS
            in_specs += [
                pl.BlockSpec((1, DIL_HALF, hd), lambda b, h, t, off=off: (b, jnp.maximum(t * nhb - 1, 0), off + h)),
                pl.BlockSpec((1, seg, hd), lambda b, h, t, off=off: (b, t, off + h)),
                pl.BlockSpec((1, DIL_HALF, hd), lambda b, h, t, off=off: (b, jnp.minimum((t + 1) * nhb, last_hb), off + h)),
            ]
        args += [qkvs[g]] * 7
    kern = functools.partial(_dil_kernel, seq=S, seg=seg, tq=tq)
    return pl.pallas_call(
        kern,
        grid=(nb, DIL_HEADS, DIL_SEGS),
        in_specs=in_specs,
        out_specs=pl.BlockSpec((1, S, hd), lambda b, h, t: (b, 0, h)),
        out_shape=jax.ShapeDtypeStruct((nb, S, DIL_HEADS * hd), BF16),
        scratch_shapes=[pltpu.VMEM((len(DIL_CONFIGS), 4, tq, win), F32),
                        pltpu.VMEM((S, hd), F32), pltpu.VMEM((S, hd), F32), pltpu.VMEM((S, hd), F32),
                        pltpu.VMEM((len(DIL_CONFIGS), seg + 2 * DIL_HALF, hd), BF16),
                        pltpu.VMEM((len(DIL_CONFIGS), seg + 2 * DIL_HALF, hd), BF16)],
        compiler_params=_cparams("parallel", "parallel", "arbitrary"),
        name="dil_attn",
    )(*args)


def _tile(n, pref):
    return pref if n % pref == 0 else n


def _ssd_lane_layout(v):
    v = v.reshape(2, SSD_GROUPS, SSD_HEADS_PER_GROUP).transpose(1, 0, 2)
    v = jnp.pad(v, ((0, 0), (0, 0), (0, GROUP_LANES // 2 - SSD_HEADS_PER_GROUP)))
    return v.reshape(1, LANES)


def _ssd_mixer(x2, h3, kmem, vmem_, w_in, conv_w, conv_b, dt_bias, a_log, d_skip, norm_g, w_out, mem_q_gain):
    nb, S, D = h3.shape
    inner = SSD_GROUPS * SSD_HEADS_PER_GROUP * SSD_HEAD_DIM
    conv_ch = inner + 2 * SSD_GROUPS * SSD_STATE
    nheads = SSD_GROUPS * SSD_HEADS_PER_GROUP
    mw = kmem.shape[2]
    tm = _tile(S, 1024)
    w_bf = w_in.astype(BF16)

    def grouped(t, z=None):
        lead = t.shape[:-1]
        parts = [t[..., :inner].reshape(*lead, SSD_GROUPS, SSD_GROUP_WIDTH),
                 t[..., inner:inner + SSD_GROUPS * SSD_STATE].reshape(*lead, SSD_GROUPS, SSD_STATE),
                 t[..., inner + SSD_GROUPS * SSD_STATE:].reshape(*lead, SSD_GROUPS, SSD_STATE)]
        if z is not None:
            parts.append(z.reshape(*lead, SSD_GROUPS, SSD_GROUP_WIDTH))
        return jnp.concatenate(parts, axis=-1).reshape(*lead, -1)

    w_p1 = grouped(w_bf[:, inner:inner + conv_ch], z=w_bf[:, :inner])
    xc = ssd_inproj(h3, w_p1, grouped(conv_w), grouped(conv_b), tm)
    w_dt = w_in[:, inner + conv_ch:inner + conv_ch + 2 * nheads]
    w_dt = w_dt.reshape(D, 2, SSD_GROUPS, SSD_HEADS_PER_GROUP).transpose(0, 2, 1, 3)
    w_dt = jnp.pad(w_dt, ((0, 0), (0, 0), (0, 0), (0, GROUP_LANES // 2 - SSD_HEADS_PER_GROUP)))
    q_off = inner + conv_ch + 2 * nheads
    w_side = jnp.concatenate([w_bf[:, q_off:], w_dt.reshape(D, LANES).astype(BF16)], axis=1)
    q_mem, dtr = side_proj(h3, w_side, mem_q_gain, tm)

    tr = ssd_prep(dtr, _ssd_lane_layout(dt_bias), _ssd_lane_layout(a_log))
    dskip = jnp.repeat(d_skip, SSD_HEAD_DIM)
    nchunks = S // SSD_CHUNK
    kc = next(k for k in (4, 2, 1) if nchunks % (2 * k) == 0)
    y = ssd_scan(xc, tr, dskip, norm_g, kc)
    o_mem = mem_attention(q_mem, kmem, vmem_, tq=tm)
    w_out_bf = w_out.astype(BF16)
    return outproj(x2, y.reshape(nb * S, inner), o_mem.reshape(nb * S, mw),
                   w_out_bf[:inner], w_out_bf[inner:], tm=_tile(nb * S, 1024), tn=1024)


def _dil_mixer(x2, h3, kmem, vmem_, w_in, q_gain, k_gain, w_out, rel_bias, mem_q_gain):
    nb, S, D = h3.shape
    width = DIL_HEADS * DIL_HEAD_DIM
    mw = kmem.shape[2]
    tm = _tile(S, 1024)
    w_bf = w_in.astype(BF16)
    qkvs = []
    for g, (_, r) in enumerate(DIL_CONFIGS):
        gains = jnp.stack([jnp.tile(q_gain[g], DIL_HEADS), jnp.tile(k_gain[g], DIL_HEADS),
                           jnp.ones((width,), F32)]).reshape(3, 1, width)
        out = proj(h3, w_bf, g * 3 * width, 3 * width, gains, out_dtype=BF16, tm=tm, tn=width,
                   norm_width=DIL_HEAD_DIM, norm_tiles=2, r=r)
        qkvs.append(out.reshape(nb, S, 3 * width))
    q_off = len(DIL_CONFIGS) * 3 * width
    hd = mem_q_gain.shape[0]
    qg = jnp.tile(mem_q_gain, mw // hd).reshape(1, 1, mw)
    q_mem = proj(h3, w_bf, q_off, mw, qg, out_dtype=BF16, tm=tm, tn=mw, norm_width=hd)[:, 0]
    o = dilated_attention(qkvs, rel_bias)
    o_mem = mem_attention(q_mem, kmem, vmem_, tq=tm)
    w_out_bf = w_out.astype(BF16)
    return outproj(x2, o.reshape(nb * S, width), o_mem.reshape(nb * S, mw),
                   w_out_bf[:width], w_out_bf[width:], tm=_tile(nb * S, 1024), tn=1024)


def kernel(x, mem, rel_bias, ffn_norm, ffn_w_in, ffn_w_out, mix_norm, mem_norm, mem_w_kv, mem_q_gain, mem_k_gain, ssd_w_in, ssd_conv_w, ssd_conv_b, ssd_dt_bias, ssd_A_log, ssd_D, ssd_norm, ssd_w_out, dil_w_in, dil_q_gain, dil_k_gain, dil_w_out):
    nb, S, D = x.shape
    depth = ffn_norm.shape[0]
    T = nb * S
    x2 = x.reshape(T, D)
    d_ff = ffn_w_out.shape[2]
    tm_ffn, tf = _tile(T, 1024), _tile(d_ff, 512)
    tm_ffn_norm, tf_norm = tm_ffn, tf
    ffn_w_in = ffn_w_in.astype(BF16)
    ffn_w_out = ffn_w_out.astype(BF16)
    for i in range(depth):
        x2, h2 = ffn(x2, ffn_norm[i, 0], ffn_w_in, ffn_w_out, i, 0, tm_ffn_norm, tf_norm, next_gain=mix_norm[i])
        h3 = h2.reshape(nb, S, D)
        kmem, vmem_ = mem_kv(mem, mem_norm[i], mem_w_kv[i].astype(BF16), mem_k_gain[i])
        j = i // 2
        if i % 2 == 0:
            x2 = _ssd_mixer(x2, h3, kmem, vmem_, ssd_w_in[j], ssd_conv_w[j], ssd_conv_b[j], ssd_dt_bias[j],
                            ssd_A_log[j], ssd_D[j], ssd_norm[j], ssd_w_out[j], mem_q_gain[i])
        else:
            x2 = _dil_mixer(x2, h3, kmem, vmem_, dil_w_in[j], dil_q_gain[j], dil_k_gain[j], dil_w_out[j],
                            rel_bias, mem_q_gain[i])
        x2 = ffn(x2, ffn_norm[i, 1], ffn_w_in, ffn_w_out, i, 1, tm_ffn, tf)
    return x2.reshape(nb, S, D)
```

```python
import functools
import math

import jax
import jax.numpy as jnp
import numpy as np
from jax import lax
from jax.experimental import pallas as pl
from jax.experimental.pallas import tpu as pltpu

F32 = jnp.float32
BF16 = jnp.bfloat16
EPS = 1e-6

MEM_HEADS = 4
SSD_HEAD_DIM = 64
SSD_GROUPS = 8
SSD_HEADS_PER_GROUP = 6
SSD_STATE = 128
SSD_CONV = 5
SSD_CHUNK = 128
DIL_CONFIGS = ((128, 1), (512, 4), (2048, 16))
DIL_HEADS = 8
DIL_HEAD_DIM = 128
DIL_HALF = 64
DIL_SEGS = 16
DIL_QUERY_BLOCK = 128
REL_BUCKETS = 32
REL_MAX_DISTANCE = 1024

LANES = 128
VMEM_LIMIT_BYTES = 62 * 1024 * 1024
GROUP_LANES = 16


def _cparams(*sem, flags=None):
    return pltpu.CompilerParams(dimension_semantics=sem, vmem_limit_bytes=VMEM_LIMIT_BYTES, flags=flags)


def _rms(x, gain):
    ms = jnp.mean(x * x, axis=-1, keepdims=True)
    return x * lax.rsqrt(ms + EPS) * gain


def _dot(a, b):
    return jnp.dot(a, b, preferred_element_type=F32)


def _dot_nt(a, b):
    return lax.dot_general(a, b, (((1,), (1,)), ((), ())), preferred_element_type=F32)


FFN_SUB_ROWS = 512
FFN_NORM_ROWS = 32


def _ffn_kernel(x_ref, g_ref, wg_ref, wu_ref, wo_ref, *rest, emit_norm):
    if emit_norm:
        g2_ref, o_ref, hn_ref = rest
        h_ref = hn_ref
    else:
        o_ref, h_ref = rest
    j = pl.program_id(1)

    tm = x_ref.shape[0]
    sub = min(tm, FFN_SUB_ROWS)

    @pl.when(j == 0)
    def _():
        def norm_rows(r, carry):
            rows = pl.ds(pl.multiple_of(r * FFN_NORM_ROWS, FFN_NORM_ROWS), FFN_NORM_ROWS)
            h_ref[rows, :] = _rms(x_ref[rows, :], g_ref[...]).astype(BF16)
            return carry

        lax.fori_loop(0, tm // FFN_NORM_ROWS, norm_rows, 0, unroll=4)

    def step(base_ref):
        for r0 in range(0, tm, sub):
            h = h_ref[r0:r0 + sub, :]
            gate = _dot(h, wg_ref[...])
            up = _dot(h, wu_ref[...])
            a = (jax.nn.silu(gate) * up * 0.5).astype(BF16)
            o_ref[r0:r0 + sub, :] = base_ref[r0:r0 + sub, :] + _dot(a, wo_ref[...])

    pl.when(j == 0)(lambda: step(x_ref))
    pl.when(j > 0)(lambda: step(o_ref))

    if emit_norm:
        @pl.when(j == pl.num_programs(1) - 1)
        def _():
            for r0 in range(0, tm, sub // 2):
                rows = slice(r0, r0 + sub // 2)
                hn_ref[rows, :] = _rms(o_ref[rows, :], g2_ref[...]).astype(hn_ref.dtype)


def ffn(x2, gain, w_in, w_out, layer, which, tm, tf, next_gain=None):
    T, D = x2.shape
    F = w_out.shape[2]
    nf = F // tf
    emit_norm = next_gain is not None
    row_spec = pl.BlockSpec((tm, D), lambda i, j: (i, 0))
    vec_spec = pl.BlockSpec((1, D), lambda i, j: (0, 0))
    in_specs = [row_spec, vec_spec,
                pl.BlockSpec((None, None, D, tf), lambda i, j: (layer, which, 0, j)),
                pl.BlockSpec((None, None, D, tf), lambda i, j: (layer, which, 0, j + nf)),
                pl.BlockSpec((None, None, tf, D), lambda i, j: (layer, which, j, 0))]
    args = [x2, gain.reshape(1, D), w_in, w_in, w_out]
    out_specs, out_shape = row_spec, jax.ShapeDtypeStruct((T, D), F32)
    if emit_norm:
        in_specs.append(vec_spec)
        args.append(next_gain.reshape(1, D))
        out_specs, out_shape = [row_spec, row_spec], [out_shape, jax.ShapeDtypeStruct((T, D), BF16)]
    return pl.pallas_call(
        functools.partial(_ffn_kernel, emit_norm=emit_norm),
        grid=(T // tm, nf),
        in_specs=in_specs,
        out_specs=out_specs,
        out_shape=out_shape,
        scratch_shapes=[] if emit_norm else [pltpu.VMEM((tm, D), BF16)],
        compiler_params=_cparams("parallel", "arbitrary"),
        name="ffn",
    )(*args)


PROJ_CHUNK = 256


def _proj_kernel(h_ref, w_ref, g_ref, o_ref, *scratch, norm_width, norm_tiles, r):
    tm, tn = h_ref.shape[1], w_ref.shape[1]
    rows_per = tm // r
    nw = norm_width if norm_width else PROJ_CHUNK
    assert PROJ_CHUNK % nw == 0 or nw % PROJ_CHUNK == 0
    chunk = max(PROJ_CHUNK, nw)

    def emit(normed):
        for c0 in range(0, tn, chunk):
            res = _dot(h_ref[0], w_ref[:, c0:c0 + chunk])
            if r == 1:
                if not normed:
                    o_ref[0, 0, :, c0:c0 + chunk] = res.astype(o_ref.dtype)
                    continue
                for c in range(c0, c0 + chunk, nw):
                    blk = res[:, c - c0:c - c0 + nw]
                    o_ref[0, 0, :, c:c + nw] = _rms(blk, g_ref[0, :, c:c + nw]).astype(o_ref.dtype)
                continue
            acc_ref, = scratch
            for c in range(c0, c0 + chunk, LANES):
                acc_ref[c // LANES] = res[:, c - c0:c - c0 + LANES]
            for m in range(r):
                rows = pl.ds(m, rows_per, stride=r)
                for c in range(c0, c0 + chunk, LANES):
                    blk = acc_ref[c // LANES, rows, :]
                    if normed:
                        blk = _rms(blk, g_ref[0, :, c:c + LANES])
                    o_ref[0, m, :, c:c + LANES] = blk.astype(o_ref.dtype)

    if r > 1:
        assert norm_width in (0, LANES)
    if norm_width == 0:
        emit(False)
    elif norm_tiles is None:
        emit(True)
    else:
        j = pl.program_id(2)
        pl.when(j < norm_tiles)(lambda: emit(True))
        pl.when(j >= norm_tiles)(lambda: emit(False))


def proj(h3, w, col0, ncols, gains, *, out_dtype, tm, tn, norm_width=0, norm_tiles=None, r=1):
    nb, S, K = h3.shape
    assert col0 % tn == 0 and ncols % tn == 0 and S % tm == 0 and tm % (8 * r) == 0
    j0 = col0 // tn
    nj = ncols // tn
    if gains is None:
        gains = jnp.ones((nj, 1, tn), F32)
    scratch = [] if r == 1 else [pltpu.VMEM((tn // LANES, tm, LANES), F32)]
    kern = functools.partial(_proj_kernel, norm_width=norm_width, norm_tiles=norm_tiles, r=r)
    return pl.pallas_call(
        kern,
        grid=(nb, S // tm, nj),
        in_specs=[pl.BlockSpec((1, tm, K), lambda b, i, j: (b, i, 0)),
                  pl.BlockSpec((K, tn), lambda b, i, j: (0, j0 + j)),
                  pl.BlockSpec((1, 1, tn), lambda b, i, j: (j, 0, 0))],
        out_specs=pl.BlockSpec((1, r, tm // r, tn), lambda b, i, j: (b, 0, i, j)),
        out_shape=jax.ShapeDtypeStruct((nb, r, S // r, ncols), out_dtype),
        scratch_shapes=scratch,
        compiler_params=_cparams("parallel", "parallel", "arbitrary"),
        name="proj",
    )(h3, w, gains)


def _side_proj_kernel(h_ref, w_ref, g_ref, q_ref, dt_ref, *, head_dim):
    mw = q_ref.shape[2]
    for c in range(mw // head_dim):
        cols = slice(c * head_dim, (c + 1) * head_dim)
        q_ref[0, :, cols] = _rms(_dot(h_ref[0], w_ref[:, cols]), g_ref[...]).astype(q_ref.dtype)
    dt_ref[0] = _dot(h_ref[0], w_ref[:, mw:])


def side_proj(h3, w, q_gain, tm):
    nb, S, K = h3.shape
    head_dim = q_gain.shape[0]
    n = w.shape[1]
    mw = n - LANES
    return pl.pallas_call(
        functools.partial(_side_proj_kernel, head_dim=head_dim),
        grid=(nb, S // tm),
        in_specs=[pl.BlockSpec((1, tm, K), lambda b, i: (b, i, 0)),
                  pl.BlockSpec((K, n), lambda b, i: (0, 0)),
                  pl.BlockSpec((1, head_dim), lambda b, i: (0, 0))],
        out_specs=[pl.BlockSpec((1, tm, mw), lambda b, i: (b, i, 0)),
                   pl.BlockSpec((1, tm, LANES), lambda b, i: (b, i, 0))],
        out_shape=[jax.ShapeDtypeStruct((nb, S, mw), BF16), jax.ShapeDtypeStruct((nb, S, LANES), F32)],
        compiler_params=_cparams("parallel", "parallel"),
        name="side_proj",
    )(h3, w, q_gain.reshape(1, head_dim))


def _outproj_kernel(x_ref, a1_ref, a2_ref, w1_ref, w2_ref, o_ref):
    for c0 in range(0, o_ref.shape[1], PROJ_CHUNK):
        cols = slice(c0, c0 + PROJ_CHUNK)
        o_ref[:, cols] = x_ref[:, cols] + _dot(a1_ref[...], w1_ref[:, cols]) + _dot(a2_ref[...], w2_ref[:, cols])


def outproj(x2, a1, a2, w1, w2, tm, tn):
    T, D = x2.shape
    k1, k2 = a1.shape[1], a2.shape[1]
    return pl.pallas_call(
        _outproj_kernel,
        grid=(T // tm, D // tn),
        in_specs=[pl.BlockSpec((tm, tn), lambda i, j: (i, j)),
                  pl.BlockSpec((tm, k1), lambda i, j: (i, 0)),
                  pl.BlockSpec((tm, k2), lambda i, j: (i, 0)),
                  pl.BlockSpec((k1, tn), lambda i, j: (0, j)),
                  pl.BlockSpec((k2, tn), lambda i, j: (0, j))],
        out_specs=pl.BlockSpec((tm, tn), lambda i, j: (i, j)),
        out_shape=jax.ShapeDtypeStruct((T, D), F32),
        compiler_params=_cparams("parallel", "arbitrary"),
        name="outproj",
    )(x2, a1, a2, w1, w2)


def _memkv_kernel(mem_ref, g_ref, w_ref, kg_ref, k_ref, v_ref):
    mw = k_ref.shape[2]
    hd = kg_ref.shape[1]
    memn = _rms(mem_ref[0], g_ref[...]).astype(BF16)
    kv = _dot(memn, w_ref[...])
    for hh in range(mw // hd):
        cols = slice(hh * hd, (hh + 1) * hd)
        k_ref[0, :, cols] = _rms(kv[:, cols], kg_ref[...]).astype(BF16)
    v_ref[0] = kv[:, mw:].astype(BF16)


def mem_kv(mem, mem_gain, w_kv, k_gain):
    nb, M, D = mem.shape
    mw = w_kv.shape[1] // 2
    hd = k_gain.shape[0]
    out = jax.ShapeDtypeStruct((nb, M, mw), BF16)
    return pl.pallas_call(
        _memkv_kernel,
        grid=(nb,),
        in_specs=[pl.BlockSpec((1, M, D), lambda b: (b, 0, 0)),
                  pl.BlockSpec((1, D), lambda b: (0, 0)),
                  pl.BlockSpec((D, 2 * mw), lambda b: (0, 0)),
                  pl.BlockSpec((1, hd), lambda b: (0, 0))],
        out_specs=[pl.BlockSpec((1, M, mw), lambda b: (b, 0, 0))] * 2,
        out_shape=[out, out],
        compiler_params=_cparams("parallel"),
        name="mem_kv",
    )(mem, mem_gain.reshape(1, D), w_kv, k_gain.reshape(1, hd))


def _memattn_kernel(q_ref, k_ref, v_ref, o_ref, *, heads):
    hd = q_ref.shape[2] // heads
    scale = hd ** -0.5
    for hh in range(heads):
        cols = slice(hh * hd, (hh + 1) * hd)
        s = _dot_nt(q_ref[0, :, cols], k_ref[0, :, cols]) * scale
        e = jnp.exp(s - jnp.max(s, axis=-1, keepdims=True))
        p = e / jnp.sum(e, axis=-1, keepdims=True)
        o_ref[0, :, cols] = _dot(p.astype(BF16), v_ref[0, :, cols]).astype(o_ref.dtype)


def mem_attention(q, k, v, tq):
    nb, S, mw = q.shape
    M = k.shape[1]
    return pl.pallas_call(
        functools.partial(_memattn_kernel, heads=MEM_HEADS),
        grid=(nb, S // tq),
        in_specs=[pl.BlockSpec((1, tq, mw), lambda b, i: (b, i, 0)),
                  pl.BlockSpec((1, M, mw), lambda b, i: (b, 0, 0)),
                  pl.BlockSpec((1, M, mw), lambda b, i: (b, 0, 0))],
        out_specs=pl.BlockSpec((1, tq, mw), lambda b, i: (b, i, 0)),
        out_shape=jax.ShapeDtypeStruct((nb, S, mw), BF16),
        compiler_params=_cparams("parallel", "parallel"),
        name="mem_attn",
    )(q, k, v)


SSD_GROUP_WIDTH = SSD_HEADS_PER_GROUP * SSD_HEAD_DIM
SSD_GROUP_CONV = SSD_GROUP_WIDTH + 2 * SSD_STATE
SSD_GROUP_COLS = SSD_GROUP_CONV + SSD_GROUP_WIDTH
HALO_ROWS = 16


def _ssd_inproj_kernel(hp_ref, h_ref, hn_ref, w_ref, cw_ref, cb_ref, o_ref, lhs_ref, ext_ref):
    i = pl.program_id(1)
    tm = h_ref.shape[1]
    lo = HALO_ROWS - SSD_CONV // 2

    @pl.when(pl.program_id(2) == 0)
    def _():
        lhs_ref[0:HALO_ROWS] = jnp.where(i > 0, hp_ref[0], jnp.zeros_like(hp_ref[0]))
        lhs_ref[HALO_ROWS:HALO_ROWS + tm] = h_ref[0]
        lhs_ref[HALO_ROWS + tm:] = jnp.where(i < pl.num_programs(1) - 1, hn_ref[0], jnp.zeros_like(hn_ref[0]))

    for c0 in range(0, SSD_GROUP_COLS, PROJ_CHUNK):
        if c0 >= SSD_GROUP_CONV:
            o_ref[0, :, c0:c0 + PROJ_CHUNK] = jax.nn.silu(_dot(h_ref[0], w_ref[:, c0:c0 + PROJ_CHUNK]))
            continue
        half = (tm + 2 * HALO_ROWS) // 2
        for r0 in (0, half):
            res = _dot(lhs_ref[r0:r0 + half, :], w_ref[:, c0:c0 + PROJ_CHUNK])
            for c in range(c0, c0 + PROJ_CHUNK, LANES):
                ext_ref[c // LANES, r0:r0 + half] = res[:, c - c0:c - c0 + LANES]
        for c in range(c0, c0 + PROJ_CHUNK, LANES):
            s = c // LANES
            if c >= SSD_GROUP_CONV:
                o_ref[0, :, c:c + LANES] = jax.nn.silu(ext_ref[s, HALO_ROWS:HALO_ROWS + tm, :])
                continue
            acc = cb_ref[:, c:c + LANES] + cw_ref[0:1, c:c + LANES] * ext_ref[s, lo:lo + tm, :]
            for k in range(1, SSD_CONV):
                acc = acc + cw_ref[k:k + 1, c:c + LANES] * ext_ref[s, lo + k:lo + k + tm, :]
            o_ref[0, :, c:c + LANES] = jax.nn.silu(acc)


def ssd_inproj(h3, w, conv_w, conv_b, tm):
    nb, S, K = h3.shape
    G, gc, cols = SSD_GROUPS, SSD_GROUP_CONV, SSD_GROUP_COLS
    assert PROJ_CHUNK % LANES == 0 and gc % LANES == 0 and cols % PROJ_CHUNK == 0 and tm % HALO_ROWS == 0
    nh = tm // HALO_ROWS
    last = S // HALO_ROWS - 1
    return pl.pallas_call(
        _ssd_inproj_kernel,
        grid=(nb, S // tm, G),
        in_specs=[pl.BlockSpec((1, HALO_ROWS, K), lambda b, i, g: (b, jnp.maximum(i * nh - 1, 0), 0)),
                  pl.BlockSpec((1, tm, K), lambda b, i, g: (b, i, 0)),
                  pl.BlockSpec((1, HALO_ROWS, K), lambda b, i, g: (b, jnp.minimum((i + 1) * nh, last), 0)),
                  pl.BlockSpec((K, cols), lambda b, i, g: (0, g)),
                  pl.BlockSpec((SSD_CONV, gc), lambda b, i, g: (0, g)),
                  pl.BlockSpec((1, gc), lambda b, i, g: (0, g))],
        out_specs=pl.BlockSpec((1, tm, cols), lambda b, i, g: (b, i, g)),
        out_shape=jax.ShapeDtypeStruct((nb, S, G * cols), F32),
        scratch_shapes=[pltpu.VMEM((tm + 2 * HALO_ROWS, K), BF16),
                        pltpu.VMEM((pl.cdiv(gc, PROJ_CHUNK) * PROJ_CHUNK // LANES, tm + 2 * HALO_ROWS, LANES), F32)],
        compiler_params=_cparams("parallel", "parallel", "arbitrary"),
        name="ssd_inproj",
    )(h3, h3, h3, w, conv_w, conv_b.reshape(1, G * gc))


LOG2E = math.log2(math.e)


def _ssd_prep_kernel(dtr_ref, bias_ref, alog_ref, tr_ref):
    Q = SSD_CHUNK
    ii = lax.broadcasted_iota(jnp.int32, (Q, Q), 0)
    jj = lax.broadcasted_iota(jnp.int32, (Q, Q), 1)
    tril = (jj <= ii).astype(F32)
    triu = (jj >= ii).astype(F32)
    lane = lax.broadcasted_iota(jnp.int32, (Q, LANES), 1)
    is_bwd = (lane % GROUP_LANES) >= GROUP_LANES // 2
    hi = lax.Precision.HIGHEST
    for i in range(tr_ref.shape[1]):
        x = dtr_ref[0, i * Q:(i + 1) * Q, :] + bias_ref[...]
        dt = jnp.maximum(x, 0.0) + jnp.log1p(jnp.exp(-jnp.abs(x)))
        a = dt * (-jnp.exp(alog_ref[...]))
        cs_fwd = jnp.dot(tril, a, precision=hi, preferred_element_type=F32)
        cs_bwd = jnp.dot(triu, a, precision=hi, preferred_element_type=F32)
        cs = jnp.where(is_bwd, cs_bwd, cs_fwd)
        last = jnp.where(is_bwd[0:1], cs[0:1], cs[Q - 1:Q])
        cs2 = cs * LOG2E
        parts = (cs2, cs2 - jnp.log2(dt), jnp.exp(last - cs) * dt, jnp.broadcast_to(jnp.exp(last), (Q, LANES)))
        for p, part in enumerate(parts):
            rows = part.T
            for g in range(SSD_GROUPS):
                tr_ref[0, i, g, p * GROUP_LANES:(p + 1) * GROUP_LANES] = rows[g * GROUP_LANES:(g + 1) * GROUP_LANES]


def ssd_prep(dtr, bias_l, alog_l):
    nb, S, _ = dtr.shape
    Q = SSD_CHUNK
    nc = S // Q
    kp = next(k for k in (8, 4, 2, 1) if nc % k == 0)
    vec_spec = pl.BlockSpec((1, LANES), lambda b, c: (0, 0))
    return pl.pallas_call(
        _ssd_prep_kernel,
        grid=(nb, nc // kp),
        in_specs=[pl.BlockSpec((1, kp * Q, LANES), lambda b, c: (b, c, 0)), vec_spec, vec_spec],
        out_specs=pl.BlockSpec((1, kp, SSD_GROUPS, 4 * GROUP_LANES, Q), lambda b, c: (b, c, 0, 0, 0)),
        out_shape=jax.ShapeDtypeStruct((nb, nc, SSD_GROUPS, 4 * GROUP_LANES, Q), F32),
        compiler_params=_cparams("parallel", "parallel"),
        name="ssd_prep",
    )(dtr, bias_l, alog_l)


def _ssd_direction(d, g, sub, xc_ref, tr_ref, st_ref):
    Q = SSD_CHUNK
    P, N, gw = SSD_HEAD_DIM, SSD_STATE, SSD_GROUP_WIDTH
    rows = slice(sub * Q, (sub + 1) * Q)
    xs = xc_ref[0, rows, 0:gw]
    bm = xc_ref[0, rows, gw:gw + N]
    cm = xc_ref[0, rows, gw + N:gw + 2 * N]
    cb = _dot_nt(cm.astype(BF16), bm.astype(BF16))
    bt = bm.T

    cs2T, rowT, wT, cdT = (tr_ref[0, sub, 0, part * GROUP_LANES:(part + 1) * GROUP_LANES, :] for part in range(4))
    ii = lax.broadcasted_iota(jnp.int32, (Q, Q), 0)
    jj = lax.broadcasted_iota(jnp.int32, (Q, Q), 1)
    causal = (jj <= ii) if d == 0 else (jj >= ii)
    lo = lax.broadcasted_iota(jnp.int32, (Q, LANES), 1) < P
    st = st_ref[d]
    ys, sts = [], []
    for p in range(SSD_HEADS_PER_GROUP // 2):
        cols = slice(p * LANES, (p + 1) * LANES)
        ms, ss = [], []
        k0 = d * (GROUP_LANES // 2) + 2 * p
        for k in (k0, k0 + 1):
            col2 = jnp.broadcast_to(cs2T[k:k + 1, :], (Q, Q)).T
            decay_dt = jnp.exp2(jnp.where(causal, col2 - rowT[k:k + 1, :], -jnp.inf))
            ms.append((cb * decay_dt).astype(BF16))
            ms.append((cm * jnp.exp2(col2)).astype(BF16))
            ss.append((bt * wT[k:k + 1, :]).astype(BF16))
        cd = jnp.where(lo[0:1], cdT[k0:k0 + 1, :], cdT[k0 + 1:k0 + 2, :])
        xp, sp = xs[:, cols], st[:, cols]
        x_lo, x_hi = jnp.where(lo, xp, 0.0).astype(BF16), jnp.where(lo, 0.0, xp).astype(BF16)
        s_lo, s_hi = jnp.where(lo, sp, 0.0).astype(BF16), jnp.where(lo, 0.0, sp).astype(BF16)
        ys.append(_dot(jnp.concatenate(ms, axis=1), jnp.concatenate([x_lo, s_lo, x_hi, s_hi], axis=0)))
        sts.append(cd * sp + _dot(jnp.concatenate(ss, axis=1), jnp.concatenate([x_lo, x_hi], axis=0)))
    st_ref[d] = jnp.concatenate(sts, axis=1)
    return jnp.concatenate(ys, axis=1)


def _ssd_scan_kernel(xcf, xcb, trf, trb, dskip_ref, ng_ref, y_ref, st_ref, ysum_ref, *, kc):
    g = pl.program_id(1)
    c = pl.program_id(2)
    nsteps = pl.num_programs(2)
    Q = SSD_CHUNK

    @pl.when(c == 0)
    def _():
        st_ref[...] = jnp.zeros_like(st_ref)
        ysum_ref[...] = jnp.zeros_like(ysum_ref)

    def finish(y_dir, row0, xc_ref, sub):
        rows = pl.ds(pl.multiple_of(row0, Q), Q)
        blk = slice(sub * Q, (sub + 1) * Q)
        tot = ysum_ref[rows, :] + y_dir + dskip_ref[...] * xc_ref[0, blk, 0:SSD_GROUP_WIDTH]
        ysum_ref[rows, :] = y_dir
        gated = tot * xc_ref[0, blk, SSD_GROUP_CONV:SSD_GROUP_COLS]
        y_ref[0, rows, :] = _rms(gated, ng_ref[...]).astype(y_ref.dtype)

    for i in range(kc):
        y_f = _ssd_direction(0, g, i, xcf, trf, st_ref)
        finish(y_f, (c * kc + i) * Q, xcf, i)
        y_b = _ssd_direction(1, g, kc - 1 - i, xcb, trb, st_ref)
        finish(y_b, ((nsteps - 1 - c) * kc + kc - 1 - i) * Q, xcb, kc - 1 - i)


def ssd_scan(xc, tr, dskip, norm_g, kc):
    nb, S, _ = xc.shape
    Q = SSD_CHUNK
    nsteps = S // (Q * kc)
    assert nsteps % 2 == 0 and Q == LANES
    G, N, gw = SSD_GROUPS, SSD_STATE, SSD_GROUP_WIDTH
    inner = G * gw

    def both(shape, f):
        return [pl.BlockSpec(shape, lambda b, g, c: f(b, g, c)),
                pl.BlockSpec(shape, lambda b, g, c: f(b, g, nsteps - 1 - c))]

    in_specs = (both((1, kc * Q, SSD_GROUP_COLS), lambda b, g, c: (b, c, g))
                + both((1, kc, 1, 4 * GROUP_LANES, Q), lambda b, g, c: (b, c, g, 0, 0))
                + [pl.BlockSpec((1, gw), lambda b, g, c: (0, g)),
                   pl.BlockSpec((1, gw), lambda b, g, c: (0, g))])
    return pl.pallas_call(
        functools.partial(_ssd_scan_kernel, kc=kc),
        grid=(nb, G, nsteps),
        in_specs=in_specs,
        out_specs=pl.BlockSpec((1, S, gw), lambda b, g, c: (b, 0, g)),
        out_shape=jax.ShapeDtypeStruct((nb, S, inner), BF16),
        scratch_shapes=[pltpu.VMEM((2, N, gw), F32), pltpu.VMEM((S, gw), F32)],
        compiler_params=_cparams("parallel", "parallel", "arbitrary"),
        name="ssd_scan",
    )(xc, xc, tr, tr, dskip.reshape(1, inner), norm_g.reshape(1, inner))


def _t5_bucket_np(rel):
    half = REL_BUCKETS // 2
    exact = half // 2
    n = np.abs(rel)
    far = exact + (np.log(np.maximum(n, 1).astype(np.float32) / np.float32(exact))
                   / np.float32(math.log(REL_MAX_DISTANCE / exact)) * np.float32(half - exact)).astype(np.int32)
    far = np.minimum(far, half - 1)
    return np.where(rel > 0, half, 0) + np.where(n < exact, n, far)


def _dil_bucket_rows(win):
    d = np.arange(win)
    rel = d - DIL_HALF
    rows = []
    for _, dilation in DIL_CONFIGS:
        rows.append(np.where(d <= 2 * DIL_HALF, _t5_bucket_np(rel * dilation), -1))
    return np.broadcast_to(np.stack(rows)[:, None, :], (len(DIL_CONFIGS), 8, win)).astype(np.int32)


def _dil_kernel(tbl_ref, bm_ref, *refs, seq, seg, tq):
    ngroups = len(DIL_CONFIGS)
    win = tq + 2 * DIL_HALF
    in_refs = refs[:7 * ngroups]
    o_ref = refs[7 * ngroups]
    bias_ref, m_ref, l_ref, a_ref, kext_ref, vext_ref = refs[7 * ngroups + 1:]
    h = pl.program_id(1)
    t = pl.program_id(2)
    scale = DIL_HEAD_DIM ** -0.5

    @pl.when(t == 0)
    def _():
        m_ref[...] = jnp.full_like(m_ref, -jnp.inf)
        l_ref[...] = jnp.zeros_like(l_ref)
        a_ref[...] = jnp.zeros_like(a_ref)
        kcol = lax.broadcasted_iota(jnp.int32, (tq, win), 1)
        for gi in range(ngroups):
            ids = bm_ref[gi]
            row = jnp.zeros(ids.shape, F32)
            for u in range(REL_BUCKETS):
                row = jnp.where(ids == u, tbl_ref[u, gi * DIL_HEADS + h], row)
            row = jnp.where(ids < 0, -jnp.inf, row)
            band = pltpu.roll(jnp.broadcast_to(row[0:1], (tq, win)), 0, 1, stride=1, stride_axis=0)
            no_left = jnp.where(kcol >= DIL_HALF, band, -jnp.inf)
            bias_ref[gi, 0] = band
            bias_ref[gi, 1] = no_left
            bias_ref[gi, 2] = jnp.where(kcol < win - DIL_HALF, band, -jnp.inf)
            bias_ref[gi, 3] = jnp.where(kcol < win - DIL_HALF, no_left, -jnp.inf)

    nblk = seg // tq
    for gi, (_, r) in enumerate(DIL_CONFIGS):
        q_ref, kp, kc, kn, vp, vc, vn = in_refs[7 * gi:7 * gi + 7]
        sub_len = seq // r
        segs_per_sub = sub_len // seg
        m_res = t // segs_per_sub
        j0 = (t % segs_per_sub) * seg
        for ext, prev, cur, nxt in ((kext_ref, kp, kc, kn), (vext_ref, vp, vc, vn)):
            ext[gi, 0:DIL_HALF] = prev[0]
            ext[gi, DIL_HALF:DIL_HALF + seg] = cur[0]
            ext[gi, DIL_HALF + seg:] = nxt[0]
        seg_in_sub = t % segs_per_sub
        for jb in range(nblk):
            q = q_ref[0, jb * tq:(jb + 1) * tq, :]
            kw = kext_ref[gi, jb * tq:jb * tq + win, :]
            vw = vext_ref[gi, jb * tq:jb * tq + win, :]
            variant = 0
            if jb == 0:
                variant = variant + (seg_in_sub == 0).astype(jnp.int32)
            if jb == nblk - 1:
                variant = variant + 2 * (seg_in_sub == segs_per_sub - 1).astype(jnp.int32)
            s = _dot_nt(q, kw) * scale + bias_ref[gi, variant]
            mb = jnp.max(s, axis=-1, keepdims=True)
            e = jnp.exp(s - mb)
            lb = jnp.sum(e, axis=-1, keepdims=True)
            acc = _dot(e.astype(BF16), vw)
            start = (j0 + jb * tq) * r + m_res
            rows = pl.ds(start, tq, stride=r) if r > 1 else pl.ds(start, tq)
            m_old = m_ref[rows, :]
            m_new = jnp.maximum(m_old, mb)
            alpha = jnp.exp(m_old - m_new)
            beta = jnp.exp(mb - m_new)
            l_new = alpha * l_ref[rows, :] + beta * lb
            a_new = alpha * a_ref[rows, :] + beta * acc
            m_ref[rows, :] = m_new
            l_ref[rows, :] = l_new
            a_ref[rows, :] = a_new

    @pl.when(t == pl.num_programs(2) - 1)
    def _():
        o_ref[0] = (a_ref[...] / l_ref[...]).astype(o_ref.dtype)


def dilated_attention(qkvs, rel_bias):
    nb, S, _ = qkvs[0].shape
    hd = DIL_HEAD_DIM
    seg = S // DIL_SEGS
    tq = _tile(seg, DIL_QUERY_BLOCK)
    win = tq + 2 * DIL_HALF
    assert seg % tq == 0 and tq % LANES == 0
    assert all(w == 2 * DIL_HALF * r and (S // r) % seg == 0 for w, r in DIL_CONFIGS)
    nhb = seg // DIL_HALF
    last_hb = S // DIL_HALF - 1
    in_specs = [pl.BlockSpec(memory_space=pltpu.SMEM),
                pl.BlockSpec((len(DIL_CONFIGS), 8, win), lambda b, h, t: (0, 0, 0))]
    args = [rel_bias, jnp.asarray(_dil_bucket_rows(win))]
    for g in range(len(DIL_CONFIGS)):
        in_specs.append(pl.BlockSpec((1, seg, hd), lambda b, h, t: (b, t, h)))
        for part in (1, 2):
            off = part * DIL_HEADS
            in_specs += [
                pl.BlockSpec((1, DIL_HALF, hd), lambda b, h, t, off=off: (b, jnp.maximum(t * nhb - 1, 0), off + h)),
                pl.BlockSpec((1, seg, hd), lambda b, h, t, off=off: (b, t, off + h)),
                pl.BlockSpec((1, DIL_HALF, hd), lambda b, h, t, off=off: (b, jnp.minimum((t + 1) * nhb, last_hb), off + h)),
            ]
        args += [qkvs[g]] * 7
    kern = functools.partial(_dil_kernel, seq=S, seg=seg, tq=tq)
    return pl.pallas_call(
        kern,
        grid=(nb, DIL_HEADS, DIL_SEGS),
        in_specs=in_specs,
        out_specs=pl.BlockSpec((1, S, hd), lambda b, h, t: (b, 0, h)),
        out_shape=jax.ShapeDtypeStruct((nb, S, DIL_HEADS * hd), BF16),
        scratch_shapes=[pltpu.VMEM((len(DIL_CONFIGS), 4, tq, win), F32),
                        pltpu.VMEM((S, hd), F32), pltpu.VMEM((S, hd), F32), pltpu.VMEM((S, hd), F32),
                        pltpu.VMEM((len(DIL_CONFIGS), seg + 2 * DIL_HALF, hd), BF16),
                        pltpu.VMEM((len(DIL_CONFIGS), seg + 2 * DIL_HALF, hd), BF16)],
        compiler_params=_cparams("parallel", "parallel", "arbitrary"),
        name="dil_attn",
    )(*args)


def _tile(n, pref):
    return pref if n % pref == 0 else n


def _ssd_lane_layout(v):
    v = v.reshape(2, SSD_GROUPS, SSD_HEADS_PER_GROUP).transpose(1, 0, 2)
    v = jnp.pad(v, ((0, 0), (0, 0), (0, GROUP_LANES // 2 - SSD_HEADS_PER_GROUP)))
    return v.reshape(1, LANES)


def _ssd_mixer(x2, h3, kmem, vmem_, w_in, conv_w, conv_b, dt_bias, a_log, d_skip, norm_g, w_out, mem_q_gain):
    nb, S, D = h3.shape
    inner = SSD_GROUPS * SSD_HEADS_PER_GROUP * SSD_HEAD_DIM
    conv_ch = inner + 2 * SSD_GROUPS * SSD_STATE
    nheads = SSD_GROUPS * SSD_HEADS_PER_GROUP
    mw = kmem.shape[2]
    tm = _tile(S, 1024)
    w_bf = w_in.astype(BF16)

    def grouped(t, z=None):
        lead = t.shape[:-1]
        parts = [t[..., :inner].reshape(*lead, SSD_GROUPS, SSD_GROUP_WIDTH),
                 t[..., inner:inner + SSD_GROUPS * SSD_STATE].reshape(*lead, SSD_GROUPS, SSD_STATE),
                 t[..., inner + SSD_GROUPS * SSD_STATE:].reshape(*lead, SSD_GROUPS, SSD_STATE)]
        if z is not None:
            parts.append(z.reshape(*lead, SSD_GROUPS, SSD_GROUP_WIDTH))
        return jnp.concatenate(parts, axis=-1).reshape(*lead, -1)

    w_p1 = grouped(w_bf[:, inner:inner + conv_ch], z=w_bf[:, :inner])
    xc = ssd_inproj(h3, w_p1, grouped(conv_w), grouped(conv_b), tm)
    w_dt = w_in[:, inner + conv_ch:inner + conv_ch + 2 * nheads]
    w_dt = w_dt.reshape(D, 2, SSD_GROUPS, SSD_HEADS_PER_GROUP).transpose(0, 2, 1, 3)
    w_dt = jnp.pad(w_dt, ((0, 0), (0, 0), (0, 0), (0, GROUP_LANES // 2 - SSD_HEADS_PER_GROUP)))
    q_off = inner + conv_ch + 2 * nheads
    w_side = jnp.concatenate([w_bf[:, q_off:], w_dt.reshape(D, LANES).astype(BF16)], axis=1)
    q_mem, dtr = side_proj(h3, w_side, mem_q_gain, tm)

    tr = ssd_prep(dtr, _ssd_lane_layout(dt_bias), _ssd_lane_layout(a_log))
    dskip = jnp.repeat(d_skip, SSD_HEAD_DIM)
    nchunks = S // SSD_CHUNK
    kc = next(k for k in (8, 4, 2, 1) if nchunks % (2 * k) == 0)
    y = ssd_scan(xc, tr, dskip, norm_g, kc)
    o_mem = mem_attention(q_mem, kmem, vmem_, tq=tm)
    w_out_bf = w_out.astype(BF16)
    return outproj(x2, y.reshape(nb * S, inner), o_mem.reshape(nb * S, mw),
                   w_out_bf[:inner], w_out_bf[inner:], tm=_tile(nb * S, 1024), tn=1024)


def _dil_mixer(x2, h3, kmem, vmem_, w_in, q_gain, k_gain, w_out, rel_bias, mem_q_gain):
    nb, S, D = h3.shape
    width = DIL_HEADS * DIL_HEAD_DIM
    mw = kmem.shape[2]
    tm = _tile(S, 1024)
    w_bf = w_in.astype(BF16)
    qkvs = []
    for g, (_, r) in enumerate(DIL_CONFIGS):
        gains = jnp.stack([jnp.tile(q_gain[g], DIL_HEADS), jnp.tile(k_gain[g], DIL_HEADS),
                           jnp.ones((width,), F32)]).reshape(3, 1, width)
        out = proj(h3, w_bf, g * 3 * width, 3 * width, gains, out_dtype=BF16, tm=tm, tn=width,
                   norm_width=DIL_HEAD_DIM, norm_tiles=2, r=r)
        qkvs.append(out.reshape(nb, S, 3 * width))
    q_off = len(DIL_CONFIGS) * 3 * width
    hd = mem_q_gain.shape[0]
    qg = jnp.tile(mem_q_gain, mw // hd).reshape(1, 1, mw)
    q_mem = proj(h3, w_bf, q_off, mw, qg, out_dtype=BF16, tm=tm, tn=mw, norm_width=hd)[:, 0]
    o = dilated_attention(qkvs, rel_bias)
    o_mem = mem_attention(q_mem, kmem, vmem_, tq=tm)
    w_out_bf = w_out.astype(BF16)
    return outproj(x2, o.reshape(nb * S, width), o_mem.reshape(nb * S, mw),
                   w_out_bf[:width], w_out_bf[width:], tm=_tile(nb * S, 1024), tn=1024)


def kernel(x, mem, rel_bias, ffn_norm, ffn_w_in, ffn_w_out, mix_norm, mem_norm, mem_w_kv, mem_q_gain, mem_k_gain, ssd_w_in, ssd_conv_w, ssd_conv_b, ssd_dt_bias, ssd_A_log, ssd_D, ssd_norm, ssd_w_out, dil_w_in, dil_q_gain, dil_k_gain, dil_w_out):
    nb, S, D = x.shape
    depth = ffn_norm.shape[0]
    T = nb * S
    x2 = x.reshape(T, D)
    d_ff = ffn_w_out.shape[2]
    tm_ffn, tf = _tile(T, 1024), _tile(d_ff, 512)
    tm_ffn_norm, tf_norm = tm_ffn, tf
    ffn_w_in = ffn_w_in.astype(BF16)
    ffn_w_out = ffn_w_out.astype(BF16)
    for i in range(depth):
        x2, h2 = ffn(x2, ffn_norm[i, 0], ffn_w_in, ffn_w_out, i, 0, tm_ffn_norm, tf_norm, next_gain=mix_norm[i])
        h3 = h2.reshape(nb, S, D)
        kmem, vmem_ = mem_kv(mem, mem_norm[i], mem_w_kv[i].astype(BF16), mem_k_gain[i])
        j = i // 2
        if i % 2 == 0:
            x2 = _ssd_mixer(x2, h3, kmem, vmem_, ssd_w_in[j], ssd_conv_w[j], ssd_conv_b[j], ssd_dt_bias[j],
                            ssd_A_log[j], ssd_D[j], ssd_norm[j], ssd_w_out[j], mem_q_gain[i])
        else:
            x2 = _dil_mixer(x2, h3, kmem, vmem_, dil_w_in[j], dil_q_gain[j], dil_k_gain[j], dil_w_out[j],
                            rel_bias, mem_q_gain[i])
        x2 = ffn(x2, ffn_norm[i, 1], ffn_w_in, ffn_w_out, i, 1, tm_ffn, tf)
    return x2.reshape(nb, S, D)
```

```python
import functools
import math

import jax
import jax.numpy as jnp
import numpy as np
from jax import lax
from jax.experimental import pallas as pl
from jax.experimental.pallas import tpu as pltpu

F32 = jnp.float32
BF16 = jnp.bfloat16
EPS = 1e-6

MEM_HEADS = 4
SSD_HEAD_DIM = 64
SSD_GROUPS = 8
SSD_HEADS_PER_GROUP = 6
SSD_STATE = 128
SSD_CONV = 5
SSD_CHUNK = 128
DIL_CONFIGS = ((128, 1), (512, 4), (2048, 16))
DIL_HEADS = 8
DIL_HEAD_DIM = 128
DIL_HALF = 64
DIL_SEGS = 8
DIL_QUERY_BLOCK = 128
REL_BUCKETS = 32
REL_MAX_DISTANCE = 1024

LANES = 128
VMEM_LIMIT_BYTES = 62 * 1024 * 1024
GROUP_LANES = 16

ROW_TILE = 1024
FF_TILE = 512
OUT_TILE = 1024


def _tile(n, pref):
    return pref if n % pref == 0 else n


def _cparams(*sem):
    return pltpu.CompilerParams(dimension_semantics=sem, vmem_limit_bytes=VMEM_LIMIT_BYTES)


def _rms(x, gain):
    ms = jnp.mean(x * x, axis=-1, keepdims=True)
    return x * lax.rsqrt(ms + EPS) * gain


def _dot(a, b):
    return jnp.dot(a, b, preferred_element_type=F32)


def _dot_nt(a, b):
    return lax.dot_general(a, b, (((1,), (1,)), ((), ())), preferred_element_type=F32)


FFN_SUB_ROWS = 512
FFN_NORM_ROWS = 32


def _ffn_kernel(x_ref, g_ref, wg_ref, wu_ref, wo_ref, *rest, emit_norm):
    if emit_norm:
        g2_ref, o_ref, hn_ref = rest
        h_ref = hn_ref
    else:
        o_ref, h_ref = rest
    j = pl.program_id(1)

    tm = x_ref.shape[0]
    sub = min(tm, FFN_SUB_ROWS)

    @pl.when(j == 0)
    def _():
        def norm_rows(r, carry):
            rows = pl.ds(pl.multiple_of(r * FFN_NORM_ROWS, FFN_NORM_ROWS), FFN_NORM_ROWS)
            h_ref[rows, :] = _rms(x_ref[rows, :], g_ref[...]).astype(BF16)
            return carry

        lax.fori_loop(0, tm // FFN_NORM_ROWS, norm_rows, 0, unroll=4)

    def step(base_ref):
        for r0 in range(0, tm, sub):
            h = h_ref[r0:r0 + sub, :]
            gate = _dot(h, wg_ref[...])
            up = _dot(h, wu_ref[...])
            a = (jax.nn.silu(gate) * up * 0.5).astype(BF16)
            o_ref[r0:r0 + sub, :] = base_ref[r0:r0 + sub, :] + _dot(a, wo_ref[...])

    pl.when(j == 0)(lambda: step(x_ref))
    pl.when(j > 0)(lambda: step(o_ref))

    if emit_norm:
        @pl.when(j == pl.num_programs(1) - 1)
        def _():
            for r0 in range(0, tm, sub // 2):
                rows = slice(r0, r0 + sub // 2)
                hn_ref[rows, :] = _rms(o_ref[rows, :], g2_ref[...]).astype(hn_ref.dtype)


def ffn(x2, gain, w_in, w_out, layer, which, tm, tf, next_gain=None):
    T, D = x2.shape
    F = w_out.shape[2]
    nf = F // tf
    emit_norm = next_gain is not None
    row_spec = pl.BlockSpec((tm, D), lambda i, j: (i, 0))
    vec_spec = pl.BlockSpec((1, D), lambda i, j: (0, 0))
    in_specs = [row_spec, vec_spec,
                pl.BlockSpec((None, None, D, tf), lambda i, j: (layer, which, 0, j)),
                pl.BlockSpec((None, None, D, tf), lambda i, j: (layer, which, 0, j + nf)),
                pl.BlockSpec((None, None, tf, D), lambda i, j: (layer, which, j, 0))]
    args = [x2, gain.reshape(1, D), w_in, w_in, w_out]
    out_specs, out_shape = row_spec, jax.ShapeDtypeStruct((T, D), F32)
    if emit_norm:
        in_specs.append(vec_spec)
        args.append(next_gain.reshape(1, D))
        out_specs, out_shape = [row_spec, row_spec], [out_shape, jax.ShapeDtypeStruct((T, D), BF16)]
    return pl.pallas_call(
        functools.partial(_ffn_kernel, emit_norm=emit_norm),
        grid=(T // tm, nf),
        in_specs=in_specs,
        out_specs=out_specs,
        out_shape=out_shape,
        scratch_shapes=[] if emit_norm else [pltpu.VMEM((tm, D), BF16)],
        compiler_params=_cparams("parallel", "arbitrary"),
        name="ffn",
    )(*args)


PROJ_CHUNK = 256


def _proj_kernel(h_ref, w_ref, g_ref, o_ref, *scratch, norm_width, norm_tiles, r):
    tm, tn = h_ref.shape[1], w_ref.shape[1]
    rows_per = tm // r
    nw = norm_width if norm_width else PROJ_CHUNK
    assert PROJ_CHUNK % nw == 0 or nw % PROJ_CHUNK == 0
    chunk = max(PROJ_CHUNK, nw)

    def emit(normed):
        for c0 in range(0, tn, chunk):
            res = _dot(h_ref[0], w_ref[:, c0:c0 + chunk])
            if r == 1:
                if not normed:
                    o_ref[0, 0, :, c0:c0 + chunk] = res.astype(o_ref.dtype)
                    continue
                for c in range(c0, c0 + chunk, nw):
                    blk = res[:, c - c0:c - c0 + nw]
                    o_ref[0, 0, :, c:c + nw] = _rms(blk, g_ref[0, :, c:c + nw]).astype(o_ref.dtype)
                continue
            acc_ref, = scratch
            for c in range(c0, c0 + chunk, LANES):
                acc_ref[c // LANES] = res[:, c - c0:c - c0 + LANES]
            for m in range(r):
                rows = pl.ds(m, rows_per, stride=r)
                for c in range(c0, c0 + chunk, LANES):
                    blk = acc_ref[c // LANES, rows, :]
                    if normed:
                        blk = _rms(blk, g_ref[0, :, c:c + LANES])
                    o_ref[0, m, :, c:c + LANES] = blk.astype(o_ref.dtype)

    if r > 1:
        assert norm_width in (0, LANES)
    if norm_width == 0:
        emit(False)
    elif norm_tiles is None:
        emit(True)
    else:
        j = pl.program_id(2)
        pl.when(j < norm_tiles)(lambda: emit(True))
        pl.when(j >= norm_tiles)(lambda: emit(False))


def proj(h3, w, col0, ncols, gains, *, out_dtype, tm, tn, norm_width=0, norm_tiles=None, r=1):
    nb, S, K = h3.shape
    assert col0 % tn == 0 and ncols % tn == 0 and S % tm == 0 and tm % (8 * r) == 0
    j0 = col0 // tn
    nj = ncols // tn
    if gains is None:
        gains = jnp.ones((nj, 1, tn), F32)
    scratch = [] if r == 1 else [pltpu.VMEM((tn // LANES, tm, LANES), F32)]
    kern = functools.partial(_proj_kernel, norm_width=norm_width, norm_tiles=norm_tiles, r=r)
    return pl.pallas_call(
        kern,
        grid=(nb, S // tm, nj),
        in_specs=[pl.BlockSpec((1, tm, K), lambda b, i, j: (b, i, 0)),
                  pl.BlockSpec((K, tn), lambda b, i, j: (0, j0 + j)),
                  pl.BlockSpec((1, 1, tn), lambda b, i, j: (j, 0, 0))],
        out_specs=pl.BlockSpec((1, r, tm // r, tn), lambda b, i, j: (b, 0, i, j)),
        out_shape=jax.ShapeDtypeStruct((nb, r, S // r, ncols), out_dtype),
        scratch_shapes=scratch,
        compiler_params=_cparams("parallel", "parallel", "arbitrary"),
        name="proj",
    )(h3, w, gains)


def _side_proj_kernel(h_ref, w_ref, g_ref, q_ref, dt_ref, *, head_dim):
    mw = q_ref.shape[2]
    for c in range(mw // head_dim):
        cols = slice(c * head_dim, (c + 1) * head_dim)
        q_ref[0, :, cols] = _rms(_dot(h_ref[0], w_ref[:, cols]), g_ref[...]).astype(q_ref.dtype)
    dt_ref[0] = _dot(h_ref[0], w_ref[:, mw:])


def side_proj(h3, w, q_gain, tm):
    nb, S, K = h3.shape
    head_dim = q_gain.shape[0]
    n = w.shape[1]
    mw = n - LANES
    return pl.pallas_call(
        functools.partial(_side_proj_kernel, head_dim=head_dim),
        grid=(nb, S // tm),
        in_specs=[pl.BlockSpec((1, tm, K), lambda b, i: (b, i, 0)),
                  pl.BlockSpec((K, n), lambda b, i: (0, 0)),
                  pl.BlockSpec((1, head_dim), lambda b, i: (0, 0))],
        out_specs=[pl.BlockSpec((1, tm, mw), lambda b, i: (b, i, 0)),
                   pl.BlockSpec((1, tm, LANES), lambda b, i: (b, i, 0))],
        out_shape=[jax.ShapeDtypeStruct((nb, S, mw), BF16), jax.ShapeDtypeStruct((nb, S, LANES), F32)],
        compiler_params=_cparams("parallel", "parallel"),
        name="side_proj",
    )(h3, w, q_gain.reshape(1, head_dim))


def _outproj_kernel(x_ref, a1_ref, a2_ref, w1_ref, w2_ref, o_ref):
    for c0 in range(0, o_ref.shape[1], PROJ_CHUNK):
        cols = slice(c0, c0 + PROJ_CHUNK)
        o_ref[:, cols] = x_ref[:, cols] + _dot(a1_ref[...], w1_ref[:, cols]) + _dot(a2_ref[...], w2_ref[:, cols])


def outproj(x2, a1, a2, w1, w2, tm, tn):
    T, D = x2.shape
    k1, k2 = a1.shape[1], a2.shape[1]
    return pl.pallas_call(
        _outproj_kernel,
        grid=(T // tm, D // tn),
        in_specs=[pl.BlockSpec((tm, tn), lambda i, j: (i, j)),
                  pl.BlockSpec((tm, k1), lambda i, j: (i, 0)),
                  pl.BlockSpec((tm, k2), lambda i, j: (i, 0)),
                  pl.BlockSpec((k1, tn), lambda i, j: (0, j)),
                  pl.BlockSpec((k2, tn), lambda i, j: (0, j))],
        out_specs=pl.BlockSpec((tm, tn), lambda i, j: (i, j)),
        out_shape=jax.ShapeDtypeStruct((T, D), F32),
        compiler_params=_cparams("parallel", "arbitrary"),
        name="outproj",
    )(x2, a1, a2, w1, w2)


def _memkv_kernel(mem_ref, g_ref, w_ref, kg_ref, k_ref, v_ref):
    mw = k_ref.shape[2]
    hd = kg_ref.shape[1]
    memn = _rms(mem_ref[0], g_ref[...]).astype(BF16)
    kv = _dot(memn, w_ref[...])
    for hh in range(mw // hd):
        cols = slice(hh * hd, (hh + 1) * hd)
        k_ref[0, :, cols] = _rms(kv[:, cols], kg_ref[...]).astype(BF16)
    v_ref[0] = kv[:, mw:].astype(BF16)


def mem_kv(mem, mem_gain, w_kv, k_gain):
    nb, M, D = mem.shape
    mw = w_kv.shape[1] // 2
    hd = k_gain.shape[0]
    out = jax.ShapeDtypeStruct((nb, M, mw), BF16)
    return pl.pallas_call(
        _memkv_kernel,
        grid=(nb,),
        in_specs=[pl.BlockSpec((1, M, D), lambda b: (b, 0, 0)),
                  pl.BlockSpec((1, D), lambda b: (0, 0)),
                  pl.BlockSpec((D, 2 * mw), lambda b: (0, 0)),
                  pl.BlockSpec((1, hd), lambda b: (0, 0))],
        out_specs=[pl.BlockSpec((1, M, mw), lambda b: (b, 0, 0))] * 2,
        out_shape=[out, out],
        compiler_params=_cparams("parallel"),
        name="mem_kv",
    )(mem, mem_gain.reshape(1, D), w_kv, k_gain.reshape(1, hd))


def _memattn_kernel(q_ref, k_ref, v_ref, o_ref, *, heads):
    hd = q_ref.shape[2] // heads
    scale = hd ** -0.5
    for hh in range(heads):
        cols = slice(hh * hd, (hh + 1) * hd)
        s = _dot_nt(q_ref[0, :, cols], k_ref[0, :, cols]) * scale
        e = jnp.exp(s - jnp.max(s, axis=-1, keepdims=True))
        p = e / jnp.sum(e, axis=-1, keepdims=True)
        o_ref[0, :, cols] = _dot(p.astype(BF16), v_ref[0, :, cols]).astype(o_ref.dtype)


def mem_attention(q, k, v, tq):
    nb, S, mw = q.shape
    M = k.shape[1]
    return pl.pallas_call(
        functools.partial(_memattn_kernel, heads=MEM_HEADS),
        grid=(nb, S // tq),
        in_specs=[pl.BlockSpec((1, tq, mw), lambda b, i: (b, i, 0)),
                  pl.BlockSpec((1, M, mw), lambda b, i: (b, 0, 0)),
                  pl.BlockSpec((1, M, mw), lambda b, i: (b, 0, 0))],
        out_specs=pl.BlockSpec((1, tq, mw), lambda b, i: (b, i, 0)),
        out_shape=jax.ShapeDtypeStruct((nb, S, mw), BF16),
        compiler_params=_cparams("parallel", "parallel"),
        name="mem_attn",
    )(q, k, v)


SSD_GROUP_WIDTH = SSD_HEADS_PER_GROUP * SSD_HEAD_DIM
SSD_GROUP_CONV = SSD_GROUP_WIDTH + 2 * SSD_STATE
SSD_GROUP_COLS = SSD_GROUP_CONV + SSD_GROUP_WIDTH
HALO_ROWS = 16


def _ssd_inproj_kernel(hp_ref, h_ref, hn_ref, w_ref, cw_ref, cb_ref, o_ref, lhs_ref, ext_ref):
    i = pl.program_id(1)
    tm = h_ref.shape[1]
    lo = HALO_ROWS - SSD_CONV // 2

    @pl.when(pl.program_id(2) == 0)
    def _():
        lhs_ref[0:HALO_ROWS] = jnp.where(i > 0, hp_ref[0], jnp.zeros_like(hp_ref[0]))
        lhs_ref[HALO_ROWS:HALO_ROWS + tm] = h_ref[0]
        lhs_ref[HALO_ROWS + tm:] = jnp.where(i < pl.num_programs(1) - 1, hn_ref[0], jnp.zeros_like(hn_ref[0]))

    for c0 in range(0, SSD_GROUP_COLS, PROJ_CHUNK):
        if c0 >= SSD_GROUP_CONV:
            o_ref[0, :, c0:c0 + PROJ_CHUNK] = jax.nn.silu(_dot(h_ref[0], w_ref[:, c0:c0 + PROJ_CHUNK]))
            continue
        half = (tm + 2 * HALO_ROWS) // 2
        for r0 in (0, half):
            res = _dot(lhs_ref[r0:r0 + half, :], w_ref[:, c0:c0 + PROJ_CHUNK])
            for c in range(c0, c0 + PROJ_CHUNK, LANES):
                ext_ref[c // LANES, r0:r0 + half] = res[:, c - c0:c - c0 + LANES]
        for c in range(c0, c0 + PROJ_CHUNK, LANES):
            s = c // LANES
            if c >= SSD_GROUP_CONV:
                o_ref[0, :, c:c + LANES] = jax.nn.silu(ext_ref[s, HALO_ROWS:HALO_ROWS + tm, :])
                continue
            acc = cb_ref[:, c:c + LANES] + cw_ref[0:1, c:c + LANES] * ext_ref[s, lo:lo + tm, :]
            for k in range(1, SSD_CONV):
                acc = acc + cw_ref[k:k + 1, c:c + LANES] * ext_ref[s, lo + k:lo + k + tm, :]
            o_ref[0, :, c:c + LANES] = jax.nn.silu(acc)


def ssd_inproj(h3, w, conv_w, conv_b, tm):
    nb, S, K = h3.shape
    G, gc, cols = SSD_GROUPS, SSD_GROUP_CONV, SSD_GROUP_COLS
    assert PROJ_CHUNK % LANES == 0 and gc % LANES == 0 and cols % PROJ_CHUNK == 0 and tm % HALO_ROWS == 0
    nh = tm // HALO_ROWS
    last = S // HALO_ROWS - 1
    return pl.pallas_call(
        _ssd_inproj_kernel,
        grid=(nb, S // tm, G),
        in_specs=[pl.BlockSpec((1, HALO_ROWS, K), lambda b, i, g: (b, jnp.maximum(i * nh - 1, 0), 0)),
                  pl.BlockSpec((1, tm, K), lambda b, i, g: (b, i, 0)),
                  pl.BlockSpec((1, HALO_ROWS, K), lambda b, i, g: (b, jnp.minimum((i + 1) * nh, last), 0)),
                  pl.BlockSpec((K, cols), lambda b, i, g: (0, g)),
                  pl.BlockSpec((SSD_CONV, gc), lambda b, i, g: (0, g)),
                  pl.BlockSpec((1, gc), lambda b, i, g: (0, g))],
        out_specs=pl.BlockSpec((1, tm, cols), lambda b, i, g: (b, i, g)),
        out_shape=jax.ShapeDtypeStruct((nb, S, G * cols), F32),
        scratch_shapes=[pltpu.VMEM((tm + 2 * HALO_ROWS, K), BF16),
                        pltpu.VMEM((pl.cdiv(gc, PROJ_CHUNK) * PROJ_CHUNK // LANES, tm + 2 * HALO_ROWS, LANES), F32)],
        compiler_params=_cparams("parallel", "parallel", "arbitrary"),
        name="ssd_inproj",
    )(h3, h3, h3, w, conv_w, conv_b.reshape(1, G * gc))


LOG2E = math.log2(math.e)


def _ssd_prep_kernel(dtr_ref, bias_ref, alog_ref, tr_ref):
    Q = SSD_CHUNK
    ii = lax.broadcasted_iota(jnp.int32, (Q, Q), 0)
    jj = lax.broadcasted_iota(jnp.int32, (Q, Q), 1)
    tril = (jj <= ii).astype(F32)
    triu = (jj >= ii).astype(F32)
    lane = lax.broadcasted_iota(jnp.int32, (Q, LANES), 1)
    is_bwd = (lane % GROUP_LANES) >= GROUP_LANES // 2
    hi = lax.Precision.HIGHEST
    for i in range(tr_ref.shape[1]):
        x = dtr_ref[0, i * Q:(i + 1) * Q, :] + bias_ref[...]
        dt = jnp.maximum(x, 0.0) + jnp.log1p(jnp.exp(-jnp.abs(x)))
        a = dt * (-jnp.exp(alog_ref[...]))
        cs_fwd = jnp.dot(tril, a, precision=hi, preferred_element_type=F32)
        cs_bwd = jnp.dot(triu, a, precision=hi, preferred_element_type=F32)
        cs = jnp.where(is_bwd, cs_bwd, cs_fwd)
        last = jnp.where(is_bwd[0:1], cs[0:1], cs[Q - 1:Q])
        cs2 = cs * LOG2E
        parts = (cs2, cs2 - jnp.log2(dt), jnp.exp(last - cs) * dt, jnp.broadcast_to(jnp.exp(last), (Q, LANES)))
        for p, part in enumerate(parts):
            rows = part.T
            for g in range(SSD_GROUPS):
                tr_ref[0, i, g, p * GROUP_LANES:(p + 1) * GROUP_LANES] = rows[g * GROUP_LANES:(g + 1) * GROUP_LANES]


def ssd_prep(dtr, bias_l, alog_l):
    nb, S, _ = dtr.shape
    Q = SSD_CHUNK
    nc = S // Q
    kp = next(k for k in (8, 4, 2, 1) if nc % k == 0)
    vec_spec = pl.BlockSpec((1, LANES), lambda b, c: (0, 0))
    return pl.pallas_call(
        _ssd_prep_kernel,
        grid=(nb, nc // kp),
        in_specs=[pl.BlockSpec((1, kp * Q, LANES), lambda b, c: (b, c, 0)), vec_spec, vec_spec],
        out_specs=pl.BlockSpec((1, kp, SSD_GROUPS, 4 * GROUP_LANES, Q), lambda b, c: (b, c, 0, 0, 0)),
        out_shape=jax.ShapeDtypeStruct((nb, nc, SSD_GROUPS, 4 * GROUP_LANES, Q), F32),
        compiler_params=_cparams("parallel", "parallel"),
        name="ssd_prep",
    )(dtr, bias_l, alog_l)


def _ssd_direction(d, g, sub, xc_ref, tr_ref, st_ref):
    Q = SSD_CHUNK
    P, N, gw = SSD_HEAD_DIM, SSD_STATE, SSD_GROUP_WIDTH
    rows = slice(sub * Q, (sub + 1) * Q)
    xs = xc_ref[0, rows, 0:gw]
    bm = xc_ref[0, rows, gw:gw + N]
    cm = xc_ref[0, rows, gw + N:gw + 2 * N]
    cb = _dot_nt(cm.astype(BF16), bm.astype(BF16))
    bt = bm.T

    cs2T, rowT, wT, cdT = (tr_ref[0, sub, 0, part * GROUP_LANES:(part + 1) * GROUP_LANES, :] for part in range(4))
    ii = lax.broadcasted_iota(jnp.int32, (Q, Q), 0)
    jj = lax.broadcasted_iota(jnp.int32, (Q, Q), 1)
    causal = (jj <= ii) if d == 0 else (jj >= ii)
    lo = lax.broadcasted_iota(jnp.int32, (Q, LANES), 1) < P
    st = st_ref[d]
    ys, sts = [], []
    for p in range(SSD_HEADS_PER_GROUP // 2):
        cols = slice(p * LANES, (p + 1) * LANES)
        ms, ss = [], []
        k0 = d * (GROUP_LANES // 2) + 2 * p
        for k in (k0, k0 + 1):
            col2 = jnp.broadcast_to(cs2T[k:k + 1, :], (Q, Q)).T
            decay_dt = jnp.exp2(jnp.where(causal, col2 - rowT[k:k + 1, :], -jnp.inf))
            ms.append((cb * decay_dt).astype(BF16))
            ms.append((cm * jnp.exp2(col2)).astype(BF16))
            ss.append((bt * wT[k:k + 1, :]).astype(BF16))
        cd = jnp.where(lo[0:1], cdT[k0:k0 + 1, :], cdT[k0 + 1:k0 + 2, :])
        xp, sp = xs[:, cols], st[:, cols]
        x_lo, x_hi = jnp.where(lo, xp, 0.0).astype(BF16), jnp.where(lo, 0.0, xp).astype(BF16)
        s_lo, s_hi = jnp.where(lo, sp, 0.0).astype(BF16), jnp.where(lo, 0.0, sp).astype(BF16)
        ys.append(_dot(jnp.concatenate(ms, axis=1), jnp.concatenate([x_lo, s_lo, x_hi, s_hi], axis=0)))
        sts.append(cd * sp + _dot(jnp.concatenate(ss, axis=1), jnp.concatenate([x_lo, x_hi], axis=0)))
    st_ref[d] = jnp.concatenate(sts, axis=1)
    return jnp.concatenate(ys, axis=1)


def _ssd_scan_kernel(xcf, xcb, trf, trb, dskip_ref, ng_ref, y_ref, st_ref, ysum_ref, *, kc):
    g = pl.program_id(1)
    c = pl.program_id(2)
    nsteps = pl.num_programs(2)
    Q = SSD_CHUNK

    @pl.when(c == 0)
    def _():
        st_ref[...] = jnp.zeros_like(st_ref)
        ysum_ref[...] = jnp.zeros_like(ysum_ref)

    def finish(y_dir, row0, xc_ref, sub):
        rows = pl.ds(pl.multiple_of(row0, Q), Q)
        blk = slice(sub * Q, (sub + 1) * Q)
        tot = ysum_ref[rows, :] + y_dir + dskip_ref[...] * xc_ref[0, blk, 0:SSD_GROUP_WIDTH]
        ysum_ref[rows, :] = y_dir
        gated = tot * xc_ref[0, blk, SSD_GROUP_CONV:SSD_GROUP_COLS]
        y_ref[0, rows, :] = _rms(gated, ng_ref[...]).astype(y_ref.dtype)

    for i in range(kc):
        y_f = _ssd_direction(0, g, i, xcf, trf, st_ref)
        finish(y_f, (c * kc + i) * Q, xcf, i)
        y_b = _ssd_direction(1, g, kc - 1 - i, xcb, trb, st_ref)
        finish(y_b, ((nsteps - 1 - c) * kc + kc - 1 - i) * Q, xcb, kc - 1 - i)


def ssd_scan(xc, tr, dskip, norm_g, kc):
    nb, S, _ = xc.shape
    Q = SSD_CHUNK
    nsteps = S // (Q * kc)
    assert nsteps % 2 == 0 and Q == LANES
    G, N, gw = SSD_GROUPS, SSD_STATE, SSD_GROUP_WIDTH
    inner = G * gw

    def both(shape, f):
        return [pl.BlockSpec(shape, lambda b, g, c: f(b, g, c)),
                pl.BlockSpec(shape, lambda b, g, c: f(b, g, nsteps - 1 - c))]

    in_specs = (both((1, kc * Q, SSD_GROUP_COLS), lambda b, g, c: (b, c, g))
                + both((1, kc, 1, 4 * GROUP_LANES, Q), lambda b, g, c: (b, c, g, 0, 0))
                + [pl.BlockSpec((1, gw), lambda b, g, c: (0, g)),
                   pl.BlockSpec((1, gw), lambda b, g, c: (0, g))])
    return pl.pallas_call(
        functools.partial(_ssd_scan_kernel, kc=kc),
        grid=(nb, G, nsteps),
        in_specs=in_specs,
        out_specs=pl.BlockSpec((1, S, gw), lambda b, g, c: (b, 0, g)),
        out_shape=jax.ShapeDtypeStruct((nb, S, inner), BF16),
        scratch_shapes=[pltpu.VMEM((2, N, gw), F32), pltpu.VMEM((S, gw), F32)],
        compiler_params=_cparams("parallel", "parallel", "arbitrary"),
        name="ssd_scan",
    )(xc, xc, tr, tr, dskip.reshape(1, inner), norm_g.reshape(1, inner))


def _t5_bucket_np(rel):
    half = REL_BUCKETS // 2
    exact = half // 2
    n = np.abs(rel)
    far = exact + (np.log(np.maximum(n, 1).astype(np.float32) / np.float32(exact))
                   / np.float32(math.log(REL_MAX_DISTANCE / exact)) * np.float32(half - exact)).astype(np.int32)
    far = np.minimum(far, half - 1)
    return np.where(rel > 0, half, 0) + np.where(n < exact, n, far)


def _dil_bucket_rows(win):
    d = np.arange(win)
    rel = d - DIL_HALF
    rows = []
    for _, dilation in DIL_CONFIGS:
        rows.append(np.where(d <= 2 * DIL_HALF, _t5_bucket_np(rel * dilation), -1))
    return np.broadcast_to(np.stack(rows)[:, None, :], (len(DIL_CONFIGS), 8, win)).astype(np.int32)


def _dil_kernel(tbl_ref, bm_ref, *refs, seq, seg, tq):
    ngroups = len(DIL_CONFIGS)
    win = tq + 2 * DIL_HALF
    in_refs = refs[:7 * ngroups]
    o_ref = refs[7 * ngroups]
    bias_ref, m_ref, l_ref, a_ref, kext_ref, vext_ref = refs[7 * ngroups + 1:]
    h = pl.program_id(1)
    t = pl.program_id(2)
    scale = DIL_HEAD_DIM ** -0.5

    @pl.when(t == 0)
    def _():
        m_ref[...] = jnp.full_like(m_ref, -jnp.inf)
        l_ref[...] = jnp.zeros_like(l_ref)
        a_ref[...] = jnp.zeros_like(a_ref)
        kcol = lax.broadcasted_iota(jnp.int32, (tq, win), 1)
        for gi in range(ngroups):
            ids = bm_ref[gi]
            row = jnp.zeros(ids.shape, F32)
            for u in range(REL_BUCKETS):
                row = jnp.where(ids == u, tbl_ref[u, gi * DIL_HEADS + h], row)
            row = jnp.where(ids < 0, -jnp.inf, row)
            band = pltpu.roll(jnp.broadcast_to(row[0:1], (tq, win)), 0, 1, stride=1, stride_axis=0)
            no_left = jnp.where(kcol >= DIL_HALF, band, -jnp.inf)
            bias_ref[gi, 0] = band
            bias_ref[gi, 1] = no_left
            bias_ref[gi, 2] = jnp.where(kcol < win - DIL_HALF, band, -jnp.inf)
            bias_ref[gi, 3] = jnp.where(kcol < win - DIL_HALF, no_left, -jnp.inf)

    for gi, (_, r) in enumerate(DIL_CONFIGS):
        q_ref, kp, kc, kn, vp, vc, vn = in_refs[7 * gi:7 * gi + 7]
        sub_len = seq // r
        piece = min(seg, sub_len)
        nblk = piece // tq
        for ext, prev, cur, nxt in ((kext_ref, kp, kc, kn), (vext_ref, vp, vc, vn)):
            ext[gi, 0:DIL_HALF] = prev[0]
            ext[gi, DIL_HALF:DIL_HALF + seg] = cur[0]
            ext[gi, DIL_HALF + seg:] = nxt[0]
        for p in range(seg // piece):
            first_row = t * seg + p * piece
            m_res = first_row // sub_len
            j0 = first_row % sub_len
            for jb in range(nblk):
                row0 = p * piece + jb * tq
                q = q_ref[0, row0:row0 + tq, :]
                kw = kext_ref[gi, row0:row0 + win, :]
                vw = vext_ref[gi, row0:row0 + win, :]
                variant = 0
                if jb == 0:
                    variant = variant + (j0 == 0).astype(jnp.int32)
                if jb == nblk - 1:
                    variant = variant + 2 * (j0 + piece == sub_len).astype(jnp.int32)
                s = _dot_nt(q, kw) * scale + bias_ref[gi, variant]
                mb = jnp.max(s, axis=-1, keepdims=True)
                e = jnp.exp(s - mb)
                lb = jnp.sum(e, axis=-1, keepdims=True)
                acc = _dot(e.astype(BF16), vw)
                start = (j0 + jb * tq) * r + m_res
                rows = pl.ds(start, tq, stride=r) if r > 1 else pl.ds(start, tq)
                m_old = m_ref[rows, :]
                m_new = jnp.maximum(m_old, mb)
                alpha = jnp.exp(m_old - m_new)
                beta = jnp.exp(mb - m_new)
                l_new = alpha * l_ref[rows, :] + beta * lb
                a_new = alpha * a_ref[rows, :] + beta * acc
                m_ref[rows, :] = m_new
                l_ref[rows, :] = l_new
                a_ref[rows, :] = a_new

    @pl.when(t == pl.num_programs(2) - 1)
    def _():
        o_ref[0] = (a_ref[...] / l_ref[...]).astype(o_ref.dtype)


def dilated_attention(qkvs, rel_bias):
    nb, S, _ = qkvs[0].shape
    hd = DIL_HEAD_DIM
    seg = S // DIL_SEGS
    tq = _tile(seg, DIL_QUERY_BLOCK)
    win = tq + 2 * DIL_HALF
    assert seg % tq == 0 and tq % LANES == 0
    for w, r in DIL_CONFIGS:
        piece = min(seg, S // r)
        assert w == 2 * DIL_HALF * r and (S // r) % piece == 0 and seg % piece == 0 and piece % tq == 0
    nhb = seg // DIL_HALF
    last_hb = S // DIL_HALF - 1
    in_specs = [pl.BlockSpec(memory_space=pltpu.SMEM),
                pl.BlockSpec((len(DIL_CONFIGS), 8, win), lambda b, h, t: (0, 0, 0))]
    args = [rel_bias, jnp.asarray(_dil_bucket_rows(win))]
    for g in range(len(DIL_CONFIGS)):
        in_specs.append(pl.BlockSpec((1, seg, hd), lambda b, h, t: (b, t, h)))
        for part in (1, 2):
            off = part * DIL_HEADS
            in_specs += [
                pl.BlockSpec((1, DIL_HALF, hd), lambda b, h, t, off=off: (b, jnp.maximum(t * nhb - 1, 0), off + h)),
                pl.BlockSpec((1, seg, hd), lambda b, h, t, off=off: (b, t, off + h)),
                pl.BlockSpec((1, DIL_HALF, hd), lambda b, h, t, off=off: (b, jnp.minimum((t + 1) * nhb, last_hb), off + h)),
            ]
        args += [qkvs[g]] * 7
    kern = functools.partial(_dil_kernel, seq=S, seg=seg, tq=tq)
    return pl.pallas_call(
        kern,
        grid=(nb, DIL_HEADS, DIL_SEGS),
        in_specs=in_specs,
        out_specs=pl.BlockSpec((1, S, hd), lambda b, h, t: (b, 0, h)),
        out_shape=jax.ShapeDtypeStruct((nb, S, DIL_HEADS * hd), BF16),
        scratch_shapes=[pltpu.VMEM((len(DIL_CONFIGS), 4, tq, win), F32),
                        pltpu.VMEM((S, hd), F32), pltpu.VMEM((S, hd), F32), pltpu.VMEM((S, hd), F32),
                        pltpu.VMEM((len(DIL_CONFIGS), seg + 2 * DIL_HALF, hd), BF16),
                        pltpu.VMEM((len(DIL_CONFIGS), seg + 2 * DIL_HALF, hd), BF16)],
        compiler_params=_cparams("parallel", "parallel", "arbitrary"),
        name="dil_attn",
    )(*args)


def _ssd_lane_layout(v):
    v = v.reshape(2, SSD_GROUPS, SSD_HEADS_PER_GROUP).transpose(1, 0, 2)
    v = jnp.pad(v, ((0, 0), (0, 0), (0, GROUP_LANES // 2 - SSD_HEADS_PER_GROUP)))
    return v.reshape(1, LANES)


def _ssd_mixer(x2, h3, kmem, vmem_, w_in, conv_w, conv_b, dt_bias, a_log, d_skip, norm_g, w_out, mem_q_gain):
    nb, S, D = h3.shape
    inner = SSD_GROUPS * SSD_HEADS_PER_GROUP * SSD_HEAD_DIM
    conv_ch = inner + 2 * SSD_GROUPS * SSD_STATE
    nheads = SSD_GROUPS * SSD_HEADS_PER_GROUP
    mw = kmem.shape[2]
    tm = _tile(S, ROW_TILE)
    w_bf = w_in.astype(BF16)

    def grouped(t, z=None):
        lead = t.shape[:-1]
        parts = [t[..., :inner].reshape(*lead, SSD_GROUPS, SSD_GROUP_WIDTH),
                 t[..., inner:inner + SSD_GROUPS * SSD_STATE].reshape(*lead, SSD_GROUPS, SSD_STATE),
                 t[..., inner + SSD_GROUPS * SSD_STATE:].reshape(*lead, SSD_GROUPS, SSD_STATE)]
        if z is not None:
            parts.append(z.reshape(*lead, SSD_GROUPS, SSD_GROUP_WIDTH))
        return jnp.concatenate(parts, axis=-1).reshape(*lead, -1)

    w_p1 = grouped(w_bf[:, inner:inner + conv_ch], z=w_bf[:, :inner])
    xc = ssd_inproj(h3, w_p1, grouped(conv_w), grouped(conv_b), tm)
    w_dt = w_in[:, inner + conv_ch:inner + conv_ch + 2 * nheads]
    w_dt = w_dt.reshape(D, 2, SSD_GROUPS, SSD_HEADS_PER_GROUP).transpose(0, 2, 1, 3)
    w_dt = jnp.pad(w_dt, ((0, 0), (0, 0), (0, 0), (0, GROUP_LANES // 2 - SSD_HEADS_PER_GROUP)))
    q_off = inner + conv_ch + 2 * nheads
    w_side = jnp.concatenate([w_bf[:, q_off:], w_dt.reshape(D, LANES).astype(BF16)], axis=1)
    q_mem, dtr = side_proj(h3, w_side, mem_q_gain, tm)

    tr = ssd_prep(dtr, _ssd_lane_layout(dt_bias), _ssd_lane_layout(a_log))
    dskip = jnp.repeat(d_skip, SSD_HEAD_DIM)
    nchunks = S // SSD_CHUNK
    kc = next(k for k in (8, 4, 2, 1) if nchunks % (2 * k) == 0)
    y = ssd_scan(xc, tr, dskip, norm_g, kc)
    o_mem = mem_attention(q_mem, kmem, vmem_, tq=tm)
    w_out_bf = w_out.astype(BF16)
    return outproj(x2, y.reshape(nb * S, inner), o_mem.reshape(nb * S, mw),
                   w_out_bf[:inner], w_out_bf[inner:], tm=_tile(nb * S, ROW_TILE), tn=_tile(D, OUT_TILE))


def _dil_mixer(x2, h3, kmem, vmem_, w_in, q_gain, k_gain, w_out, rel_bias, mem_q_gain):
    nb, S, D = h3.shape
    width = DIL_HEADS * DIL_HEAD_DIM
    mw = kmem.shape[2]
    tm = _tile(S, ROW_TILE)
    w_bf = w_in.astype(BF16)
    qkvs = []
    for g, (_, r) in enumerate(DIL_CONFIGS):
        gains = jnp.stack([jnp.tile(q_gain[g], DIL_HEADS), jnp.tile(k_gain[g], DIL_HEADS),
                           jnp.ones((width,), F32)]).reshape(3, 1, width)
        out = proj(h3, w_bf, g * 3 * width, 3 * width, gains, out_dtype=BF16, tm=tm, tn=width,
                   norm_width=DIL_HEAD_DIM, norm_tiles=2, r=r)
        qkvs.append(out.reshape(nb, S, 3 * width))
    q_off = len(DIL_CONFIGS) * 3 * width
    hd = mem_q_gain.shape[0]
    qg = jnp.tile(mem_q_gain, mw // hd).reshape(1, 1, mw)
    q_mem = proj(h3, w_bf, q_off, mw, qg, out_dtype=BF16, tm=tm, tn=mw, norm_width=hd)[:, 0]
    o = dilated_attention(qkvs, rel_bias)
    o_mem = mem_attention(q_mem, kmem, vmem_, tq=tm)
    w_out_bf = w_out.astype(BF16)
    return outproj(x2, o.reshape(nb * S, width), o_mem.reshape(nb * S, mw),
                   w_out_bf[:width], w_out_bf[width:], tm=_tile(nb * S, ROW_TILE), tn=_tile(D, OUT_TILE))


def kernel(x, mem, rel_bias, ffn_norm, ffn_w_in, ffn_w_out, mix_norm, mem_norm, mem_w_kv, mem_q_gain, mem_k_gain, ssd_w_in, ssd_conv_w, ssd_conv_b, ssd_dt_bias, ssd_A_log, ssd_D, ssd_norm, ssd_w_out, dil_w_in, dil_q_gain, dil_k_gain, dil_w_out):
    nb, S, D = x.shape
    depth = ffn_norm.shape[0]
    T = nb * S
    x2 = x.reshape(T, D)
    tm_ffn, tf = _tile(T, ROW_TILE), _tile(ffn_w_out.shape[2], FF_TILE)
    ffn_w_in = ffn_w_in.astype(BF16)
    ffn_w_out = ffn_w_out.astype(BF16)
    for i in range(depth):
        x2, h2 = ffn(x2, ffn_norm[i, 0], ffn_w_in, ffn_w_out, i, 0, tm_ffn, tf, next_gain=mix_norm[i])
        h3 = h2.reshape(nb, S, D)
        kmem, vmem_ = mem_kv(mem, mem_norm[i], mem_w_kv[i].astype(BF16), mem_k_gain[i])
        j = i // 2
        if i % 2 == 0:
            x2 = _ssd_mixer(x2, h3, kmem, vmem_, ssd_w_in[j], ssd_conv_w[j], ssd_conv_b[j], ssd_dt_bias[j],
                            ssd_A_log[j], ssd_D[j], ssd_norm[j], ssd_w_out[j], mem_q_gain[i])
        else:
            x2 = _dil_mixer(x2, h3, kmem, vmem_, dil_w_in[j], dil_q_gain[j], dil_k_gain[j], dil_w_out[j],
                            rel_bias, mem_q_gain[i])
        x2 = ffn(x2, ffn_norm[i, 1], ffn_w_in, ffn_w_out, i, 1, tm_ffn, tf)
    return x2.reshape(nb, S, D)
```

```python
import functools
import math

import jax
import jax.numpy as jnp
import numpy as np
from jax import lax
from jax.experimental import pallas as pl
from jax.experimental.pallas import tpu as pltpu

F32 = jnp.float32
BF16 = jnp.bfloat16
EPS = 1e-6

MEM_HEADS = 4
SSD_HEAD_DIM = 64
SSD_GROUPS = 8
SSD_HEADS_PER_GROUP = 6
SSD_STATE = 128
SSD_CONV = 5
SSD_CHUNK = 128
DIL_CONFIGS = ((128, 1), (512, 4), (2048, 16))
DIL_HEADS = 8
DIL_HEAD_DIM = 128
DIL_HALF = 64
DIL_SEGS = 4
DIL_QUERY_BLOCK = 128
REL_BUCKETS = 32
REL_MAX_DISTANCE = 1024

LANES = 128
VMEM_LIMIT_BYTES = 62 * 1024 * 1024
GROUP_LANES = 16

ROW_TILE = 1024
FF_TILE = 512
OUT_TILE = 1024


def _tile(n, pref):
    return pref if n % pref == 0 else n


def _cparams(*sem):
    return pltpu.CompilerParams(dimension_semantics=sem, vmem_limit_bytes=VMEM_LIMIT_BYTES)


def _rms(x, gain):
    ms = jnp.mean(x * x, axis=-1, keepdims=True)
    return x * lax.rsqrt(ms + EPS) * gain


def _dot(a, b):
    return jnp.dot(a, b, preferred_element_type=F32)


def _dot_nt(a, b):
    return lax.dot_general(a, b, (((1,), (1,)), ((), ())), preferred_element_type=F32)


FFN_SUB_ROWS = 512
FFN_NORM_ROWS = 32


def _ffn_kernel(x_ref, g_ref, wg_ref, wu_ref, wo_ref, *rest, emit_norm):
    if emit_norm:
        g2_ref, o_ref, hn_ref = rest
        h_ref = hn_ref
    else:
        o_ref, h_ref = rest
    j = pl.program_id(1)

    tm = x_ref.shape[0]
    sub = min(tm, FFN_SUB_ROWS)

    @pl.when(j == 0)
    def _():
        def norm_rows(r, carry):
            rows = pl.ds(pl.multiple_of(r * FFN_NORM_ROWS, FFN_NORM_ROWS), FFN_NORM_ROWS)
            h_ref[rows, :] = _rms(x_ref[rows, :], g_ref[...]).astype(BF16)
            return carry

        lax.fori_loop(0, tm // FFN_NORM_ROWS, norm_rows, 0, unroll=4)

    def step(base_ref):
        for r0 in range(0, tm, sub):
            h = h_ref[r0:r0 + sub, :]
            gate = _dot(h, wg_ref[...])
            up = _dot(h, wu_ref[...])
            a = (jax.nn.silu(gate) * up * 0.5).astype(BF16)
            o_ref[r0:r0 + sub, :] = base_ref[r0:r0 + sub, :] + _dot(a, wo_ref[...])

    pl.when(j == 0)(lambda: step(x_ref))
    pl.when(j > 0)(lambda: step(o_ref))

    if emit_norm:
        @pl.when(j == pl.num_programs(1) - 1)
        def _():
            for r0 in range(0, tm, sub // 2):
                rows = slice(r0, r0 + sub // 2)
                hn_ref[rows, :] = _rms(o_ref[rows, :], g2_ref[...]).astype(hn_ref.dtype)


def ffn(x2, gain, w_in, w_out, layer, which, tm, tf, next_gain=None):
    T, D = x2.shape
    F = w_out.shape[2]
    nf = F // tf
    emit_norm = next_gain is not None
    row_spec = pl.BlockSpec((tm, D), lambda i, j: (i, 0))
    vec_spec = pl.BlockSpec((1, D), lambda i, j: (0, 0))
    in_specs = [row_spec, vec_spec,
                pl.BlockSpec((None, None, D, tf), lambda i, j: (layer, which, 0, j)),
                pl.BlockSpec((None, None, D, tf), lambda i, j: (layer, which, 0, j + nf)),
                pl.BlockSpec((None, None, tf, D), lambda i, j: (layer, which, j, 0))]
    args = [x2, gain.reshape(1, D), w_in, w_in, w_out]
    out_specs, out_shape = row_spec, jax.ShapeDtypeStruct((T, D), F32)
    if emit_norm:
        in_specs.append(vec_spec)
        args.append(next_gain.reshape(1, D))
        out_specs, out_shape = [row_spec, row_spec], [out_shape, jax.ShapeDtypeStruct((T, D), BF16)]
    return pl.pallas_call(
        functools.partial(_ffn_kernel, emit_norm=emit_norm),
        grid=(T // tm, nf),
        in_specs=in_specs,
        out_specs=out_specs,
        out_shape=out_shape,
        scratch_shapes=[] if emit_norm else [pltpu.VMEM((tm, D), BF16)],
        compiler_params=_cparams("parallel", "arbitrary"),
        name="ffn",
    )(*args)


PROJ_CHUNK = 256


def _proj_kernel(h_ref, w_ref, g_ref, o_ref, *scratch, norm_width, norm_tiles, r):
    tm, tn = h_ref.shape[1], w_ref.shape[1]
    rows_per = tm // r
    nw = norm_width if norm_width else PROJ_CHUNK
    assert PROJ_CHUNK % nw == 0 or nw % PROJ_CHUNK == 0
    chunk = max(PROJ_CHUNK, nw)

    def emit(normed):
        for c0 in range(0, tn, chunk):
            res = _dot(h_ref[0], w_ref[:, c0:c0 + chunk])
            if r == 1:
                if not normed:
                    o_ref[0, 0, :, c0:c0 + chunk] = res.astype(o_ref.dtype)
                    continue
                for c in range(c0, c0 + chunk, nw):
                    blk = res[:, c - c0:c - c0 + nw]
                    o_ref[0, 0, :, c:c + nw] = _rms(blk, g_ref[0, :, c:c + nw]).astype(o_ref.dtype)
                continue
            acc_ref, = scratch
            for c in range(c0, c0 + chunk, LANES):
                acc_ref[c // LANES] = res[:, c - c0:c - c0 + LANES]
            for m in range(r):
                rows = pl.ds(m, rows_per, stride=r)
                for c in range(c0, c0 + chunk, LANES):
                    blk = acc_ref[c // LANES, rows, :]
                    if normed:
                        blk = _rms(blk, g_ref[0, :, c:c + LANES])
                    o_ref[0, m, :, c:c + LANES] = blk.astype(o_ref.dtype)

    if r > 1:
        assert norm_width in (0, LANES)
    if norm_width == 0:
        emit(False)
    elif norm_tiles is None:
        emit(True)
    else:
        j = pl.program_id(2)
        pl.when(j < norm_tiles)(lambda: emit(True))
        pl.when(j >= norm_tiles)(lambda: emit(False))


def proj(h3, w, col0, ncols, gains, *, out_dtype, tm, tn, norm_width=0, norm_tiles=None, r=1):
    nb, S, K = h3.shape
    assert col0 % tn == 0 and ncols % tn == 0 and S % tm == 0 and tm % (8 * r) == 0
    j0 = col0 // tn
    nj = ncols // tn
    if gains is None:
        gains = jnp.ones((nj, 1, tn), F32)
    scratch = [] if r == 1 else [pltpu.VMEM((tn // LANES, tm, LANES), F32)]
    kern = functools.partial(_proj_kernel, norm_width=norm_width, norm_tiles=norm_tiles, r=r)
    return pl.pallas_call(
        kern,
        grid=(nb, S // tm, nj),
        in_specs=[pl.BlockSpec((1, tm, K), lambda b, i, j: (b, i, 0)),
                  pl.BlockSpec((K, tn), lambda b, i, j: (0, j0 + j)),
                  pl.BlockSpec((1, 1, tn), lambda b, i, j: (j, 0, 0))],
        out_specs=pl.BlockSpec((1, r, tm // r, tn), lambda b, i, j: (b, 0, i, j)),
        out_shape=jax.ShapeDtypeStruct((nb, r, S // r, ncols), out_dtype),
        scratch_shapes=scratch,
        compiler_params=_cparams("parallel", "parallel", "arbitrary"),
        name="proj",
    )(h3, w, gains)


def _side_proj_kernel(h_ref, w_ref, g_ref, q_ref, dt_ref, *, head_dim):
    mw = q_ref.shape[2]
    for c in range(mw // head_dim):
        cols = slice(c * head_dim, (c + 1) * head_dim)
        q_ref[0, :, cols] = _rms(_dot(h_ref[0], w_ref[:, cols]), g_ref[...]).astype(q_ref.dtype)
    dt_ref[0] = _dot(h_ref[0], w_ref[:, mw:])


def side_proj(h3, w, q_gain, tm):
    nb, S, K = h3.shape
    head_dim = q_gain.shape[0]
    n = w.shape[1]
    mw = n - LANES
    return pl.pallas_call(
        functools.partial(_side_proj_kernel, head_dim=head_dim),
        grid=(nb, S // tm),
        in_specs=[pl.BlockSpec((1, tm, K), lambda b, i: (b, i, 0)),
                  pl.BlockSpec((K, n), lambda b, i: (0, 0)),
                  pl.BlockSpec((1, head_dim), lambda b, i: (0, 0))],
        out_specs=[pl.BlockSpec((1, tm, mw), lambda b, i: (b, i, 0)),
                   pl.BlockSpec((1, tm, LANES), lambda b, i: (b, i, 0))],
        out_shape=[jax.ShapeDtypeStruct((nb, S, mw), BF16), jax.ShapeDtypeStruct((nb, S, LANES), F32)],
        compiler_params=_cparams("parallel", "parallel"),
        name="side_proj",
    )(h3, w, q_gain.reshape(1, head_dim))


def _outproj_kernel(x_ref, a1_ref, a2_ref, w1_ref, w2_ref, o_ref):
    for c0 in range(0, o_ref.shape[1], PROJ_CHUNK):
        cols = slice(c0, c0 + PROJ_CHUNK)
        o_ref[:, cols] = x_ref[:, cols] + _dot(a1_ref[...], w1_ref[:, cols]) + _dot(a2_ref[...], w2_ref[:, cols])


def outproj(x2, a1, a2, w1, w2, tm, tn):
    T, D = x2.shape
    k1, k2 = a1.shape[1], a2.shape[1]
    return pl.pallas_call(
        _outproj_kernel,
        grid=(T // tm, D // tn),
        in_specs=[pl.BlockSpec((tm, tn), lambda i, j: (i, j)),
                  pl.BlockSpec((tm, k1), lambda i, j: (i, 0)),
                  pl.BlockSpec((tm, k2), lambda i, j: (i, 0)),
                  pl.BlockSpec((k1, tn), lambda i, j: (0, j)),
                  pl.BlockSpec((k2, tn), lambda i, j: (0, j))],
        out_specs=pl.BlockSpec((tm, tn), lambda i, j: (i, j)),
        out_shape=jax.ShapeDtypeStruct((T, D), F32),
        compiler_params=_cparams("parallel", "arbitrary"),
        name="outproj",
    )(x2, a1, a2, w1, w2)


def _memkv_kernel(mem_ref, g_ref, w_ref, kg_ref, k_ref, v_ref):
    mw = k_ref.shape[2]
    hd = kg_ref.shape[1]
    memn = _rms(mem_ref[0], g_ref[...]).astype(BF16)
    kv = _dot(memn, w_ref[...])
    for hh in range(mw // hd):
        cols = slice(hh * hd, (hh + 1) * hd)
        k_ref[0, :, cols] = _rms(kv[:, cols], kg_ref[...]).astype(BF16)
    v_ref[0] = kv[:, mw:].astype(BF16)


def mem_kv(mem, mem_gain, w_kv, k_gain):
    nb, M, D = mem.shape
    mw = w_kv.shape[1] // 2
    hd = k_gain.shape[0]
    out = jax.ShapeDtypeStruct((nb, M, mw), BF16)
    return pl.pallas_call(
        _memkv_kernel,
        grid=(nb,),
        in_specs=[pl.BlockSpec((1, M, D), lambda b: (b, 0, 0)),
                  pl.BlockSpec((1, D), lambda b: (0, 0)),
                  pl.BlockSpec((D, 2 * mw), lambda b: (0, 0)),
                  pl.BlockSpec((1, hd), lambda b: (0, 0))],
        out_specs=[pl.BlockSpec((1, M, mw), lambda b: (b, 0, 0))] * 2,
        out_shape=[out, out],
        compiler_params=_cparams("parallel"),
        name="mem_kv",
    )(mem, mem_gain.reshape(1, D), w_kv, k_gain.reshape(1, hd))


def _memattn_kernel(q_ref, k_ref, v_ref, o_ref, *, heads):
    hd = q_ref.shape[2] // heads
    scale = hd ** -0.5
    for hh in range(heads):
        cols = slice(hh * hd, (hh + 1) * hd)
        s = _dot_nt(q_ref[0, :, cols], k_ref[0, :, cols]) * scale
        e = jnp.exp(s - jnp.max(s, axis=-1, keepdims=True))
        p = e / jnp.sum(e, axis=-1, keepdims=True)
        o_ref[0, :, cols] = _dot(p.astype(BF16), v_ref[0, :, cols]).astype(o_ref.dtype)


def mem_attention(q, k, v, tq):
    nb, S, mw = q.shape
    M = k.shape[1]
    return pl.pallas_call(
        functools.partial(_memattn_kernel, heads=MEM_HEADS),
        grid=(nb, S // tq),
        in_specs=[pl.BlockSpec((1, tq, mw), lambda b, i: (b, i, 0)),
                  pl.BlockSpec((1, M, mw), lambda b, i: (b, 0, 0)),
                  pl.BlockSpec((1, M, mw), lambda b, i: (b, 0, 0))],
        out_specs=pl.BlockSpec((1, tq, mw), lambda b, i: (b, i, 0)),
        out_shape=jax.ShapeDtypeStruct((nb, S, mw), BF16),
        compiler_params=_cparams("parallel", "parallel"),
        name="mem_attn",
    )(q, k, v)


SSD_GROUP_WIDTH = SSD_HEADS_PER_GROUP * SSD_HEAD_DIM
SSD_GROUP_CONV = SSD_GROUP_WIDTH + 2 * SSD_STATE
SSD_GROUP_COLS = SSD_GROUP_CONV + SSD_GROUP_WIDTH
HALO_ROWS = 16


def _ssd_inproj_kernel(hp_ref, h_ref, hn_ref, w_ref, cw_ref, cb_ref, o_ref, lhs_ref, ext_ref):
    i = pl.program_id(1)
    tm = h_ref.shape[1]
    lo = HALO_ROWS - SSD_CONV // 2

    @pl.when(pl.program_id(2) == 0)
    def _():
        lhs_ref[0:HALO_ROWS] = jnp.where(i > 0, hp_ref[0], jnp.zeros_like(hp_ref[0]))
        lhs_ref[HALO_ROWS:HALO_ROWS + tm] = h_ref[0]
        lhs_ref[HALO_ROWS + tm:] = jnp.where(i < pl.num_programs(1) - 1, hn_ref[0], jnp.zeros_like(hn_ref[0]))

    for c0 in range(0, SSD_GROUP_COLS, PROJ_CHUNK):
        if c0 >= SSD_GROUP_CONV:
            o_ref[0, :, c0:c0 + PROJ_CHUNK] = jax.nn.silu(_dot(h_ref[0], w_ref[:, c0:c0 + PROJ_CHUNK]))
            continue
        half = (tm + 2 * HALO_ROWS) // 2
        for r0 in (0, half):
            res = _dot(lhs_ref[r0:r0 + half, :], w_ref[:, c0:c0 + PROJ_CHUNK])
            for c in range(c0, c0 + PROJ_CHUNK, LANES):
                ext_ref[c // LANES, r0:r0 + half] = res[:, c - c0:c - c0 + LANES]
        for c in range(c0, c0 + PROJ_CHUNK, LANES):
            s = c // LANES
            if c >= SSD_GROUP_CONV:
                o_ref[0, :, c:c + LANES] = jax.nn.silu(ext_ref[s, HALO_ROWS:HALO_ROWS + tm, :])
                continue
            acc = cb_ref[:, c:c + LANES] + cw_ref[0:1, c:c + LANES] * ext_ref[s, lo:lo + tm, :]
            for k in range(1, SSD_CONV):
                acc = acc + cw_ref[k:k + 1, c:c + LANES] * ext_ref[s, lo + k:lo + k + tm, :]
            o_ref[0, :, c:c + LANES] = jax.nn.silu(acc)


def ssd_inproj(h3, w, conv_w, conv_b, tm):
    nb, S, K = h3.shape
    G, gc, cols = SSD_GROUPS, SSD_GROUP_CONV, SSD_GROUP_COLS
    assert PROJ_CHUNK % LANES == 0 and gc % LANES == 0 and cols % PROJ_CHUNK == 0 and tm % HALO_ROWS == 0
    nh = tm // HALO_ROWS
    last = S // HALO_ROWS - 1
    return pl.pallas_call(
        _ssd_inproj_kernel,
        grid=(nb, S // tm, G),
        in_specs=[pl.BlockSpec((1, HALO_ROWS, K), lambda b, i, g: (b, jnp.maximum(i * nh - 1, 0), 0)),
                  pl.BlockSpec((1, tm, K), lambda b, i, g: (b, i, 0)),
                  pl.BlockSpec((1, HALO_ROWS, K), lambda b, i, g: (b, jnp.minimum((i + 1) * nh, last), 0)),
                  pl.BlockSpec((K, cols), lambda b, i, g: (0, g)),
                  pl.BlockSpec((SSD_CONV, gc), lambda b, i, g: (0, g)),
                  pl.BlockSpec((1, gc), lambda b, i, g: (0, g))],
        out_specs=pl.BlockSpec((1, tm, cols), lambda b, i, g: (b, i, g)),
        out_shape=jax.ShapeDtypeStruct((nb, S, G * cols), F32),
        scratch_shapes=[pltpu.VMEM((tm + 2 * HALO_ROWS, K), BF16),
                        pltpu.VMEM((pl.cdiv(gc, PROJ_CHUNK) * PROJ_CHUNK // LANES, tm + 2 * HALO_ROWS, LANES), F32)],
        compiler_params=_cparams("parallel", "parallel", "arbitrary"),
        name="ssd_inproj",
    )(h3, h3, h3, w, conv_w, conv_b.reshape(1, G * gc))


LOG2E = math.log2(math.e)


def _ssd_prep_kernel(dtr_ref, bias_ref, alog_ref, tr_ref):
    Q = SSD_CHUNK
    ii = lax.broadcasted_iota(jnp.int32, (Q, Q), 0)
    jj = lax.broadcasted_iota(jnp.int32, (Q, Q), 1)
    tril = (jj <= ii).astype(F32)
    triu = (jj >= ii).astype(F32)
    lane = lax.broadcasted_iota(jnp.int32, (Q, LANES), 1)
    is_bwd = (lane % GROUP_LANES) >= GROUP_LANES // 2
    hi = lax.Precision.HIGHEST
    for i in range(tr_ref.shape[1]):
        x = dtr_ref[0, i * Q:(i + 1) * Q, :] + bias_ref[...]
        dt = jnp.maximum(x, 0.0) + jnp.log1p(jnp.exp(-jnp.abs(x)))
        a = dt * (-jnp.exp(alog_ref[...]))
        cs_fwd = jnp.dot(tril, a, precision=hi, preferred_element_type=F32)
        cs_bwd = jnp.dot(triu, a, precision=hi, preferred_element_type=F32)
        cs = jnp.where(is_bwd, cs_bwd, cs_fwd)
        last = jnp.where(is_bwd[0:1], cs[0:1], cs[Q - 1:Q])
        cs2 = cs * LOG2E
        parts = (cs2, cs2 - jnp.log2(dt), jnp.exp(last - cs) * dt, jnp.broadcast_to(jnp.exp(last), (Q, LANES)))
        for p, part in enumerate(parts):
            rows = part.T
            for g in range(SSD_GROUPS):
                tr_ref[0, i, g, p * GROUP_LANES:(p + 1) * GROUP_LANES] = rows[g * GROUP_LANES:(g + 1) * GROUP_LANES]


def ssd_prep(dtr, bias_l, alog_l):
    nb, S, _ = dtr.shape
    Q = SSD_CHUNK
    nc = S // Q
    kp = next(k for k in (8, 4, 2, 1) if nc % k == 0)
    vec_spec = pl.BlockSpec((1, LANES), lambda b, c: (0, 0))
    return pl.pallas_call(
        _ssd_prep_kernel,
        grid=(nb, nc // kp),
        in_specs=[pl.BlockSpec((1, kp * Q, LANES), lambda b, c: (b, c, 0)), vec_spec, vec_spec],
        out_specs=pl.BlockSpec((1, kp, SSD_GROUPS, 4 * GROUP_LANES, Q), lambda b, c: (b, c, 0, 0, 0)),
        out_shape=jax.ShapeDtypeStruct((nb, nc, SSD_GROUPS, 4 * GROUP_LANES, Q), F32),
        compiler_params=_cparams("parallel", "parallel"),
        name="ssd_prep",
    )(dtr, bias_l, alog_l)


def _ssd_direction(d, g, sub, xc_ref, tr_ref, st_ref):
    Q = SSD_CHUNK
    P, N, gw = SSD_HEAD_DIM, SSD_STATE, SSD_GROUP_WIDTH
    rows = slice(sub * Q, (sub + 1) * Q)
    xs = xc_ref[0, rows, 0:gw]
    bm = xc_ref[0, rows, gw:gw + N]
    cm = xc_ref[0, rows, gw + N:gw + 2 * N]
    cb = _dot_nt(cm.astype(BF16), bm.astype(BF16))
    bt = bm.T

    cs2T, rowT, wT, cdT = (tr_ref[0, sub, 0, part * GROUP_LANES:(part + 1) * GROUP_LANES, :] for part in range(4))
    ii = lax.broadcasted_iota(jnp.int32, (Q, Q), 0)
    jj = lax.broadcasted_iota(jnp.int32, (Q, Q), 1)
    causal = (jj <= ii) if d == 0 else (jj >= ii)
    lo = lax.broadcasted_iota(jnp.int32, (Q, LANES), 1) < P
    st = st_ref[d]
    ys, sts = [], []
    for p in range(SSD_HEADS_PER_GROUP // 2):
        cols = slice(p * LANES, (p + 1) * LANES)
        ms, ss = [], []
        k0 = d * (GROUP_LANES // 2) + 2 * p
        for k in (k0, k0 + 1):
            col2 = jnp.broadcast_to(cs2T[k:k + 1, :], (Q, Q)).T
            decay_dt = jnp.exp2(jnp.where(causal, col2 - rowT[k:k + 1, :], -jnp.inf))
            ms.append((cb * decay_dt).astype(BF16))
            ms.append((cm * jnp.exp2(col2)).astype(BF16))
            ss.append((bt * wT[k:k + 1, :]).astype(BF16))
        cd = jnp.where(lo[0:1], cdT[k0:k0 + 1, :], cdT[k0 + 1:k0 + 2, :])
        xp, sp = xs[:, cols], st[:, cols]
        x_lo, x_hi = jnp.where(lo, xp, 0.0).astype(BF16), jnp.where(lo, 0.0, xp).astype(BF16)
        s_lo, s_hi = jnp.where(lo, sp, 0.0).astype(BF16), jnp.where(lo, 0.0, sp).astype(BF16)
        ys.append(_dot(jnp.concatenate(ms, axis=1), jnp.concatenate([x_lo, s_lo, x_hi, s_hi], axis=0)))
        sts.append(cd * sp + _dot(jnp.concatenate(ss, axis=1), jnp.concatenate([x_lo, x_hi], axis=0)))
    st_ref[d] = jnp.concatenate(sts, axis=1)
    return jnp.concatenate(ys, axis=1)


def _ssd_scan_kernel(xcf, xcb, trf, trb, dskip_ref, ng_ref, y_ref, st_ref, ysum_ref, *, kc):
    g = pl.program_id(1)
    c = pl.program_id(2)
    nsteps = pl.num_programs(2)
    Q = SSD_CHUNK

    @pl.when(c == 0)
    def _():
        st_ref[...] = jnp.zeros_like(st_ref)
        ysum_ref[...] = jnp.zeros_like(ysum_ref)

    def finish(y_dir, row0, xc_ref, sub):
        rows = pl.ds(pl.multiple_of(row0, Q), Q)
        blk = slice(sub * Q, (sub + 1) * Q)
        tot = ysum_ref[rows, :] + y_dir + dskip_ref[...] * xc_ref[0, blk, 0:SSD_GROUP_WIDTH]
        ysum_ref[rows, :] = y_dir
        gated = tot * xc_ref[0, blk, SSD_GROUP_CONV:SSD_GROUP_COLS]
        y_ref[0, rows, :] = _rms(gated, ng_ref[...]).astype(y_ref.dtype)

    for i in range(kc):
        y_f = _ssd_direction(0, g, i, xcf, trf, st_ref)
        finish(y_f, (c * kc + i) * Q, xcf, i)
        y_b = _ssd_direction(1, g, kc - 1 - i, xcb, trb, st_ref)
        finish(y_b, ((nsteps - 1 - c) * kc + kc - 1 - i) * Q, xcb, kc - 1 - i)


def ssd_scan(xc, tr, dskip, norm_g, kc):
    nb, S, _ = xc.shape
    Q = SSD_CHUNK
    nsteps = S // (Q * kc)
    assert nsteps % 2 == 0 and Q == LANES
    G, N, gw = SSD_GROUPS, SSD_STATE, SSD_GROUP_WIDTH
    inner = G * gw

    def both(shape, f):
        return [pl.BlockSpec(shape, lambda b, g, c: f(b, g, c)),
                pl.BlockSpec(shape, lambda b, g, c: f(b, g, nsteps - 1 - c))]

    in_specs = (both((1, kc * Q, SSD_GROUP_COLS), lambda b, g, c: (b, c, g))
                + both((1, kc, 1, 4 * GROUP_LANES, Q), lambda b, g, c: (b, c, g, 0, 0))
                + [pl.BlockSpec((1, gw), lambda b, g, c: (0, g)),
                   pl.BlockSpec((1, gw), lambda b, g, c: (0, g))])
    return pl.pallas_call(
        functools.partial(_ssd_scan_kernel, kc=kc),
        grid=(nb, G, nsteps),
        in_specs=in_specs,
        out_specs=pl.BlockSpec((1, S, gw), lambda b, g, c: (b, 0, g)),
        out_shape=jax.ShapeDtypeStruct((nb, S, inner), BF16),
        scratch_shapes=[pltpu.VMEM((2, N, gw), F32), pltpu.VMEM((S, gw), F32)],
        compiler_params=_cparams("parallel", "parallel", "arbitrary"),
        name="ssd_scan",
    )(xc, xc, tr, tr, dskip.reshape(1, inner), norm_g.reshape(1, inner))


def _t5_bucket_np(rel):
    half = REL_BUCKETS // 2
    exact = half // 2
    n = np.abs(rel)
    far = exact + (np.log(np.maximum(n, 1).astype(np.float32) / np.float32(exact))
                   / np.float32(math.log(REL_MAX_DISTANCE / exact)) * np.float32(half - exact)).astype(np.int32)
    far = np.minimum(far, half - 1)
    return np.where(rel > 0, half, 0) + np.where(n < exact, n, far)


def _dil_bucket_rows(win):
    d = np.arange(win)
    rel = d - DIL_HALF
    rows = []
    for _, dilation in DIL_CONFIGS:
        rows.append(np.where(d <= 2 * DIL_HALF, _t5_bucket_np(rel * dilation), -1))
    return np.broadcast_to(np.stack(rows)[:, None, :], (len(DIL_CONFIGS), 8, win)).astype(np.int32)


def _dil_kernel(tbl_ref, bm_ref, *refs, seq, seg, tq):
    ngroups = len(DIL_CONFIGS)
    win = tq + 2 * DIL_HALF
    in_refs = refs[:7 * ngroups]
    o_ref = refs[7 * ngroups]
    bias_ref, m_ref, l_ref, a_ref, kext_ref, vext_ref = refs[7 * ngroups + 1:]
    h = pl.program_id(1)
    t = pl.program_id(2)
    scale = DIL_HEAD_DIM ** -0.5

    @pl.when(t == 0)
    def _():
        m_ref[...] = jnp.full_like(m_ref, -jnp.inf)
        l_ref[...] = jnp.zeros_like(l_ref)
        a_ref[...] = jnp.zeros_like(a_ref)
        kcol = lax.broadcasted_iota(jnp.int32, (tq, win), 1)
        for gi in range(ngroups):
            ids = bm_ref[gi]
            row = jnp.zeros(ids.shape, F32)
            for u in range(REL_BUCKETS):
                row = jnp.where(ids == u, tbl_ref[u, gi * DIL_HEADS + h], row)
            row = jnp.where(ids < 0, -jnp.inf, row)
            band = pltpu.roll(jnp.broadcast_to(row[0:1], (tq, win)), 0, 1, stride=1, stride_axis=0)
            no_left = jnp.where(kcol >= DIL_HALF, band, -jnp.inf)
            bias_ref[gi, 0] = band
            bias_ref[gi, 1] = no_left
            bias_ref[gi, 2] = jnp.where(kcol < win - DIL_HALF, band, -jnp.inf)
            bias_ref[gi, 3] = jnp.where(kcol < win - DIL_HALF, no_left, -jnp.inf)

    for gi, (_, r) in enumerate(DIL_CONFIGS):
        q_ref, kp, kc, kn, vp, vc, vn = in_refs[7 * gi:7 * gi + 7]
        sub_len = seq // r
        piece = min(seg, sub_len)
        nblk = piece // tq
        for ext, prev, cur, nxt in ((kext_ref, kp, kc, kn), (vext_ref, vp, vc, vn)):
            ext[gi, 0:DIL_HALF] = prev[0]
            ext[gi, DIL_HALF:DIL_HALF + seg] = cur[0]
            ext[gi, DIL_HALF + seg:] = nxt[0]
        for p in range(seg // piece):
            first_row = t * seg + p * piece
            m_res = first_row // sub_len
            j0 = first_row % sub_len
            for jb in range(nblk):
                row0 = p * piece + jb * tq
                q = q_ref[0, row0:row0 + tq, :]
                kw = kext_ref[gi, row0:row0 + win, :]
                vw = vext_ref[gi, row0:row0 + win, :]
                variant = 0
                if jb == 0:
                    variant = variant + (j0 == 0).astype(jnp.int32)
                if jb == nblk - 1:
                    variant = variant + 2 * (j0 + piece == sub_len).astype(jnp.int32)
                s = _dot_nt(q, kw) * scale + bias_ref[gi, variant]
                mb = jnp.max(s, axis=-1, keepdims=True)
                e = jnp.exp(s - mb)
                lb = jnp.sum(e, axis=-1, keepdims=True)
                acc = _dot(e.astype(BF16), vw)
                start = (j0 + jb * tq) * r + m_res
                rows = pl.ds(start, tq, stride=r) if r > 1 else pl.ds(start, tq)
                m_old = m_ref[rows, :]
                m_new = jnp.maximum(m_old, mb)
                alpha = jnp.exp(m_old - m_new)
                beta = jnp.exp(mb - m_new)
                l_new = alpha * l_ref[rows, :] + beta * lb
                a_new = alpha * a_ref[rows, :] + beta * acc
                m_ref[rows, :] = m_new
                l_ref[rows, :] = l_new
                a_ref[rows, :] = a_new

    @pl.when(t == pl.num_programs(2) - 1)
    def _():
        o_ref[0] = (a_ref[...] / l_ref[...]).astype(o_ref.dtype)


def dilated_attention(qkvs, rel_bias):
    nb, S, _ = qkvs[0].shape
    hd = DIL_HEAD_DIM
    seg = S // DIL_SEGS
    tq = _tile(seg, DIL_QUERY_BLOCK)
    win = tq + 2 * DIL_HALF
    assert seg % tq == 0 and tq % LANES == 0
    for w, r in DIL_CONFIGS:
        piece = min(seg, S // r)
        assert w == 2 * DIL_HALF * r and (S // r) % piece == 0 and seg % piece == 0 and piece % tq == 0
    nhb = seg // DIL_HALF
    last_hb = S // DIL_HALF - 1
    in_specs = [pl.BlockSpec(memory_space=pltpu.SMEM),
                pl.BlockSpec((len(DIL_CONFIGS), 8, win), lambda b, h, t: (0, 0, 0))]
    args = [rel_bias, jnp.asarray(_dil_bucket_rows(win))]
    for g in range(len(DIL_CONFIGS)):
        in_specs.append(pl.BlockSpec((1, seg, hd), lambda b, h, t: (b, t, h)))
        for part in (1, 2):
            off = part * DIL_HEADS
            in_specs += [
                pl.BlockSpec((1, DIL_HALF, hd), lambda b, h, t, off=off: (b, jnp.maximum(t * nhb - 1, 0), off + h)),
                pl.BlockSpec((1, seg, hd), lambda b, h, t, off=off: (b, t, off + h)),
                pl.BlockSpec((1, DIL_HALF, hd), lambda b, h, t, off=off: (b, jnp.minimum((t + 1) * nhb, last_hb), off + h)),
            ]
        args += [qkvs[g]] * 7
    kern = functools.partial(_dil_kernel, seq=S, seg=seg, tq=tq)
    return pl.pallas_call(
        kern,
        grid=(nb, DIL_HEADS, DIL_SEGS),
        in_specs=in_specs,
        out_specs=pl.BlockSpec((1, S, hd), lambda b, h, t: (b, 0, h)),
        out_shape=jax.ShapeDtypeStruct((nb, S, DIL_HEADS * hd), BF16),
        scratch_shapes=[pltpu.VMEM((len(DIL_CONFIGS), 4, tq, win), F32),
                        pltpu.VMEM((S, hd), F32), pltpu.VMEM((S, hd), F32), pltpu.VMEM((S, hd), F32),
                        pltpu.VMEM((len(DIL_CONFIGS), seg + 2 * DIL_HALF, hd), BF16),
                        pltpu.VMEM((len(DIL_CONFIGS), seg + 2 * DIL_HALF, hd), BF16)],
        compiler_params=_cparams("parallel", "parallel", "arbitrary"),
        name="dil_attn",
    )(*args)


def _ssd_lane_layout(v):
    v = v.reshape(2, SSD_GROUPS, SSD_HEADS_PER_GROUP).transpose(1, 0, 2)
    v = jnp.pad(v, ((0, 0), (0, 0), (0, GROUP_LANES // 2 - SSD_HEADS_PER_GROUP)))
    return v.reshape(1, LANES)


def _ssd_mixer(x2, h3, kmem, vmem_, w_in, conv_w, conv_b, dt_bias, a_log, d_skip, norm_g, w_out, mem_q_gain):
    nb, S, D = h3.shape
    inner = SSD_GROUPS * SSD_HEADS_PER_GROUP * SSD_HEAD_DIM
    conv_ch = inner + 2 * SSD_GROUPS * SSD_STATE
    nheads = SSD_GROUPS * SSD_HEADS_PER_GROUP
    mw = kmem.shape[2]
    tm = _tile(S, ROW_TILE)
    w_bf = w_in.astype(BF16)

    def grouped(t, z=None):
        lead = t.shape[:-1]
        parts = [t[..., :inner].reshape(*lead, SSD_GROUPS, SSD_GROUP_WIDTH),
                 t[..., inner:inner + SSD_GROUPS * SSD_STATE].reshape(*lead, SSD_GROUPS, SSD_STATE),
                 t[..., inner + SSD_GROUPS * SSD_STATE:].reshape(*lead, SSD_GROUPS, SSD_STATE)]
        if z is not None:
            parts.append(z.reshape(*lead, SSD_GROUPS, SSD_GROUP_WIDTH))
        return jnp.concatenate(parts, axis=-1).reshape(*lead, -1)

    w_p1 = grouped(w_bf[:, inner:inner + conv_ch], z=w_bf[:, :inner])
    xc = ssd_inproj(h3, w_p1, grouped(conv_w), grouped(conv_b), tm)
    w_dt = w_in[:, inner + conv_ch:inner + conv_ch + 2 * nheads]
    w_dt = w_dt.reshape(D, 2, SSD_GROUPS, SSD_HEADS_PER_GROUP).transpose(0, 2, 1, 3)
    w_dt = jnp.pad(w_dt, ((0, 0), (0, 0), (0, 0), (0, GROUP_LANES // 2 - SSD_HEADS_PER_GROUP)))
    q_off = inner + conv_ch + 2 * nheads
    w_side = jnp.concatenate([w_bf[:, q_off:], w_dt.reshape(D, LANES).astype(BF16)], axis=1)
    q_mem, dtr = side_proj(h3, w_side, mem_q_gain, tm)

    tr = ssd_prep(dtr, _ssd_lane_layout(dt_bias), _ssd_lane_layout(a_log))
    dskip = jnp.repeat(d_skip, SSD_HEAD_DIM)
    nchunks = S // SSD_CHUNK
    kc = next(k for k in (8, 4, 2, 1) if nchunks % (2 * k) == 0)
    y = ssd_scan(xc, tr, dskip, norm_g, kc)
    o_mem = mem_attention(q_mem, kmem, vmem_, tq=tm)
    w_out_bf = w_out.astype(BF16)
    return outproj(x2, y.reshape(nb * S, inner), o_mem.reshape(nb * S, mw),
                   w_out_bf[:inner], w_out_bf[inner:], tm=_tile(nb * S, ROW_TILE), tn=_tile(D, OUT_TILE))


def _dil_mixer(x2, h3, kmem, vmem_, w_in, q_gain, k_gain, w_out, rel_bias, mem_q_gain):
    nb, S, D = h3.shape
    width = DIL_HEADS * DIL_HEAD_DIM
    mw = kmem.shape[2]
    tm = _tile(S, ROW_TILE)
    w_bf = w_in.astype(BF16)
    qkvs = []
    for g, (_, r) in enumerate(DIL_CONFIGS):
        gains = jnp.stack([jnp.tile(q_gain[g], DIL_HEADS), jnp.tile(k_gain[g], DIL_HEADS),
                           jnp.ones((width,), F32)]).reshape(3, 1, width)
        out = proj(h3, w_bf, g * 3 * width, 3 * width, gains, out_dtype=BF16, tm=tm, tn=width,
                   norm_width=DIL_HEAD_DIM, norm_tiles=2, r=r)
        qkvs.append(out.reshape(nb, S, 3 * width))
    q_off = len(DIL_CONFIGS) * 3 * width
    hd = mem_q_gain.shape[0]
    qg = jnp.tile(mem_q_gain, mw // hd).reshape(1, 1, mw)
    q_mem = proj(h3, w_bf, q_off, mw, qg, out_dtype=BF16, tm=tm, tn=mw, norm_width=hd)[:, 0]
    o = dilated_attention(qkvs, rel_bias)
    o_mem = mem_attention(q_mem, kmem, vmem_, tq=tm)
    w_out_bf = w_out.astype(BF16)
    return outproj(x2, o.reshape(nb * S, width), o_mem.reshape(nb * S, mw),
                   w_out_bf[:width], w_out_bf[width:], tm=_tile(nb * S, ROW_TILE), tn=_tile(D, OUT_TILE))


def kernel(x, mem, rel_bias, ffn_norm, ffn_w_in, ffn_w_out, mix_norm, mem_norm, mem_w_kv, mem_q_gain, mem_k_gain, ssd_w_in, ssd_conv_w, ssd_conv_b, ssd_dt_bias, ssd_A_log, ssd_D, ssd_norm, ssd_w_out, dil_w_in, dil_q_gain, dil_k_gain, dil_w_out):
    nb, S, D = x.shape
    depth = ffn_norm.shape[0]
    T = nb * S
    x2 = x.reshape(T, D)
    tm_ffn, tf = _tile(T, ROW_TILE), _tile(ffn_w_out.shape[2], FF_TILE)
    ffn_w_in = ffn_w_in.astype(BF16)
    ffn_w_out = ffn_w_out.astype(BF16)
    for i in range(depth):
        x2, h2 = ffn(x2, ffn_norm[i, 0], ffn_w_in, ffn_w_out, i, 0, tm_ffn, tf, next_gain=mix_norm[i])
        h3 = h2.reshape(nb, S, D)
        kmem, vmem_ = mem_kv(mem, mem_norm[i], mem_w_kv[i].astype(BF16), mem_k_gain[i])
        j = i // 2
        if i % 2 == 0:
            x2 = _ssd_mixer(x2, h3, kmem, vmem_, ssd_w_in[j], ssd_conv_w[j], ssd_conv_b[j], ssd_dt_bias[j],
                            ssd_A_log[j], ssd_D[j], ssd_norm[j], ssd_w_out[j], mem_q_gain[i])
        else:
            x2 = _dil_mixer(x2, h3, kmem, vmem_, dil_w_in[j], dil_q_gain[j], dil_k_gain[j], dil_w_out[j],
                            rel_bias, mem_q_gain[i])
        x2 = ffn(x2, ffn_norm[i, 1], ffn_w_in, ffn_w_out, i, 1, tm_ffn, tf)
    return x2.reshape(nb, S, D)
```

```python
import functools
import math

import jax
import jax.numpy as jnp
import numpy as np
from jax import lax
from jax.experimental import pallas as pl
from jax.experimental.pallas import tpu as pltpu

F32 = jnp.float32
BF16 = jnp.bfloat16
EPS = 1e-6

MEM_HEADS = 4
SSD_HEAD_DIM = 64
SSD_GROUPS = 8
SSD_HEADS_PER_GROUP = 6
SSD_STATE = 128
SSD_CONV = 5
SSD_CHUNK = 128
DIL_CONFIGS = ((128, 1), (512, 4), (2048, 16))
DIL_HEADS = 8
DIL_HEAD_DIM = 128
DIL_HALF = 64
DIL_SEGS = 4
DIL_QUERY_BLOCK = 128
REL_BUCKETS = 32
REL_MAX_DISTANCE = 1024

LANES = 128
VMEM_LIMIT_BYTES = 62 * 1024 * 1024
GROUP_LANES = 16

ROW_TILE = 1024
FF_TILE = 512
OUT_TILE = 1024


def _tile(n, pref):
    return pref if n % pref == 0 else n


def _cparams(*sem):
    return pltpu.CompilerParams(dimension_semantics=sem, vmem_limit_bytes=VMEM_LIMIT_BYTES)


def _rms(x, gain):
    ms = jnp.mean(x * x, axis=-1, keepdims=True)
    return x * lax.rsqrt(ms + EPS) * gain


def _dot(a, b):
    return jnp.dot(a, b, preferred_element_type=F32)


def _dot_nt(a, b):
    return lax.dot_general(a, b, (((1,), (1,)), ((), ())), preferred_element_type=F32)


FFN_SUB_ROWS = 512
FFN_NORM_ROWS = 32


def _ffn_kernel(x_ref, g_ref, wg_ref, wu_ref, wo_ref, *rest, emit_norm):
    if emit_norm:
        g2_ref, o_ref, hn_ref = rest
        h_ref = hn_ref
    else:
        o_ref, h_ref = rest
    j = pl.program_id(1)

    tm = x_ref.shape[0]
    sub = min(tm, FFN_SUB_ROWS)

    @pl.when(j == 0)
    def _():
        def norm_rows(r, carry):
            rows = pl.ds(pl.multiple_of(r * FFN_NORM_ROWS, FFN_NORM_ROWS), FFN_NORM_ROWS)
            h_ref[rows, :] = _rms(x_ref[rows, :], g_ref[...]).astype(BF16)
            return carry

        lax.fori_loop(0, tm // FFN_NORM_ROWS, norm_rows, 0, unroll=4)

    def step(base_ref):
        for r0 in range(0, tm, sub):
            h = h_ref[r0:r0 + sub, :]
            gate = _dot(h, wg_ref[...])
            up = _dot(h, wu_ref[...])
            a = (jax.nn.silu(gate) * up * 0.5).astype(BF16)
            o_ref[r0:r0 + sub, :] = base_ref[r0:r0 + sub, :] + _dot(a, wo_ref[...])

    pl.when(j == 0)(lambda: step(x_ref))
    pl.when(j > 0)(lambda: step(o_ref))

    if emit_norm:
        @pl.when(j == pl.num_programs(1) - 1)
        def _():
            for r0 in range(0, tm, sub // 2):
                rows = slice(r0, r0 + sub // 2)
                hn_ref[rows, :] = _rms(o_ref[rows, :], g2_ref[...]).astype(hn_ref.dtype)


def ffn(x2, gain, w_in, w_out, layer, which, tm, tf, next_gain=None):
    T, D = x2.shape
    F = w_out.shape[2]
    nf = F // tf
    emit_norm = next_gain is not None
    row_spec = pl.BlockSpec((tm, D), lambda i, j: (i, 0))
    vec_spec = pl.BlockSpec((1, D), lambda i, j: (0, 0))
    in_specs = [row_spec, vec_spec,
                pl.BlockSpec((None, None, D, tf), lambda i, j: (layer, which, 0, j)),
                pl.BlockSpec((None, None, D, tf), lambda i, j: (layer, which, 0, j + nf)),
                pl.BlockSpec((None, None, tf, D), lambda i, j: (layer, which, j, 0))]
    args = [x2, gain.reshape(1, D), w_in, w_in, w_out]
    out_specs, out_shape = row_spec, jax.ShapeDtypeStruct((T, D), F32)
    if emit_norm:
        in_specs.append(vec_spec)
        args.append(next_gain.reshape(1, D))
        out_specs, out_shape = [row_spec, row_spec], [out_shape, jax.ShapeDtypeStruct((T, D), BF16)]
    return pl.pallas_call(
        functools.partial(_ffn_kernel, emit_norm=emit_norm),
        grid=(T // tm, nf),
        in_specs=in_specs,
        out_specs=out_specs,
        out_shape=out_shape,
        scratch_shapes=[] if emit_norm else [pltpu.VMEM((tm, D), BF16)],
        compiler_params=_cparams("parallel", "arbitrary"),
        name="ffn",
    )(*args)


PROJ_CHUNK = 256
EINSHAPE_MIN_STRIDE = 16


def _proj_kernel(h_ref, w_ref, g_ref, o_ref, *scratch, norm_width, norm_tiles, r):
    tm, tn = h_ref.shape[1], w_ref.shape[1]
    rows_per = tm // r
    nw = norm_width if norm_width else PROJ_CHUNK
    assert PROJ_CHUNK % nw == 0 or nw % PROJ_CHUNK == 0
    chunk = max(PROJ_CHUNK, nw)

    def emit(normed):
        for c0 in range(0, tn, chunk):
            res = _dot(h_ref[0], w_ref[:, c0:c0 + chunk])
            if r == 1:
                if not normed:
                    o_ref[0, 0, :, c0:c0 + chunk] = res.astype(o_ref.dtype)
                    continue
                for c in range(c0, c0 + chunk, nw):
                    blk = res[:, c - c0:c - c0 + nw]
                    o_ref[0, 0, :, c:c + nw] = _rms(blk, g_ref[0, :, c:c + nw]).astype(o_ref.dtype)
                continue
            if r >= EINSHAPE_MIN_STRIDE:
                for c in range(c0, c0 + chunk, LANES):
                    blk = pltpu.einshape("(jm)l->mjl", res[:, c - c0:c - c0 + LANES], m=r)
                    if normed:
                        blk = _rms(blk, g_ref[0, :, c:c + LANES])
                    o_ref[0, :, :, c:c + LANES] = blk.astype(o_ref.dtype)
                continue
            acc_ref, = scratch
            for c in range(c0, c0 + chunk, LANES):
                acc_ref[c // LANES] = res[:, c - c0:c - c0 + LANES]
            for m in range(r):
                rows = pl.ds(m, rows_per, stride=r)
                for c in range(c0, c0 + chunk, LANES):
                    blk = acc_ref[c // LANES, rows, :]
                    if normed:
                        blk = _rms(blk, g_ref[0, :, c:c + LANES])
                    o_ref[0, m, :, c:c + LANES] = blk.astype(o_ref.dtype)

    if r > 1:
        assert norm_width in (0, LANES)
    if norm_width == 0:
        emit(False)
    elif norm_tiles is None:
        emit(True)
    else:
        j = pl.program_id(2)
        pl.when(j < norm_tiles)(lambda: emit(True))
        pl.when(j >= norm_tiles)(lambda: emit(False))


def proj(h3, w, col0, ncols, gains, *, out_dtype, tm, tn, norm_width=0, norm_tiles=None, r=1):
    nb, S, K = h3.shape
    assert col0 % tn == 0 and ncols % tn == 0 and S % tm == 0 and tm % (8 * r) == 0
    j0 = col0 // tn
    nj = ncols // tn
    if gains is None:
        gains = jnp.ones((nj, 1, tn), F32)
    strided = 1 < r < EINSHAPE_MIN_STRIDE
    scratch = [pltpu.VMEM((tn // LANES, tm, LANES), F32)] if strided else []
    kern = functools.partial(_proj_kernel, norm_width=norm_width, norm_tiles=norm_tiles, r=r)
    return pl.pallas_call(
        kern,
        grid=(nb, S // tm, nj),
        in_specs=[pl.BlockSpec((1, tm, K), lambda b, i, j: (b, i, 0)),
                  pl.BlockSpec((K, tn), lambda b, i, j: (0, j0 + j)),
                  pl.BlockSpec((1, 1, tn), lambda b, i, j: (j, 0, 0))],
        out_specs=pl.BlockSpec((1, r, tm // r, tn), lambda b, i, j: (b, 0, i, j)),
        out_shape=jax.ShapeDtypeStruct((nb, r, S // r, ncols), out_dtype),
        scratch_shapes=scratch,
        compiler_params=_cparams("parallel", "parallel", "arbitrary"),
        name="proj",
    )(h3, w, gains)


def _side_proj_kernel(h_ref, w_ref, g_ref, q_ref, dt_ref, *, head_dim):
    mw = q_ref.shape[2]
    for c in range(mw // head_dim):
        cols = slice(c * head_dim, (c + 1) * head_dim)
        q_ref[0, :, cols] = _rms(_dot(h_ref[0], w_ref[:, cols]), g_ref[...]).astype(q_ref.dtype)
    dt_ref[0] = _dot(h_ref[0], w_ref[:, mw:])


def side_proj(h3, w, q_gain, tm):
    nb, S, K = h3.shape
    head_dim = q_gain.shape[0]
    n = w.shape[1]
    mw = n - LANES
    return pl.pallas_call(
        functools.partial(_side_proj_kernel, head_dim=head_dim),
        grid=(nb, S // tm),
        in_specs=[pl.BlockSpec((1, tm, K), lambda b, i: (b, i, 0)),
                  pl.BlockSpec((K, n), lambda b, i: (0, 0)),
                  pl.BlockSpec((1, head_dim), lambda b, i: (0, 0))],
        out_specs=[pl.BlockSpec((1, tm, mw), lambda b, i: (b, i, 0)),
                   pl.BlockSpec((1, tm, LANES), lambda b, i: (b, i, 0))],
        out_shape=[jax.ShapeDtypeStruct((nb, S, mw), BF16), jax.ShapeDtypeStruct((nb, S, LANES), F32)],
        compiler_params=_cparams("parallel", "parallel"),
        name="side_proj",
    )(h3, w, q_gain.reshape(1, head_dim))


def _outproj_kernel(x_ref, a1_ref, a2_ref, w1_ref, w2_ref, o_ref):
    for c0 in range(0, o_ref.shape[1], PROJ_CHUNK):
        cols = slice(c0, c0 + PROJ_CHUNK)
        o_ref[:, cols] = x_ref[:, cols] + _dot(a1_ref[...], w1_ref[:, cols]) + _dot(a2_ref[...], w2_ref[:, cols])


def outproj(x2, a1, a2, w1, w2, tm, tn):
    T, D = x2.shape
    k1, k2 = a1.shape[1], a2.shape[1]
    return pl.pallas_call(
        _outproj_kernel,
        grid=(T // tm, D // tn),
        in_specs=[pl.BlockSpec((tm, tn), lambda i, j: (i, j)),
                  pl.BlockSpec((tm, k1), lambda i, j: (i, 0)),
                  pl.BlockSpec((tm, k2), lambda i, j: (i, 0)),
                  pl.BlockSpec((k1, tn), lambda i, j: (0, j)),
                  pl.BlockSpec((k2, tn), lambda i, j: (0, j))],
        out_specs=pl.BlockSpec((tm, tn), lambda i, j: (i, j)),
        out_shape=jax.ShapeDtypeStruct((T, D), F32),
        compiler_params=_cparams("parallel", "arbitrary"),
        name="outproj",
    )(x2, a1, a2, w1, w2)


def _memkv_kernel(mem_ref, g_ref, w_ref, kg_ref, k_ref, v_ref):
    mw = k_ref.shape[2]
    hd = kg_ref.shape[1]
    memn = _rms(mem_ref[0], g_ref[...]).astype(BF16)
    kv = _dot(memn, w_ref[...])
    for hh in range(mw // hd):
        cols = slice(hh * hd, (hh + 1) * hd)
        k_ref[0, :, cols] = _rms(kv[:, cols], kg_ref[...]).astype(BF16)
    v_ref[0] = kv[:, mw:].astype(BF16)


def mem_kv(mem, mem_gain, w_kv, k_gain):
    nb, M, D = mem.shape
    mw = w_kv.shape[1] // 2
    hd = k_gain.shape[0]
    out = jax.ShapeDtypeStruct((nb, M, mw), BF16)
    return pl.pallas_call(
        _memkv_kernel,
        grid=(nb,),
        in_specs=[pl.BlockSpec((1, M, D), lambda b: (b, 0, 0)),
                  pl.BlockSpec((1, D), lambda b: (0, 0)),
                  pl.BlockSpec((D, 2 * mw), lambda b: (0, 0)),
                  pl.BlockSpec((1, hd), lambda b: (0, 0))],
        out_specs=[pl.BlockSpec((1, M, mw), lambda b: (b, 0, 0))] * 2,
        out_shape=[out, out],
        compiler_params=_cparams("parallel"),
        name="mem_kv",
    )(mem, mem_gain.reshape(1, D), w_kv, k_gain.reshape(1, hd))


def _memattn_kernel(q_ref, k_ref, v_ref, o_ref, *, heads):
    hd = q_ref.shape[2] // heads
    scale = hd ** -0.5
    for hh in range(heads):
        cols = slice(hh * hd, (hh + 1) * hd)
        s = _dot_nt(q_ref[0, :, cols], k_ref[0, :, cols]) * scale
        e = jnp.exp(s - jnp.max(s, axis=-1, keepdims=True))
        p = e / jnp.sum(e, axis=-1, keepdims=True)
        o_ref[0, :, cols] = _dot(p.astype(BF16), v_ref[0, :, cols]).astype(o_ref.dtype)


def mem_attention(q, k, v, tq):
    nb, S, mw = q.shape
    M = k.shape[1]
    return pl.pallas_call(
        functools.partial(_memattn_kernel, heads=MEM_HEADS),
        grid=(nb, S // tq),
        in_specs=[pl.BlockSpec((1, tq, mw), lambda b, i: (b, i, 0)),
                  pl.BlockSpec((1, M, mw), lambda b, i: (b, 0, 0)),
                  pl.BlockSpec((1, M, mw), lambda b, i: (b, 0, 0))],
        out_specs=pl.BlockSpec((1, tq, mw), lambda b, i: (b, i, 0)),
        out_shape=jax.ShapeDtypeStruct((nb, S, mw), BF16),
        compiler_params=_cparams("parallel", "parallel"),
        name="mem_attn",
    )(q, k, v)


SSD_GROUP_WIDTH = SSD_HEADS_PER_GROUP * SSD_HEAD_DIM
SSD_GROUP_CONV = SSD_GROUP_WIDTH + 2 * SSD_STATE
SSD_GROUP_COLS = SSD_GROUP_CONV + SSD_GROUP_WIDTH
HALO_ROWS = 16


def _ssd_inproj_kernel(hp_ref, h_ref, hn_ref, w_ref, cw_ref, cb_ref, o_ref, lhs_ref, ext_ref):
    i = pl.program_id(1)
    tm = h_ref.shape[1]
    lo = HALO_ROWS - SSD_CONV // 2

    @pl.when(pl.program_id(2) == 0)
    def _():
        lhs_ref[0:HALO_ROWS] = jnp.where(i > 0, hp_ref[0], jnp.zeros_like(hp_ref[0]))
        lhs_ref[HALO_ROWS:HALO_ROWS + tm] = h_ref[0]
        lhs_ref[HALO_ROWS + tm:] = jnp.where(i < pl.num_programs(1) - 1, hn_ref[0], jnp.zeros_like(hn_ref[0]))

    for c0 in range(0, SSD_GROUP_COLS, PROJ_CHUNK):
        if c0 >= SSD_GROUP_CONV:
            o_ref[0, :, c0:c0 + PROJ_CHUNK] = jax.nn.silu(_dot(h_ref[0], w_ref[:, c0:c0 + PROJ_CHUNK]))
            continue
        half = (tm + 2 * HALO_ROWS) // 2
        for r0 in (0, half):
            res = _dot(lhs_ref[r0:r0 + half, :], w_ref[:, c0:c0 + PROJ_CHUNK])
            for c in range(c0, c0 + PROJ_CHUNK, LANES):
                ext_ref[c // LANES, r0:r0 + half] = res[:, c - c0:c - c0 + LANES]
        for c in range(c0, c0 + PROJ_CHUNK, LANES):
            s = c // LANES
            if c >= SSD_GROUP_CONV:
                o_ref[0, :, c:c + LANES] = jax.nn.silu(ext_ref[s, HALO_ROWS:HALO_ROWS + tm, :])
                continue
            acc = cb_ref[:, c:c + LANES] + cw_ref[0:1, c:c + LANES] * ext_ref[s, lo:lo + tm, :]
            for k in range(1, SSD_CONV):
                acc = acc + cw_ref[k:k + 1, c:c + LANES] * ext_ref[s, lo + k:lo + k + tm, :]
            o_ref[0, :, c:c + LANES] = jax.nn.silu(acc)


def ssd_inproj(h3, w, conv_w, conv_b, tm):
    nb, S, K = h3.shape
    G, gc, cols = SSD_GROUPS, SSD_GROUP_CONV, SSD_GROUP_COLS
    assert PROJ_CHUNK % LANES == 0 and gc % LANES == 0 and cols % PROJ_CHUNK == 0 and tm % HALO_ROWS == 0
    nh = tm // HALO_ROWS
    last = S // HALO_ROWS - 1
    return pl.pallas_call(
        _ssd_inproj_kernel,
        grid=(nb, S // tm, G),
        in_specs=[pl.BlockSpec((1, HALO_ROWS, K), lambda b, i, g: (b, jnp.maximum(i * nh - 1, 0), 0)),
                  pl.BlockSpec((1, tm, K), lambda b, i, g: (b, i, 0)),
                  pl.BlockSpec((1, HALO_ROWS, K), lambda b, i, g: (b, jnp.minimum((i + 1) * nh, last), 0)),
                  pl.BlockSpec((K, cols), lambda b, i, g: (0, g)),
                  pl.BlockSpec((SSD_CONV, gc), lambda b, i, g: (0, g)),
                  pl.BlockSpec((1, gc), lambda b, i, g: (0, g))],
        out_specs=pl.BlockSpec((1, tm, cols), lambda b, i, g: (b, i, g)),
        out_shape=jax.ShapeDtypeStruct((nb, S, G * cols), F32),
        scratch_shapes=[pltpu.VMEM((tm + 2 * HALO_ROWS, K), BF16),
                        pltpu.VMEM((pl.cdiv(gc, PROJ_CHUNK) * PROJ_CHUNK // LANES, tm + 2 * HALO_ROWS, LANES), F32)],
        compiler_params=_cparams("parallel", "parallel", "arbitrary"),
        name="ssd_inproj",
    )(h3, h3, h3, w, conv_w, conv_b.reshape(1, G * gc))


LOG2E = math.log2(math.e)


def _ssd_prep_kernel(dtr_ref, bias_ref, alog_ref, tr_ref):
    Q = SSD_CHUNK
    ii = lax.broadcasted_iota(jnp.int32, (Q, Q), 0)
    jj = lax.broadcasted_iota(jnp.int32, (Q, Q), 1)
    tril = (jj <= ii).astype(F32)
    triu = (jj >= ii).astype(F32)
    lane = lax.broadcasted_iota(jnp.int32, (Q, LANES), 1)
    is_bwd = (lane % GROUP_LANES) >= GROUP_LANES // 2
    hi = lax.Precision.HIGHEST
    for i in range(tr_ref.shape[1]):
        x = dtr_ref[0, i * Q:(i + 1) * Q, :] + bias_ref[...]
        dt = jnp.maximum(x, 0.0) + jnp.log1p(jnp.exp(-jnp.abs(x)))
        a = dt * (-jnp.exp(alog_ref[...]))
        cs_fwd = jnp.dot(tril, a, precision=hi, preferred_element_type=F32)
        cs_bwd = jnp.dot(triu, a, precision=hi, preferred_element_type=F32)
        cs = jnp.where(is_bwd, cs_bwd, cs_fwd)
        last = jnp.where(is_bwd[0:1], cs[0:1], cs[Q - 1:Q])
        cs2 = cs * LOG2E
        parts = (cs2, cs2 - jnp.log2(dt), jnp.exp(last - cs) * dt, jnp.broadcast_to(jnp.exp(last), (Q, LANES)))
        for p, part in enumerate(parts):
            rows = part.T
            for g in range(SSD_GROUPS):
                tr_ref[0, i, g, p * GROUP_LANES:(p + 1) * GROUP_LANES] = rows[g * GROUP_LANES:(g + 1) * GROUP_LANES]


def ssd_prep(dtr, bias_l, alog_l):
    nb, S, _ = dtr.shape
    Q = SSD_CHUNK
    nc = S // Q
    kp = next(k for k in (8, 4, 2, 1) if nc % k == 0)
    vec_spec = pl.BlockSpec((1, LANES), lambda b, c: (0, 0))
    return pl.pallas_call(
        _ssd_prep_kernel,
        grid=(nb, nc // kp),
        in_specs=[pl.BlockSpec((1, kp * Q, LANES), lambda b, c: (b, c, 0)), vec_spec, vec_spec],
        out_specs=pl.BlockSpec((1, kp, SSD_GROUPS, 4 * GROUP_LANES, Q), lambda b, c: (b, c, 0, 0, 0)),
        out_shape=jax.ShapeDtypeStruct((nb, nc, SSD_GROUPS, 4 * GROUP_LANES, Q), F32),
        compiler_params=_cparams("parallel", "parallel"),
        name="ssd_prep",
    )(dtr, bias_l, alog_l)


def _ssd_direction(d, g, sub, xc_ref, tr_ref, st_ref):
    Q = SSD_CHUNK
    P, N, gw = SSD_HEAD_DIM, SSD_STATE, SSD_GROUP_WIDTH
    rows = slice(sub * Q, (sub + 1) * Q)
    xs = xc_ref[0, rows, 0:gw]
    bm = xc_ref[0, rows, gw:gw + N]
    cm = xc_ref[0, rows, gw + N:gw + 2 * N]
    cb = _dot_nt(cm.astype(BF16), bm.astype(BF16))
    bt = bm.T

    cs2T, rowT, wT, cdT = (tr_ref[0, sub, 0, part * GROUP_LANES:(part + 1) * GROUP_LANES, :] for part in range(4))
    ii = lax.broadcasted_iota(jnp.int32, (Q, Q), 0)
    jj = lax.broadcasted_iota(jnp.int32, (Q, Q), 1)
    causal = (jj <= ii) if d == 0 else (jj >= ii)
    lo = lax.broadcasted_iota(jnp.int32, (Q, LANES), 1) < P
    st = st_ref[d]
    ys, sts = [], []
    for p in range(SSD_HEADS_PER_GROUP // 2):
        cols = slice(p * LANES, (p + 1) * LANES)
        ms, ss = [], []
        k0 = d * (GROUP_LANES // 2) + 2 * p
        for k in (k0, k0 + 1):
            col2 = jnp.broadcast_to(cs2T[k:k + 1, :], (Q, Q)).T
            decay_dt = jnp.exp2(jnp.where(causal, col2 - rowT[k:k + 1, :], -jnp.inf))
            ms.append((cb * decay_dt).astype(BF16))
            ms.append((cm * jnp.exp2(col2)).astype(BF16))
            ss.append((bt * wT[k:k + 1, :]).astype(BF16))
        cd = jnp.where(lo[0:1], cdT[k0:k0 + 1, :], cdT[k0 + 1:k0 + 2, :])
        xp, sp = xs[:, cols], st[:, cols]
        x_lo, x_hi = jnp.where(lo, xp, 0.0).astype(BF16), jnp.where(lo, 0.0, xp).astype(BF16)
        s_lo, s_hi = jnp.where(lo, sp, 0.0).astype(BF16), jnp.where(lo, 0.0, sp).astype(BF16)
        ys.append(_dot(jnp.concatenate(ms, axis=1), jnp.concatenate([x_lo, s_lo, x_hi, s_hi], axis=0)))
        sts.append(cd * sp + _dot(jnp.concatenate(ss, axis=1), jnp.concatenate([x_lo, x_hi], axis=0)))
    st_ref[d] = jnp.concatenate(sts, axis=1)
    return jnp.concatenate(ys, axis=1)


def _ssd_scan_kernel(xcf, xcb, trf, trb, dskip_ref, ng_ref, y_ref, st_ref, ysum_ref, *, kc):
    g = pl.program_id(1)
    c = pl.program_id(2)
    nsteps = pl.num_programs(2)
    Q = SSD_CHUNK

    @pl.when(c == 0)
    def _():
        st_ref[...] = jnp.zeros_like(st_ref)
        ysum_ref[...] = jnp.zeros_like(ysum_ref)

    def finish(y_dir, row0, xc_ref, sub):
        rows = pl.ds(pl.multiple_of(row0, Q), Q)
        blk = slice(sub * Q, (sub + 1) * Q)
        tot = ysum_ref[rows, :] + y_dir + dskip_ref[...] * xc_ref[0, blk, 0:SSD_GROUP_WIDTH]
        ysum_ref[rows, :] = y_dir
        gated = tot * xc_ref[0, blk, SSD_GROUP_CONV:SSD_GROUP_COLS]
        y_ref[0, rows, :] = _rms(gated, ng_ref[...]).astype(y_ref.dtype)

    for i in range(kc):
        y_f = _ssd_direction(0, g, i, xcf, trf, st_ref)
        finish(y_f, (c * kc + i) * Q, xcf, i)
        y_b = _ssd_direction(1, g, kc - 1 - i, xcb, trb, st_ref)
        finish(y_b, ((nsteps - 1 - c) * kc + kc - 1 - i) * Q, xcb, kc - 1 - i)


def ssd_scan(xc, tr, dskip, norm_g, kc):
    nb, S, _ = xc.shape
    Q = SSD_CHUNK
    nsteps = S // (Q * kc)
    assert nsteps % 2 == 0 and Q == LANES
    G, N, gw = SSD_GROUPS, SSD_STATE, SSD_GROUP_WIDTH
    inner = G * gw

    def both(shape, f):
        return [pl.BlockSpec(shape, lambda b, g, c: f(b, g, c)),
                pl.BlockSpec(shape, lambda b, g, c: f(b, g, nsteps - 1 - c))]

    in_specs = (both((1, kc * Q, SSD_GROUP_COLS), lambda b, g, c: (b, c, g))
                + both((1, kc, 1, 4 * GROUP_LANES, Q), lambda b, g, c: (b, c, g, 0, 0))
                + [pl.BlockSpec((1, gw), lambda b, g, c: (0, g)),
                   pl.BlockSpec((1, gw), lambda b, g, c: (0, g))])
    return pl.pallas_call(
        functools.partial(_ssd_scan_kernel, kc=kc),
        grid=(nb, G, nsteps),
        in_specs=in_specs,
        out_specs=pl.BlockSpec((1, S, gw), lambda b, g, c: (b, 0, g)),
        out_shape=jax.ShapeDtypeStruct((nb, S, inner), BF16),
        scratch_shapes=[pltpu.VMEM((2, N, gw), F32), pltpu.VMEM((S, gw), F32)],
        compiler_params=_cparams("parallel", "parallel", "arbitrary"),
        name="ssd_scan",
    )(xc, xc, tr, tr, dskip.reshape(1, inner), norm_g.reshape(1, inner))


def _t5_bucket_np(rel):
    half = REL_BUCKETS // 2
    exact = half // 2
    n = np.abs(rel)
    far = exact + (np.log(np.maximum(n, 1).astype(np.float32) / np.float32(exact))
                   / np.float32(math.log(REL_MAX_DISTANCE / exact)) * np.float32(half - exact)).astype(np.int32)
    far = np.minimum(far, half - 1)
    return np.where(rel > 0, half, 0) + np.where(n < exact, n, far)


def _dil_bucket_rows(win):
    d = np.arange(win)
    rel = d - DIL_HALF
    rows = []
    for _, dilation in DIL_CONFIGS:
        rows.append(np.where(d <= 2 * DIL_HALF, _t5_bucket_np(rel * dilation), -1))
    return np.broadcast_to(np.stack(rows)[:, None, :], (len(DIL_CONFIGS), 8, win)).astype(np.int32)


def _dil_kernel(tbl_ref, bm_ref, *refs, seq, seg, tq):
    ngroups = len(DIL_CONFIGS)
    win = tq + 2 * DIL_HALF
    in_refs = refs[:7 * ngroups]
    o_ref = refs[7 * ngroups]
    bias_ref, m_ref, l_ref, a_ref, kext_ref, vext_ref = refs[7 * ngroups + 1:]
    h = pl.program_id(1)
    t = pl.program_id(2)
    scale = DIL_HEAD_DIM ** -0.5

    @pl.when(t == 0)
    def _():
        m_ref[...] = jnp.full_like(m_ref, -jnp.inf)
        l_ref[...] = jnp.zeros_like(l_ref)
        a_ref[...] = jnp.zeros_like(a_ref)
        kcol = lax.broadcasted_iota(jnp.int32, (tq, win), 1)
        for gi in range(ngroups):
            ids = bm_ref[gi]
            row = jnp.zeros(ids.shape, F32)
            for u in range(REL_BUCKETS):
                row = jnp.where(ids == u, tbl_ref[u, gi * DIL_HEADS + h], row)
            row = jnp.where(ids < 0, -jnp.inf, row)
            band = pltpu.roll(jnp.broadcast_to(row[0:1], (tq, win)), 0, 1, stride=1, stride_axis=0)
            no_left = jnp.where(kcol >= DIL_HALF, band, -jnp.inf)
            bias_ref[gi, 0] = band
            bias_ref[gi, 1] = no_left
            bias_ref[gi, 2] = jnp.where(kcol < win - DIL_HALF, band, -jnp.inf)
            bias_ref[gi, 3] = jnp.where(kcol < win - DIL_HALF, no_left, -jnp.inf)

    for gi, (_, r) in enumerate(DIL_CONFIGS):
        q_ref, kp, kc, kn, vp, vc, vn = in_refs[7 * gi:7 * gi + 7]
        sub_len = seq // r
        piece = min(seg, sub_len)
        nblk = piece // tq
        for ext, prev, cur, nxt in ((kext_ref, kp, kc, kn), (vext_ref, vp, vc, vn)):
            ext[gi, 0:DIL_HALF] = prev[0]
            ext[gi, DIL_HALF:DIL_HALF + seg] = cur[0]
            ext[gi, DIL_HALF + seg:] = nxt[0]
        for p in range(seg // piece):
            first_row = t * seg + p * piece
            m_res = first_row // sub_len
            j0 = first_row % sub_len
            for jb in range(nblk):
                row0 = p * piece + jb * tq
                q = q_ref[0, row0:row0 + tq, :]
                kw = kext_ref[gi, row0:row0 + win, :]
                vw = vext_ref[gi, row0:row0 + win, :]
                variant = 0
                if jb == 0:
                    variant = variant + (j0 == 0).astype(jnp.int32)
                if jb == nblk - 1:
                    variant = variant + 2 * (j0 + piece == sub_len).astype(jnp.int32)
                s = _dot_nt(q, kw) * scale + bias_ref[gi, variant]
                mb = jnp.max(s, axis=-1, keepdims=True)
                e = jnp.exp(s - mb)
                lb = jnp.sum(e, axis=-1, keepdims=True)
                acc = _dot(e.astype(BF16), vw)
                start = (j0 + jb * tq) * r + m_res
                rows = pl.ds(start, tq, stride=r) if r > 1 else pl.ds(start, tq)
                m_old = m_ref[rows, :]
                m_new = jnp.maximum(m_old, mb)
                alpha = jnp.exp(m_old - m_new)
                beta = jnp.exp(mb - m_new)
                l_new = alpha * l_ref[rows, :] + beta * lb
                a_new = alpha * a_ref[rows, :] + beta * acc
                m_ref[rows, :] = m_new
                l_ref[rows, :] = l_new
                a_ref[rows, :] = a_new

    @pl.when(t == pl.num_programs(2) - 1)
    def _():
        o_ref[0] = (a_ref[...] / l_ref[...]).astype(o_ref.dtype)


def dilated_attention(qkvs, rel_bias):
    nb, S, _ = qkvs[0].shape
    hd = DIL_HEAD_DIM
    seg = S // DIL_SEGS
    tq = _tile(seg, DIL_QUERY_BLOCK)
    win = tq + 2 * DIL_HALF
    assert seg % tq == 0 and tq % LANES == 0
    for w, r in DIL_CONFIGS:
        piece = min(seg, S // r)
        assert w == 2 * DIL_HALF * r and (S // r) % piece == 0 and seg % piece == 0 and piece % tq == 0
    nhb = seg // DIL_HALF
    last_hb = S // DIL_HALF - 1
    in_specs = [pl.BlockSpec(memory_space=pltpu.SMEM),
                pl.BlockSpec((len(DIL_CONFIGS), 8, win), lambda b, h, t: (0, 0, 0))]
    args = [rel_bias, jnp.asarray(_dil_bucket_rows(win))]
    for g in range(len(DIL_CONFIGS)):
        in_specs.append(pl.BlockSpec((1, seg, hd), lambda b, h, t: (b, t, h)))
        for part in (1, 2):
            off = part * DIL_HEADS
            in_specs += [
                pl.BlockSpec((1, DIL_HALF, hd), lambda b, h, t, off=off: (b, jnp.maximum(t * nhb - 1, 0), off + h)),
                pl.BlockSpec((1, seg, hd), lambda b, h, t, off=off: (b, t, off + h)),
                pl.BlockSpec((1, DIL_HALF, hd), lambda b, h, t, off=off: (b, jnp.minimum((t + 1) * nhb, last_hb), off + h)),
            ]
        args += [qkvs[g]] * 7
    kern = functools.partial(_dil_kernel, seq=S, seg=seg, tq=tq)
    return pl.pallas_call(
        kern,
        grid=(nb, DIL_HEADS, DIL_SEGS),
        in_specs=in_specs,
        out_specs=pl.BlockSpec((1, S, hd), lambda b, h, t: (b, 0, h)),
        out_shape=jax.ShapeDtypeStruct((nb, S, DIL_HEADS * hd), BF16),
        scratch_shapes=[pltpu.VMEM((len(DIL_CONFIGS), 4, tq, win), F32),
                        pltpu.VMEM((S, hd), F32), pltpu.VMEM((S, hd), F32), pltpu.VMEM((S, hd), F32),
                        pltpu.VMEM((len(DIL_CONFIGS), seg + 2 * DIL_HALF, hd), BF16),
                        pltpu.VMEM((len(DIL_CONFIGS), seg + 2 * DIL_HALF, hd), BF16)],
        compiler_params=_cparams("parallel", "parallel", "arbitrary"),
        name="dil_attn",
    )(*args)


def _ssd_lane_layout(v):
    v = v.reshape(2, SSD_GROUPS, SSD_HEADS_PER_GROUP).transpose(1, 0, 2)
    v = jnp.pad(v, ((0, 0), (0, 0), (0, GROUP_LANES // 2 - SSD_HEADS_PER_GROUP)))
    return v.reshape(1, LANES)


def _ssd_mixer(x2, h3, kmem, vmem_, w_in, conv_w, conv_b, dt_bias, a_log, d_skip, norm_g, w_out, mem_q_gain):
    nb, S, D = h3.shape
    inner = SSD_GROUPS * SSD_HEADS_PER_GROUP * SSD_HEAD_DIM
    conv_ch = inner + 2 * SSD_GROUPS * SSD_STATE
    nheads = SSD_GROUPS * SSD_HEADS_PER_GROUP
    mw = kmem.shape[2]
    tm = _tile(S, ROW_TILE)
    w_bf = w_in.astype(BF16)

    def grouped(t, z=None):
        lead = t.shape[:-1]
        parts = [t[..., :inner].reshape(*lead, SSD_GROUPS, SSD_GROUP_WIDTH),
                 t[..., inner:inner + SSD_GROUPS * SSD_STATE].reshape(*lead, SSD_GROUPS, SSD_STATE),
                 t[..., inner + SSD_GROUPS * SSD_STATE:].reshape(*lead, SSD_GROUPS, SSD_STATE)]
        if z is not None:
            parts.append(z.reshape(*lead, SSD_GROUPS, SSD_GROUP_WIDTH))
        return jnp.concatenate(parts, axis=-1).reshape(*lead, -1)

    w_p1 = grouped(w_bf[:, inner:inner + conv_ch], z=w_bf[:, :inner])
    xc = ssd_inproj(h3, w_p1, grouped(conv_w), grouped(conv_b), tm)
    w_dt = w_in[:, inner + conv_ch:inner + conv_ch + 2 * nheads]
    w_dt = w_dt.reshape(D, 2, SSD_GROUPS, SSD_HEADS_PER_GROUP).transpose(0, 2, 1, 3)
    w_dt = jnp.pad(w_dt, ((0, 0), (0, 0), (0, 0), (0, GROUP_LANES // 2 - SSD_HEADS_PER_GROUP)))
    q_off = inner + conv_ch + 2 * nheads
    w_side = jnp.concatenate([w_bf[:, q_off:], w_dt.reshape(D, LANES).astype(BF16)], axis=1)
    q_mem, dtr = side_proj(h3, w_side, mem_q_gain, tm)

    tr = ssd_prep(dtr, _ssd_lane_layout(dt_bias), _ssd_lane_layout(a_log))
    dskip = jnp.repeat(d_skip, SSD_HEAD_DIM)
    nchunks = S // SSD_CHUNK
    kc = next(k for k in (8, 4, 2, 1) if nchunks % (2 * k) == 0)
    y = ssd_scan(xc, tr, dskip, norm_g, kc)
    o_mem = mem_attention(q_mem, kmem, vmem_, tq=tm)
    w_out_bf = w_out.astype(BF16)
    return outproj(x2, y.reshape(nb * S, inner), o_mem.reshape(nb * S, mw),
                   w_out_bf[:inner], w_out_bf[inner:], tm=_tile(nb * S, ROW_TILE), tn=_tile(D, OUT_TILE))


def _dil_mixer(x2, h3, kmem, vmem_, w_in, q_gain, k_gain, w_out, rel_bias, mem_q_gain):
    nb, S, D = h3.shape
    width = DIL_HEADS * DIL_HEAD_DIM
    mw = kmem.shape[2]
    tm = _tile(S, ROW_TILE)
    w_bf = w_in.astype(BF16)
    qkvs = []
    for g, (_, r) in enumerate(DIL_CONFIGS):
        gains = jnp.stack([jnp.tile(q_gain[g], DIL_HEADS), jnp.tile(k_gain[g], DIL_HEADS),
                           jnp.ones((width,), F32)]).reshape(3, 1, width)
        out = proj(h3, w_bf, g * 3 * width, 3 * width, gains, out_dtype=BF16, tm=tm, tn=width,
                   norm_width=DIL_HEAD_DIM, norm_tiles=2, r=r)
        qkvs.append(out.reshape(nb, S, 3 * width))
    q_off = len(DIL_CONFIGS) * 3 * width
    hd = mem_q_gain.shape[0]
    qg = jnp.tile(mem_q_gain, mw // hd).reshape(1, 1, mw)
    q_mem = proj(h3, w_bf, q_off, mw, qg, out_dtype=BF16, tm=tm, tn=mw, norm_width=hd)[:, 0]
    o = dilated_attention(qkvs, rel_bias)
    o_mem = mem_attention(q_mem, kmem, vmem_, tq=tm)
    w_out_bf = w_out.astype(BF16)
    return outproj(x2, o.reshape(nb * S, width), o_mem.reshape(nb * S, mw),
                   w_out_bf[:width], w_out_bf[width:], tm=_tile(nb * S, ROW_TILE), tn=_tile(D, OUT_TILE))


def kernel(x, mem, rel_bias, ffn_norm, ffn_w_in, ffn_w_out, mix_norm, mem_norm, mem_w_kv, mem_q_gain, mem_k_gain, ssd_w_in, ssd_conv_w, ssd_conv_b, ssd_dt_bias, ssd_A_log, ssd_D, ssd_norm, ssd_w_out, dil_w_in, dil_q_gain, dil_k_gain, dil_w_out):
    nb, S, D = x.shape
    depth = ffn_norm.shape[0]
    T = nb * S
    x2 = x.reshape(T, D)
    tm_ffn, tf = _tile(T, ROW_TILE), _tile(ffn_w_out.shape[2], FF_TILE)
    ffn_w_in = ffn_w_in.astype(BF16)
    ffn_w_out = ffn_w_out.astype(BF16)
    for i in range(depth):
        x2, h2 = ffn(x2, ffn_norm[i, 0], ffn_w_in, ffn_w_out, i, 0, tm_ffn, tf, next_gain=mix_norm[i])
        h3 = h2.reshape(nb, S, D)
        kmem, vmem_ = mem_kv(mem, mem_norm[i], mem_w_kv[i].astype(BF16), mem_k_gain[i])
        j = i // 2
        if i % 2 == 0:
            x2 = _ssd_mixer(x2, h3, kmem, vmem_, ssd_w_in[j], ssd_conv_w[j], ssd_conv_b[j], ssd_dt_bias[j],
                            ssd_A_log[j], ssd_D[j], ssd_norm[j], ssd_w_out[j], mem_q_gain[i])
        else:
            x2 = _dil_mixer(x2, h3, kmem, vmem_, dil_w_in[j], dil_q_gain[j], dil_k_gain[j], dil_w_out[j],
                            rel_bias, mem_q_gain[i])
        x2 = ffn(x2, ffn_norm[i, 1], ffn_w_in, ffn_w_out, i, 1, tm_ffn, tf)
    return x2.reshape(nb, S, D)
```

```python
import functools
import math

import jax
import jax.numpy as jnp
import numpy as np
from jax import lax
from jax.experimental import pallas as pl
from jax.experimental.pallas import tpu as pltpu

F32 = jnp.float32
BF16 = jnp.bfloat16
EPS = 1e-6

MEM_HEADS = 4
SSD_HEAD_DIM = 64
SSD_GROUPS = 8
SSD_HEADS_PER_GROUP = 6
SSD_STATE = 128
SSD_CONV = 5
SSD_CHUNK = 128
DIL_CONFIGS = ((128, 1), (512, 4), (2048, 16))
DIL_HEADS = 8
DIL_HEAD_DIM = 128
DIL_HALF = 64
DIL_SEGS = 4
DIL_QUERY_BLOCK = 128
REL_BUCKETS = 32
REL_MAX_DISTANCE = 1024

LANES = 128
VMEM_LIMIT_BYTES = 62 * 1024 * 1024
GROUP_LANES = 16

ROW_TILE = 1024
FF_TILE = 512
OUT_TILE = 1024


def _tile(n, pref):
    return pref if n % pref == 0 else n


def _cparams(*sem):
    return pltpu.CompilerParams(dimension_semantics=sem, vmem_limit_bytes=VMEM_LIMIT_BYTES)


def _rms(x, gain):
    ms = jnp.mean(x * x, axis=-1, keepdims=True)
    return x * lax.rsqrt(ms + EPS) * gain


def _dot(a, b):
    return jnp.dot(a, b, preferred_element_type=F32)


def _dot_nt(a, b):
    return lax.dot_general(a, b, (((1,), (1,)), ((), ())), preferred_element_type=F32)


FFN_SUB_ROWS = 512


def _ffn_kernel(x_ref, g_ref, wg_ref, wu_ref, wo_ref, *rest, emit_norm):
    if emit_norm:
        g2_ref, o_ref, hn_ref = rest
        h_ref = hn_ref
    else:
        o_ref, h_ref = rest
    j = pl.program_id(1)

    tm = x_ref.shape[0]
    sub = min(tm, FFN_SUB_ROWS)

    @pl.when(j == 0)
    def _():
        for r0 in range(0, tm, sub // 2):
            rows = slice(r0, r0 + sub // 2)
            h_ref[rows, :] = _rms(x_ref[rows, :], g_ref[...]).astype(BF16)

    def step(base_ref):
        for r0 in range(0, tm, sub):
            h = h_ref[r0:r0 + sub, :]
            gate = _dot(h, wg_ref[...])
            up = _dot(h, wu_ref[...])
            a = (jax.nn.silu(gate) * up * 0.5).astype(BF16)
            o_ref[r0:r0 + sub, :] = base_ref[r0:r0 + sub, :] + _dot(a, wo_ref[...])

    pl.when(j == 0)(lambda: step(x_ref))
    pl.when(j > 0)(lambda: step(o_ref))

    if emit_norm:
        @pl.when(j == pl.num_programs(1) - 1)
        def _():
            for r0 in range(0, tm, sub // 2):
                rows = slice(r0, r0 + sub // 2)
                hn_ref[rows, :] = _rms(o_ref[rows, :], g2_ref[...]).astype(hn_ref.dtype)


def ffn(x2, gain, w_in, w_out, layer, which, tm, tf, next_gain=None):
    T, D = x2.shape
    F = w_out.shape[2]
    nf = F // tf
    emit_norm = next_gain is not None
    row_spec = pl.BlockSpec((tm, D), lambda i, j: (i, 0))
    vec_spec = pl.BlockSpec((1, D), lambda i, j: (0, 0))
    in_specs = [row_spec, vec_spec,
                pl.BlockSpec((None, None, D, tf), lambda i, j: (layer, which, 0, j)),
                pl.BlockSpec((None, None, D, tf), lambda i, j: (layer, which, 0, j + nf)),
                pl.BlockSpec((None, None, tf, D), lambda i, j: (layer, which, j, 0))]
    args = [x2, gain.reshape(1, D), w_in, w_in, w_out]
    out_specs, out_shape = row_spec, jax.ShapeDtypeStruct((T, D), F32)
    if emit_norm:
        in_specs.append(vec_spec)
        args.append(next_gain.reshape(1, D))
        out_specs, out_shape = [row_spec, row_spec], [out_shape, jax.ShapeDtypeStruct((T, D), BF16)]
    return pl.pallas_call(
        functools.partial(_ffn_kernel, emit_norm=emit_norm),
        grid=(T // tm, nf),
        in_specs=in_specs,
        out_specs=out_specs,
        out_shape=out_shape,
        scratch_shapes=[] if emit_norm else [pltpu.VMEM((tm, D), BF16)],
        compiler_params=_cparams("parallel", "arbitrary"),
        name="ffn",
    )(*args)


PROJ_CHUNK = 256
EINSHAPE_MIN_STRIDE = 16


def _proj_kernel(h_ref, w_ref, g_ref, o_ref, *scratch, norm_width, norm_tiles, r):
    tm, tn = h_ref.shape[1], w_ref.shape[1]
    rows_per = tm // r
    nw = norm_width if norm_width else PROJ_CHUNK
    assert PROJ_CHUNK % nw == 0 or nw % PROJ_CHUNK == 0
    chunk = max(PROJ_CHUNK, nw)

    def emit(normed):
        for c0 in range(0, tn, chunk):
            res = _dot(h_ref[0], w_ref[:, c0:c0 + chunk])
            if r == 1:
                if not normed:
                    o_ref[0, 0, :, c0:c0 + chunk] = res.astype(o_ref.dtype)
                    continue
                for c in range(c0, c0 + chunk, nw):
                    blk = res[:, c - c0:c - c0 + nw]
                    o_ref[0, 0, :, c:c + nw] = _rms(blk, g_ref[0, :, c:c + nw]).astype(o_ref.dtype)
                continue
            if r >= EINSHAPE_MIN_STRIDE:
                for c in range(c0, c0 + chunk, LANES):
                    blk = pltpu.einshape("(jm)l->mjl", res[:, c - c0:c - c0 + LANES], m=r)
                    if normed:
                        blk = _rms(blk, g_ref[0, :, c:c + LANES])
                    o_ref[0, :, :, c:c + LANES] = blk.astype(o_ref.dtype)
                continue
            acc_ref, = scratch
            for c in range(c0, c0 + chunk, LANES):
                acc_ref[c // LANES] = res[:, c - c0:c - c0 + LANES]
            for m in range(r):
                rows = pl.ds(m, rows_per, stride=r)
                for c in range(c0, c0 + chunk, LANES):
                    blk = acc_ref[c // LANES, rows, :]
                    if normed:
                        blk = _rms(blk, g_ref[0, :, c:c + LANES])
                    o_ref[0, m, :, c:c + LANES] = blk.astype(o_ref.dtype)

    if r > 1:
        assert norm_width in (0, LANES)
    if norm_width == 0:
        emit(False)
    elif norm_tiles is None:
        emit(True)
    else:
        j = pl.program_id(2)
        pl.when(j < norm_tiles)(lambda: emit(True))
        pl.when(j >= norm_tiles)(lambda: emit(False))


def proj(h3, w, col0, ncols, gains, *, out_dtype, tm, tn, norm_width=0, norm_tiles=None, r=1):
    nb, S, K = h3.shape
    assert col0 % tn == 0 and ncols % tn == 0 and S % tm == 0 and tm % (8 * r) == 0
    j0 = col0 // tn
    nj = ncols // tn
    if gains is None:
        gains = jnp.ones((nj, 1, tn), F32)
    strided = 1 < r < EINSHAPE_MIN_STRIDE
    scratch = [pltpu.VMEM((tn // LANES, tm, LANES), F32)] if strided else []
    kern = functools.partial(_proj_kernel, norm_width=norm_width, norm_tiles=norm_tiles, r=r)
    return pl.pallas_call(
        kern,
        grid=(nb, S // tm, nj),
        in_specs=[pl.BlockSpec((1, tm, K), lambda b, i, j: (b, i, 0)),
                  pl.BlockSpec((K, tn), lambda b, i, j: (0, j0 + j)),
                  pl.BlockSpec((1, 1, tn), lambda b, i, j: (j, 0, 0))],
        out_specs=pl.BlockSpec((1, r, tm // r, tn), lambda b, i, j: (b, 0, i, j)),
        out_shape=jax.ShapeDtypeStruct((nb, r, S // r, ncols), out_dtype),
        scratch_shapes=scratch,
        compiler_params=_cparams("parallel", "parallel", "arbitrary"),
        name="proj",
    )(h3, w, gains)


def _side_proj_kernel(h_ref, w_ref, g_ref, q_ref, dt_ref, *, head_dim):
    mw = q_ref.shape[2]
    for c in range(mw // head_dim):
        cols = slice(c * head_dim, (c + 1) * head_dim)
        q_ref[0, :, cols] = _rms(_dot(h_ref[0], w_ref[:, cols]), g_ref[...]).astype(q_ref.dtype)
    dt_ref[0] = _dot(h_ref[0], w_ref[:, mw:])


def side_proj(h3, w, q_gain, tm):
    nb, S, K = h3.shape
    head_dim = q_gain.shape[0]
    n = w.shape[1]
    mw = n - LANES
    return pl.pallas_call(
        functools.partial(_side_proj_kernel, head_dim=head_dim),
        grid=(nb, S // tm),
        in_specs=[pl.BlockSpec((1, tm, K), lambda b, i: (b, i, 0)),
                  pl.BlockSpec((K, n), lambda b, i: (0, 0)),
                  pl.BlockSpec((1, head_dim), lambda b, i: (0, 0))],
        out_specs=[pl.BlockSpec((1, tm, mw), lambda b, i: (b, i, 0)),
                   pl.BlockSpec((1, tm, LANES), lambda b, i: (b, i, 0))],
        out_shape=[jax.ShapeDtypeStruct((nb, S, mw), BF16), jax.ShapeDtypeStruct((nb, S, LANES), F32)],
        compiler_params=_cparams("parallel", "parallel"),
        name="side_proj",
    )(h3, w, q_gain.reshape(1, head_dim))


def _outproj_kernel(x_ref, a1_ref, a2_ref, w1_ref, w2_ref, o_ref):
    for c0 in range(0, o_ref.shape[1], PROJ_CHUNK):
        cols = slice(c0, c0 + PROJ_CHUNK)
        o_ref[:, cols] = x_ref[:, cols] + _dot(a1_ref[...], w1_ref[:, cols]) + _dot(a2_ref[...], w2_ref[:, cols])


def outproj(x2, a1, a2, w1, w2, tm, tn):
    T, D = x2.shape
    k1, k2 = a1.shape[1], a2.shape[1]
    return pl.pallas_call(
        _outproj_kernel,
        grid=(T // tm, D // tn),
        in_specs=[pl.BlockSpec((tm, tn), lambda i, j: (i, j)),
                  pl.BlockSpec((tm, k1), lambda i, j: (i, 0)),
                  pl.BlockSpec((tm, k2), lambda i, j: (i, 0)),
                  pl.BlockSpec((k1, tn), lambda i, j: (0, j)),
                  pl.BlockSpec((k2, tn), lambda i, j: (0, j))],
        out_specs=pl.BlockSpec((tm, tn), lambda i, j: (i, j)),
        out_shape=jax.ShapeDtypeStruct((T, D), F32),
        compiler_params=_cparams("parallel", "arbitrary"),
        name="outproj",
    )(x2, a1, a2, w1, w2)


def _memkv_kernel(mem_ref, g_ref, w_ref, kg_ref, k_ref, v_ref):
    mw = k_ref.shape[2]
    hd = kg_ref.shape[1]
    memn = _rms(mem_ref[0], g_ref[...]).astype(BF16)
    kv = _dot(memn, w_ref[...])
    for hh in range(mw // hd):
        cols = slice(hh * hd, (hh + 1) * hd)
        k_ref[0, :, cols] = _rms(kv[:, cols], kg_ref[...]).astype(BF16)
    v_ref[0] = kv[:, mw:].astype(BF16)


def mem_kv(mem, mem_gain, w_kv, k_gain):
    nb, M, D = mem.shape
    mw = w_kv.shape[1] // 2
    hd = k_gain.shape[0]
    out = jax.ShapeDtypeStruct((nb, M, mw), BF16)
    return pl.pallas_call(
        _memkv_kernel,
        grid=(nb,),
        in_specs=[pl.BlockSpec((1, M, D), lambda b: (b, 0, 0)),
                  pl.BlockSpec((1, D), lambda b: (0, 0)),
                  pl.BlockSpec((D, 2 * mw), lambda b: (0, 0)),
                  pl.BlockSpec((1, hd), lambda b: (0, 0))],
        out_specs=[pl.BlockSpec((1, M, mw), lambda b: (b, 0, 0))] * 2,
        out_shape=[out, out],
        compiler_params=_cparams("parallel"),
        name="mem_kv",
    )(mem, mem_gain.reshape(1, D), w_kv, k_gain.reshape(1, hd))


def _memattn_kernel(q_ref, k_ref, v_ref, o_ref, *, heads):
    hd = q_ref.shape[2] // heads
    scale = hd ** -0.5
    for hh in range(heads):
        cols = slice(hh * hd, (hh + 1) * hd)
        s = _dot_nt(q_ref[0, :, cols], k_ref[0, :, cols]) * scale
        e = jnp.exp(s - jnp.max(s, axis=-1, keepdims=True))
        p = e / jnp.sum(e, axis=-1, keepdims=True)
        o_ref[0, :, cols] = _dot(p.astype(BF16), v_ref[0, :, cols]).astype(o_ref.dtype)


def mem_attention(q, k, v, tq):
    nb, S, mw = q.shape
    M = k.shape[1]
    return pl.pallas_call(
        functools.partial(_memattn_kernel, heads=MEM_HEADS),
        grid=(nb, S // tq),
        in_specs=[pl.BlockSpec((1, tq, mw), lambda b, i: (b, i, 0)),
                  pl.BlockSpec((1, M, mw), lambda b, i: (b, 0, 0)),
                  pl.BlockSpec((1, M, mw), lambda b, i: (b, 0, 0))],
        out_specs=pl.BlockSpec((1, tq, mw), lambda b, i: (b, i, 0)),
        out_shape=jax.ShapeDtypeStruct((nb, S, mw), BF16),
        compiler_params=_cparams("parallel", "parallel"),
        name="mem_attn",
    )(q, k, v)


SSD_GROUP_WIDTH = SSD_HEADS_PER_GROUP * SSD_HEAD_DIM
SSD_GROUP_CONV = SSD_GROUP_WIDTH + 2 * SSD_STATE
SSD_GROUP_COLS = SSD_GROUP_CONV + SSD_GROUP_WIDTH
HALO_ROWS = 16


def _ssd_inproj_kernel(hp_ref, h_ref, hn_ref, w_ref, cw_ref, cb_ref, o_ref, lhs_ref, ext_ref):
    i = pl.program_id(1)
    tm = h_ref.shape[1]
    lo = HALO_ROWS - SSD_CONV // 2

    @pl.when(pl.program_id(2) == 0)
    def _():
        lhs_ref[0:HALO_ROWS] = jnp.where(i > 0, hp_ref[0], jnp.zeros_like(hp_ref[0]))
        lhs_ref[HALO_ROWS:HALO_ROWS + tm] = h_ref[0]
        lhs_ref[HALO_ROWS + tm:] = jnp.where(i < pl.num_programs(1) - 1, hn_ref[0], jnp.zeros_like(hn_ref[0]))

    for c0 in range(0, SSD_GROUP_COLS, PROJ_CHUNK):
        if c0 >= SSD_GROUP_CONV:
            o_ref[0, :, c0:c0 + PROJ_CHUNK] = jax.nn.silu(_dot(h_ref[0], w_ref[:, c0:c0 + PROJ_CHUNK]))
            continue
        half = (tm + 2 * HALO_ROWS) // 2
        for r0 in (0, half):
            res = _dot(lhs_ref[r0:r0 + half, :], w_ref[:, c0:c0 + PROJ_CHUNK])
            for c in range(c0, c0 + PROJ_CHUNK, LANES):
                ext_ref[c // LANES, r0:r0 + half] = res[:, c - c0:c - c0 + LANES]
        for c in range(c0, c0 + PROJ_CHUNK, LANES):
            s = c // LANES
            if c >= SSD_GROUP_CONV:
                o_ref[0, :, c:c + LANES] = jax.nn.silu(ext_ref[s, HALO_ROWS:HALO_ROWS + tm, :])
                continue
            acc = cb_ref[:, c:c + LANES] + cw_ref[0:1, c:c + LANES] * ext_ref[s, lo:lo + tm, :]
            for k in range(1, SSD_CONV):
                acc = acc + cw_ref[k:k + 1, c:c + LANES] * ext_ref[s, lo + k:lo + k + tm, :]
            o_ref[0, :, c:c + LANES] = jax.nn.silu(acc)


def ssd_inproj(h3, w, conv_w, conv_b, tm):
    nb, S, K = h3.shape
    G, gc, cols = SSD_GROUPS, SSD_GROUP_CONV, SSD_GROUP_COLS
    assert PROJ_CHUNK % LANES == 0 and gc % LANES == 0 and cols % PROJ_CHUNK == 0 and tm % HALO_ROWS == 0
    nh = tm // HALO_ROWS
    last = S // HALO_ROWS - 1
    return pl.pallas_call(
        _ssd_inproj_kernel,
        grid=(nb, S // tm, G),
        in_specs=[pl.BlockSpec((1, HALO_ROWS, K), lambda b, i, g: (b, jnp.maximum(i * nh - 1, 0), 0)),
                  pl.BlockSpec((1, tm, K), lambda b, i, g: (b, i, 0)),
                  pl.BlockSpec((1, HALO_ROWS, K), lambda b, i, g: (b, jnp.minimum((i + 1) * nh, last), 0)),
                  pl.BlockSpec((K, cols), lambda b, i, g: (0, g)),
                  pl.BlockSpec((SSD_CONV, gc), lambda b, i, g: (0, g)),
                  pl.BlockSpec((1, gc), lambda b, i, g: (0, g))],
        out_specs=pl.BlockSpec((1, tm, cols), lambda b, i, g: (b, i, g)),
        out_shape=jax.ShapeDtypeStruct((nb, S, G * cols), F32),
        scratch_shapes=[pltpu.VMEM((tm + 2 * HALO_ROWS, K), BF16),
                        pltpu.VMEM((pl.cdiv(gc, PROJ_CHUNK) * PROJ_CHUNK // LANES, tm + 2 * HALO_ROWS, LANES), F32)],
        compiler_params=_cparams("parallel", "parallel", "arbitrary"),
        name="ssd_inproj",
    )(h3, h3, h3, w, conv_w, conv_b.reshape(1, G * gc))


LOG2E = math.log2(math.e)


def _ssd_prep_kernel(dtr_ref, bias_ref, alog_ref, tr_ref):
    Q = SSD_CHUNK
    ii = lax.broadcasted_iota(jnp.int32, (Q, Q), 0)
    jj = lax.broadcasted_iota(jnp.int32, (Q, Q), 1)
    tril = (jj <= ii).astype(F32)
    triu = (jj >= ii).astype(F32)
    lane = lax.broadcasted_iota(jnp.int32, (Q, LANES), 1)
    is_bwd = (lane % GROUP_LANES) >= GROUP_LANES // 2
    hi = lax.Precision.HIGHEST
    for i in range(tr_ref.shape[1]):
        x = dtr_ref[0, i * Q:(i + 1) * Q, :] + bias_ref[...]
        dt = jnp.maximum(x, 0.0) + jnp.log1p(jnp.exp(-jnp.abs(x)))
        a = dt * (-jnp.exp(alog_ref[...]))
        cs_fwd = jnp.dot(tril, a, precision=hi, preferred_element_type=F32)
        cs_bwd = jnp.dot(triu, a, precision=hi, preferred_element_type=F32)
        cs = jnp.where(is_bwd, cs_bwd, cs_fwd)
        last = jnp.where(is_bwd[0:1], cs[0:1], cs[Q - 1:Q])
        cs2 = cs * LOG2E
        parts = (cs2, cs2 - jnp.log2(dt), jnp.exp(last - cs) * dt, jnp.broadcast_to(jnp.exp(last), (Q, LANES)))
        for p, part in enumerate(parts):
            rows = part.T
            for g in range(SSD_GROUPS):
                tr_ref[0, i, g, p * GROUP_LANES:(p + 1) * GROUP_LANES] = rows[g * GROUP_LANES:(g + 1) * GROUP_LANES]


def ssd_prep(dtr, bias_l, alog_l):
    nb, S, _ = dtr.shape
    Q = SSD_CHUNK
    nc = S // Q
    kp = next(k for k in (8, 4, 2, 1) if nc % k == 0)
    vec_spec = pl.BlockSpec((1, LANES), lambda b, c: (0, 0))
    return pl.pallas_call(
        _ssd_prep_kernel,
        grid=(nb, nc // kp),
        in_specs=[pl.BlockSpec((1, kp * Q, LANES), lambda b, c: (b, c, 0)), vec_spec, vec_spec],
        out_specs=pl.BlockSpec((1, kp, SSD_GROUPS, 4 * GROUP_LANES, Q), lambda b, c: (b, c, 0, 0, 0)),
        out_shape=jax.ShapeDtypeStruct((nb, nc, SSD_GROUPS, 4 * GROUP_LANES, Q), F32),
        compiler_params=_cparams("parallel", "parallel"),
        name="ssd_prep",
    )(dtr, bias_l, alog_l)


def _ssd_direction(d, g, sub, xc_ref, tr_ref, st_ref):
    Q = SSD_CHUNK
    P, N, gw = SSD_HEAD_DIM, SSD_STATE, SSD_GROUP_WIDTH
    rows = slice(sub * Q, (sub + 1) * Q)
    xs = xc_ref[0, rows, 0:gw]
    bm = xc_ref[0, rows, gw:gw + N]
    cm = xc_ref[0, rows, gw + N:gw + 2 * N]
    cb = _dot_nt(cm.astype(BF16), bm.astype(BF16))
    bt = bm.T

    cs2T, rowT, wT, cdT = (tr_ref[0, sub, 0, part * GROUP_LANES:(part + 1) * GROUP_LANES, :] for part in range(4))
    ii = lax.broadcasted_iota(jnp.int32, (Q, Q), 0)
    jj = lax.broadcasted_iota(jnp.int32, (Q, Q), 1)
    causal = (jj <= ii) if d == 0 else (jj >= ii)
    lo = lax.broadcasted_iota(jnp.int32, (Q, LANES), 1) < P
    st = st_ref[d]
    ys, sts = [], []
    for p in range(SSD_HEADS_PER_GROUP // 2):
        cols = slice(p * LANES, (p + 1) * LANES)
        ms, ss = [], []
        k0 = d * (GROUP_LANES // 2) + 2 * p
        for k in (k0, k0 + 1):
            col2 = jnp.broadcast_to(cs2T[k:k + 1, :], (Q, Q)).T
            decay_dt = jnp.exp2(jnp.where(causal, col2 - rowT[k:k + 1, :], -jnp.inf))
            ms.append((cb * decay_dt).astype(BF16))
            ms.append((cm * jnp.exp2(col2)).astype(BF16))
            ss.append((bt * wT[k:k + 1, :]).astype(BF16))
        cd = jnp.where(lo[0:1], cdT[k0:k0 + 1, :], cdT[k0 + 1:k0 + 2, :])
        xp, sp = xs[:, cols], st[:, cols]
        x_lo, x_hi = jnp.where(lo, xp, 0.0).astype(BF16), jnp.where(lo, 0.0, xp).astype(BF16)
        s_lo, s_hi = jnp.where(lo, sp, 0.0).astype(BF16), jnp.where(lo, 0.0, sp).astype(BF16)
        ys.append(_dot(jnp.concatenate(ms, axis=1), jnp.concatenate([x_lo, s_lo, x_hi, s_hi], axis=0)))
        sts.append(cd * sp + _dot(jnp.concatenate(ss, axis=1), jnp.concatenate([x_lo, x_hi], axis=0)))
    st_ref[d] = jnp.concatenate(sts, axis=1)
    return jnp.concatenate(ys, axis=1)


def _ssd_scan_kernel(xcf, xcb, trf, trb, dskip_ref, ng_ref, y_ref, st_ref, ysum_ref, *, kc):
    g = pl.program_id(1)
    c = pl.program_id(2)
    nsteps = pl.num_programs(2)
    Q = SSD_CHUNK

    @pl.when(c == 0)
    def _():
        st_ref[...] = jnp.zeros_like(st_ref)
        ysum_ref[...] = jnp.zeros_like(ysum_ref)

    def finish(y_dir, row0, xc_ref, sub):
        rows = pl.ds(pl.multiple_of(row0, Q), Q)
        blk = slice(sub * Q, (sub + 1) * Q)
        tot = ysum_ref[rows, :] + y_dir + dskip_ref[...] * xc_ref[0, blk, 0:SSD_GROUP_WIDTH]
        ysum_ref[rows, :] = y_dir
        gated = tot * xc_ref[0, blk, SSD_GROUP_CONV:SSD_GROUP_COLS]
        y_ref[0, rows, :] = _rms(gated, ng_ref[...]).astype(y_ref.dtype)

    for i in range(kc):
        y_f = _ssd_direction(0, g, i, xcf, trf, st_ref)
        finish(y_f, (c * kc + i) * Q, xcf, i)
        y_b = _ssd_direction(1, g, kc - 1 - i, xcb, trb, st_ref)
        finish(y_b, ((nsteps - 1 - c) * kc + kc - 1 - i) * Q, xcb, kc - 1 - i)


def ssd_scan(xc, tr, dskip, norm_g, kc):
    nb, S, _ = xc.shape
    Q = SSD_CHUNK
    nsteps = S // (Q * kc)
    assert nsteps % 2 == 0 and Q == LANES
    G, N, gw = SSD_GROUPS, SSD_STATE, SSD_GROUP_WIDTH
    inner = G * gw

    def both(shape, f):
        return [pl.BlockSpec(shape, lambda b, g, c: f(b, g, c)),
                pl.BlockSpec(shape, lambda b, g, c: f(b, g, nsteps - 1 - c))]

    in_specs = (both((1, kc * Q, SSD_GROUP_COLS), lambda b, g, c: (b, c, g))
                + both((1, kc, 1, 4 * GROUP_LANES, Q), lambda b, g, c: (b, c, g, 0, 0))
                + [pl.BlockSpec((1, gw), lambda b, g, c: (0, g)),
                   pl.BlockSpec((1, gw), lambda b, g, c: (0, g))])
    return pl.pallas_call(
        functools.partial(_ssd_scan_kernel, kc=kc),
        grid=(nb, G, nsteps),
        in_specs=in_specs,
        out_specs=pl.BlockSpec((1, S, gw), lambda b, g, c: (b, 0, g)),
        out_shape=jax.ShapeDtypeStruct((nb, S, inner), BF16),
        scratch_shapes=[pltpu.VMEM((2, N, gw), F32), pltpu.VMEM((S, gw), F32)],
        compiler_params=_cparams("parallel", "parallel", "arbitrary"),
        name="ssd_scan",
    )(xc, xc, tr, tr, dskip.reshape(1, inner), norm_g.reshape(1, inner))


def _t5_bucket_np(rel):
    half = REL_BUCKETS // 2
    exact = half // 2
    n = np.abs(rel)
    far = exact + (np.log(np.maximum(n, 1).astype(np.float32) / np.float32(exact))
                   / np.float32(math.log(REL_MAX_DISTANCE / exact)) * np.float32(half - exact)).astype(np.int32)
    far = np.minimum(far, half - 1)
    return np.where(rel > 0, half, 0) + np.where(n < exact, n, far)


def _dil_bucket_rows(win):
    d = np.arange(win)
    rel = d - DIL_HALF
    rows = []
    for _, dilation in DIL_CONFIGS:
        rows.append(np.where(d <= 2 * DIL_HALF, _t5_bucket_np(rel * dilation), -1))
    return np.broadcast_to(np.stack(rows)[:, None, :], (len(DIL_CONFIGS), 8, win)).astype(np.int32)


def _dil_kernel(tbl_ref, bm_ref, *refs, seq, seg, tq):
    ngroups = len(DIL_CONFIGS)
    win = tq + 2 * DIL_HALF
    in_refs = refs[:7 * ngroups]
    o_ref = refs[7 * ngroups]
    bias_ref, m_ref, l_ref, a_ref, kext_ref, vext_ref = refs[7 * ngroups + 1:]
    h = pl.program_id(1)
    t = pl.program_id(2)
    scale = DIL_HEAD_DIM ** -0.5

    @pl.when(t == 0)
    def _():
        m_ref[...] = jnp.full_like(m_ref, -jnp.inf)
        l_ref[...] = jnp.zeros_like(l_ref)
        a_ref[...] = jnp.zeros_like(a_ref)
        kcol = lax.broadcasted_iota(jnp.int32, (tq, win), 1)
        for gi in range(ngroups):
            ids = bm_ref[gi]
            row = jnp.zeros(ids.shape, F32)
            for u in range(REL_BUCKETS):
                row = jnp.where(ids == u, tbl_ref[u, gi * DIL_HEADS + h], row)
            row = jnp.where(ids < 0, -jnp.inf, row)
            band = pltpu.roll(jnp.broadcast_to(row[0:1], (tq, win)), 0, 1, stride=1, stride_axis=0)
            no_left = jnp.where(kcol >= DIL_HALF, band, -jnp.inf)
            bias_ref[gi, 0] = band
            bias_ref[gi, 1] = no_left
            bias_ref[gi, 2] = jnp.where(kcol < win - DIL_HALF, band, -jnp.inf)
            bias_ref[gi, 3] = jnp.where(kcol < win - DIL_HALF, no_left, -jnp.inf)

    for gi, (_, r) in enumerate(DIL_CONFIGS):
        q_ref, kp, kc, kn, vp, vc, vn = in_refs[7 * gi:7 * gi + 7]
        sub_len = seq // r
        piece = min(seg, sub_len)
        nblk = piece // tq
        for ext, prev, cur, nxt in ((kext_ref, kp, kc, kn), (vext_ref, vp, vc, vn)):
            ext[gi, 0:DIL_HALF] = prev[0]
            ext[gi, DIL_HALF:DIL_HALF + seg] = cur[0]
            ext[gi, DIL_HALF + seg:] = nxt[0]
        for p in range(seg // piece):
            first_row = t * seg + p * piece
            m_res = first_row // sub_len
            j0 = first_row % sub_len
            for jb in range(nblk):
                row0 = p * piece + jb * tq
                q = q_ref[0, row0:row0 + tq, :]
                kw = kext_ref[gi, row0:row0 + win, :]
                vw = vext_ref[gi, row0:row0 + win, :]
                variant = 0
                if jb == 0:
                    variant = variant + (j0 == 0).astype(jnp.int32)
                if jb == nblk - 1:
                    variant = variant + 2 * (j0 + piece == sub_len).astype(jnp.int32)
                s = _dot_nt(q, kw) * scale + bias_ref[gi, variant]
                mb = jnp.max(s, axis=-1, keepdims=True)
                e = jnp.exp(s - mb)
                lb = jnp.sum(e, axis=-1, keepdims=True)
                acc = _dot(e.astype(BF16), vw)
                start = (j0 + jb * tq) * r + m_res
                rows = pl.ds(start, tq, stride=r) if r > 1 else pl.ds(start, tq)
                m_old = m_ref[rows, :]
                m_new = jnp.maximum(m_old, mb)
                alpha = jnp.exp(m_old - m_new)
                beta = jnp.exp(mb - m_new)
                l_new = alpha * l_ref[rows, :] + beta * lb
                a_new = alpha * a_ref[rows, :] + beta * acc
                m_ref[rows, :] = m_new
                l_ref[rows, :] = l_new
                a_ref[rows, :] = a_new

    @pl.when(t == pl.num_programs(2) - 1)
    def _():
        o_ref[0] = (a_ref[...] / l_ref[...]).astype(o_ref.dtype)


def dilated_attention(qkvs, rel_bias):
    nb, S, _ = qkvs[0].shape
    hd = DIL_HEAD_DIM
    seg = S // DIL_SEGS
    tq = _tile(seg, DIL_QUERY_BLOCK)
    win = tq + 2 * DIL_HALF
    assert seg % tq == 0 and tq % LANES == 0
    for w, r in DIL_CONFIGS:
        piece = min(seg, S // r)
        assert w == 2 * DIL_HALF * r and (S // r) % piece == 0 and seg % piece == 0 and piece % tq == 0
    nhb = seg // DIL_HALF
    last_hb = S // DIL_HALF - 1
    in_specs = [pl.BlockSpec(memory_space=pltpu.SMEM),
                pl.BlockSpec((len(DIL_CONFIGS), 8, win), lambda b, h, t: (0, 0, 0))]
    args = [rel_bias, jnp.asarray(_dil_bucket_rows(win))]
    for g in range(len(DIL_CONFIGS)):
        in_specs.append(pl.BlockSpec((1, seg, hd), lambda b, h, t: (b, t, h)))
        for part in (1, 2):
            off = part * DIL_HEADS
            in_specs += [
                pl.BlockSpec((1, DIL_HALF, hd), lambda b, h, t, off=off: (b, jnp.maximum(t * nhb - 1, 0), off + h)),
                pl.BlockSpec((1, seg, hd), lambda b, h, t, off=off: (b, t, off + h)),
                pl.BlockSpec((1, DIL_HALF, hd), lambda b, h, t, off=off: (b, jnp.minimum((t + 1) * nhb, last_hb), off + h)),
            ]
        args += [qkvs[g]] * 7
    kern = functools.partial(_dil_kernel, seq=S, seg=seg, tq=tq)
    return pl.pallas_call(
        kern,
        grid=(nb, DIL_HEADS, DIL_SEGS),
        in_specs=in_specs,
        out_specs=pl.BlockSpec((1, S, hd), lambda b, h, t: (b, 0, h)),
        out_shape=jax.ShapeDtypeStruct((nb, S, DIL_HEADS * hd), BF16),
        scratch_shapes=[pltpu.VMEM((len(DIL_CONFIGS), 4, tq, win), F32),
                        pltpu.VMEM((S, hd), F32), pltpu.VMEM((S, hd), F32), pltpu.VMEM((S, hd), F32),
                        pltpu.VMEM((len(DIL_CONFIGS), seg + 2 * DIL_HALF, hd), BF16),
                        pltpu.VMEM((len(DIL_CONFIGS), seg + 2 * DIL_HALF, hd), BF16)],
        compiler_params=_cparams("parallel", "parallel", "arbitrary"),
        name="dil_attn",
    )(*args)


def _ssd_lane_layout(v):
    v = v.reshape(2, SSD_GROUPS, SSD_HEADS_PER_GROUP).transpose(1, 0, 2)
    v = jnp.pad(v, ((0, 0), (0, 0), (0, GROUP_LANES // 2 - SSD_HEADS_PER_GROUP)))
    return v.reshape(1, LANES)


def _ssd_mixer(x2, h3, kmem, vmem_, w_in, conv_w, conv_b, dt_bias, a_log, d_skip, norm_g, w_out, mem_q_gain):
    nb, S, D = h3.shape
    inner = SSD_GROUPS * SSD_HEADS_PER_GROUP * SSD_HEAD_DIM
    conv_ch = inner + 2 * SSD_GROUPS * SSD_STATE
    nheads = SSD_GROUPS * SSD_HEADS_PER_GROUP
    mw = kmem.shape[2]
    tm = _tile(S, ROW_TILE)
    w_bf = w_in.astype(BF16)

    def grouped(t, z=None):
        lead = t.shape[:-1]
        parts = [t[..., :inner].reshape(*lead, SSD_GROUPS, SSD_GROUP_WIDTH),
                 t[..., inner:inner + SSD_GROUPS * SSD_STATE].reshape(*lead, SSD_GROUPS, SSD_STATE),
                 t[..., inner + SSD_GROUPS * SSD_STATE:].reshape(*lead, SSD_GROUPS, SSD_STATE)]
        if z is not None:
            parts.append(z.reshape(*lead, SSD_GROUPS, SSD_GROUP_WIDTH))
        return jnp.concatenate(parts, axis=-1).reshape(*lead, -1)

    w_p1 = grouped(w_bf[:, inner:inner + conv_ch], z=w_bf[:, :inner])
    xc = ssd_inproj(h3, w_p1, grouped(conv_w), grouped(conv_b), tm)
    w_dt = w_in[:, inner + conv_ch:inner + conv_ch + 2 * nheads]
    w_dt = w_dt.reshape(D, 2, SSD_GROUPS, SSD_HEADS_PER_GROUP).transpose(0, 2, 1, 3)
    w_dt = jnp.pad(w_dt, ((0, 0), (0, 0), (0, 0), (0, GROUP_LANES // 2 - SSD_HEADS_PER_GROUP)))
    q_off = inner + conv_ch + 2 * nheads
    w_side = jnp.concatenate([w_bf[:, q_off:], w_dt.reshape(D, LANES).astype(BF16)], axis=1)
    q_mem, dtr = side_proj(h3, w_side, mem_q_gain, tm)

    tr = ssd_prep(dtr, _ssd_lane_layout(dt_bias), _ssd_lane_layout(a_log))
    dskip = jnp.repeat(d_skip, SSD_HEAD_DIM)
    nchunks = S // SSD_CHUNK
    kc = next(k for k in (8, 4, 2, 1) if nchunks % (2 * k) == 0)
    y = ssd_scan(xc, tr, dskip, norm_g, kc)
    o_mem = mem_attention(q_mem, kmem, vmem_, tq=tm)
    w_out_bf = w_out.astype(BF16)
    return outproj(x2, y.reshape(nb * S, inner), o_mem.reshape(nb * S, mw),
                   w_out_bf[:inner], w_out_bf[inner:], tm=_tile(nb * S, ROW_TILE), tn=_tile(D, OUT_TILE))


def _dil_mixer(x2, h3, kmem, vmem_, w_in, q_gain, k_gain, w_out, rel_bias, mem_q_gain):
    nb, S, D = h3.shape
    width = DIL_HEADS * DIL_HEAD_DIM
    mw = kmem.shape[2]
    tm = _tile(S, ROW_TILE)
    w_bf = w_in.astype(BF16)
    qkvs = []
    for g, (_, r) in enumerate(DIL_CONFIGS):
        gains = jnp.stack([jnp.tile(q_gain[g], DIL_HEADS), jnp.tile(k_gain[g], DIL_HEADS),
                           jnp.ones((width,), F32)]).reshape(3, 1, width)
        out = proj(h3, w_bf, g * 3 * width, 3 * width, gains, out_dtype=BF16, tm=tm, tn=width,
                   norm_width=DIL_HEAD_DIM, norm_tiles=2, r=r)
        qkvs.append(out.reshape(nb, S, 3 * width))
    q_off = len(DIL_CONFIGS) * 3 * width
    hd = mem_q_gain.shape[0]
    qg = jnp.tile(mem_q_gain, mw // hd).reshape(1, 1, mw)
    q_mem = proj(h3, w_bf, q_off, mw, qg, out_dtype=BF16, tm=tm, tn=mw, norm_width=hd)[:, 0]
    o = dilated_attention(qkvs, rel_bias)
    o_mem = mem_attention(q_mem, kmem, vmem_, tq=tm)
    w_out_bf = w_out.astype(BF16)
    return outproj(x2, o.reshape(nb * S, width), o_mem.reshape(nb * S, mw),
                   w_out_bf[:width], w_out_bf[width:], tm=_tile(nb * S, ROW_TILE), tn=_tile(D, OUT_TILE))


def kernel(x, mem, rel_bias, ffn_norm, ffn_w_in, ffn_w_out, mix_norm, mem_norm, mem_w_kv, mem_q_gain, mem_k_gain, ssd_w_in, ssd_conv_w, ssd_conv_b, ssd_dt_bias, ssd_A_log, ssd_D, ssd_norm, ssd_w_out, dil_w_in, dil_q_gain, dil_k_gain, dil_w_out):
    nb, S, D = x.shape
    depth = ffn_norm.shape[0]
    T = nb * S
    x2 = x.reshape(T, D)
    tm_ffn, tf = _tile(T, ROW_TILE), _tile(ffn_w_out.shape[2], FF_TILE)
    ffn_w_in = ffn_w_in.astype(BF16)
    ffn_w_out = ffn_w_out.astype(BF16)
    for i in range(depth):
        x2, h2 = ffn(x2, ffn_norm[i, 0], ffn_w_in, ffn_w_out, i, 0, tm_ffn, tf, next_gain=mix_norm[i])
        h3 = h2.reshape(nb, S, D)
        kmem, vmem_ = mem_kv(mem, mem_norm[i], mem_w_kv[i].astype(BF16), mem_k_gain[i])
        j = i // 2
        if i % 2 == 0:
            x2 = _ssd_mixer(x2, h3, kmem, vmem_, ssd_w_in[j], ssd_conv_w[j], ssd_conv_b[j], ssd_dt_bias[j],
                            ssd_A_log[j], ssd_D[j], ssd_norm[j], ssd_w_out[j], mem_q_gain[i])
        else:
            x2 = _dil_mixer(x2, h3, kmem, vmem_, dil_w_in[j], dil_q_gain[j], dil_k_gain[j], dil_w_out[j],
                            rel_bias, mem_q_gain[i])
        x2 = ffn(x2, ffn_norm[i, 1], ffn_w_in, ffn_w_out, i, 1, tm_ffn, tf)
    return x2.reshape(nb, S, D)
```

```python
import functools
import math

import jax
import jax.numpy as jnp
import numpy as np
from jax import lax
from jax.experimental import pallas as pl
from jax.experimental.pallas import tpu as pltpu

F32 = jnp.float32
BF16 = jnp.bfloat16
EPS = 1e-6

MEM_HEADS = 4
SSD_HEAD_DIM = 64
SSD_GROUPS = 8
SSD_HEADS_PER_GROUP = 6
SSD_STATE = 128
SSD_CONV = 5
SSD_CHUNK = 128
DIL_CONFIGS = ((128, 1), (512, 4), (2048, 16))
DIL_HEADS = 8
DIL_HEAD_DIM = 128
DIL_HALF = 64
DIL_SEGS = 4
DIL_QUERY_BLOCK = 128
REL_BUCKETS = 32
REL_MAX_DISTANCE = 1024

LANES = 128
VMEM_LIMIT_BYTES = 62 * 1024 * 1024
GROUP_LANES = 16

ROW_TILE = 1024
FF_TILE = 512
OUT_TILE = 1024


def _tile(n, pref):
    return pref if n % pref == 0 else n


def _cparams(*sem):
    return pltpu.CompilerParams(dimension_semantics=sem, vmem_limit_bytes=VMEM_LIMIT_BYTES)


def _rms(x, gain):
    ms = jnp.mean(x * x, axis=-1, keepdims=True)
    return x * lax.rsqrt(ms + EPS) * gain


def _dot(a, b):
    return jnp.dot(a, b, preferred_element_type=F32)


def _dot_nt(a, b):
    return lax.dot_general(a, b, (((1,), (1,)), ((), ())), preferred_element_type=F32)


FFN_SUB_ROWS = 512


def _ffn_kernel(x_ref, g_ref, wg_ref, wu_ref, wo_ref, *rest, emit_norm):
    if emit_norm:
        g2_ref, o_ref, hn_ref = rest
        h_ref = hn_ref
    else:
        o_ref, h_ref = rest
    j = pl.program_id(1)

    tm = x_ref.shape[0]
    sub = min(tm, FFN_SUB_ROWS)

    @pl.when(j == 0)
    def _():
        for r0 in range(0, tm, sub // 2):
            rows = slice(r0, r0 + sub // 2)
            h_ref[rows, :] = _rms(x_ref[rows, :], g_ref[...]).astype(BF16)

    def step(base_ref, norm_out=False):
        for r0 in range(0, tm, sub):
            h = h_ref[r0:r0 + sub, :]
            gate = _dot(h, wg_ref[...])
            up = _dot(h, wu_ref[...])
            a = (jax.nn.silu(gate) * up * 0.5).astype(BF16)
            o_ref[r0:r0 + sub, :] = base_ref[r0:r0 + sub, :] + _dot(a, wo_ref[...])
            if norm_out:
                for q0 in range(r0, r0 + sub, sub // 2):
                    rows = slice(q0, q0 + sub // 2)
                    hn_ref[rows, :] = _rms(o_ref[rows, :], g2_ref[...]).astype(hn_ref.dtype)

    last = pl.num_programs(1) - 1
    pl.when(j == 0)(lambda: step(x_ref))
    if emit_norm:
        pl.when((j > 0) & (j < last))(lambda: step(o_ref))
        pl.when(j == last)(lambda: step(o_ref, norm_out=True))
    else:
        pl.when(j > 0)(lambda: step(o_ref))


def ffn(x2, gain, w_in, w_out, layer, which, tm, tf, next_gain=None):
    T, D = x2.shape
    F = w_out.shape[2]
    nf = F // tf
    emit_norm = next_gain is not None
    assert nf > 1 or not emit_norm
    row_spec = pl.BlockSpec((tm, D), lambda i, j: (i, 0))
    vec_spec = pl.BlockSpec((1, D), lambda i, j: (0, 0))
    in_specs = [row_spec, vec_spec,
                pl.BlockSpec((None, None, D, tf), lambda i, j: (layer, which, 0, j)),
                pl.BlockSpec((None, None, D, tf), lambda i, j: (layer, which, 0, j + nf)),
                pl.BlockSpec((None, None, tf, D), lambda i, j: (layer, which, j, 0))]
    args = [x2, gain.reshape(1, D), w_in, w_in, w_out]
    out_specs, out_shape = row_spec, jax.ShapeDtypeStruct((T, D), F32)
    if emit_norm:
        in_specs.append(vec_spec)
        args.append(next_gain.reshape(1, D))
        out_specs, out_shape = [row_spec, row_spec], [out_shape, jax.ShapeDtypeStruct((T, D), BF16)]
    return pl.pallas_call(
        functools.partial(_ffn_kernel, emit_norm=emit_norm),
        grid=(T // tm, nf),
        in_specs=in_specs,
        out_specs=out_specs,
        out_shape=out_shape,
        scratch_shapes=[] if emit_norm else [pltpu.VMEM((tm, D), BF16)],
        compiler_params=_cparams("parallel", "arbitrary"),
        name="ffn",
    )(*args)


PROJ_CHUNK = 256
EINSHAPE_MIN_STRIDE = 16


def _proj_kernel(h_ref, w_ref, g_ref, o_ref, *scratch, norm_width, norm_tiles, r):
    tm, tn = h_ref.shape[1], w_ref.shape[1]
    rows_per = tm // r
    nw = norm_width if norm_width else PROJ_CHUNK
    assert PROJ_CHUNK % nw == 0 or nw % PROJ_CHUNK == 0
    chunk = max(PROJ_CHUNK, nw)

    def emit(normed):
        for c0 in range(0, tn, chunk):
            res = _dot(h_ref[0], w_ref[:, c0:c0 + chunk])
            if r == 1:
                if not normed:
                    o_ref[0, 0, :, c0:c0 + chunk] = res.astype(o_ref.dtype)
                    continue
                for c in range(c0, c0 + chunk, nw):
                    blk = res[:, c - c0:c - c0 + nw]
                    o_ref[0, 0, :, c:c + nw] = _rms(blk, g_ref[0, :, c:c + nw]).astype(o_ref.dtype)
                continue
            if r >= EINSHAPE_MIN_STRIDE:
                for c in range(c0, c0 + chunk, LANES):
                    blk = pltpu.einshape("(jm)l->mjl", res[:, c - c0:c - c0 + LANES], m=r)
                    if normed:
                        blk = _rms(blk, g_ref[0, :, c:c + LANES])
                    o_ref[0, :, :, c:c + LANES] = blk.astype(o_ref.dtype)
                continue
            acc_ref, = scratch
            for c in range(c0, c0 + chunk, LANES):
                acc_ref[c // LANES] = res[:, c - c0:c - c0 + LANES]
            for m in range(r):
                rows = pl.ds(m, rows_per, stride=r)
                for c in range(c0, c0 + chunk, LANES):
                    blk = acc_ref[c // LANES, rows, :]
                    if normed:
                        blk = _rms(blk, g_ref[0, :, c:c + LANES])
                    o_ref[0, m, :, c:c + LANES] = blk.astype(o_ref.dtype)

    if r > 1:
        assert norm_width in (0, LANES)
    if norm_width == 0:
        emit(False)
    elif norm_tiles is None:
        emit(True)
    else:
        j = pl.program_id(2)
        pl.when(j < norm_tiles)(lambda: emit(True))
        pl.when(j >= norm_tiles)(lambda: emit(False))


def proj(h3, w, col0, ncols, gains, *, out_dtype, tm, tn, norm_width=0, norm_tiles=None, r=1):
    nb, S, K = h3.shape
    assert col0 % tn == 0 and ncols % tn == 0 and S % tm == 0 and tm % (8 * r) == 0
    j0 = col0 // tn
    nj = ncols // tn
    if gains is None:
        gains = jnp.ones((nj, 1, tn), F32)
    strided = 1 < r < EINSHAPE_MIN_STRIDE
    scratch = [pltpu.VMEM((tn // LANES, tm, LANES), F32)] if strided else []
    kern = functools.partial(_proj_kernel, norm_width=norm_width, norm_tiles=norm_tiles, r=r)
    return pl.pallas_call(
        kern,
        grid=(nb, S // tm, nj),
        in_specs=[pl.BlockSpec((1, tm, K), lambda b, i, j: (b, i, 0)),
                  pl.BlockSpec((K, tn), lambda b, i, j: (0, j0 + j)),
                  pl.BlockSpec((1, 1, tn), lambda b, i, j: (j, 0, 0))],
        out_specs=pl.BlockSpec((1, r, tm // r, tn), lambda b, i, j: (b, 0, i, j)),
        out_shape=jax.ShapeDtypeStruct((nb, r, S // r, ncols), out_dtype),
        scratch_shapes=scratch,
        compiler_params=_cparams("parallel", "parallel", "arbitrary"),
        name="proj",
    )(h3, w, gains)


def _side_proj_kernel(h_ref, w_ref, g_ref, q_ref, dt_ref, *, head_dim):
    mw = q_ref.shape[2]
    for c in range(mw // head_dim):
        cols = slice(c * head_dim, (c + 1) * head_dim)
        q_ref[0, :, cols] = _rms(_dot(h_ref[0], w_ref[:, cols]), g_ref[...]).astype(q_ref.dtype)
    dt_ref[0] = _dot(h_ref[0], w_ref[:, mw:])


def side_proj(h3, w, q_gain, tm):
    nb, S, K = h3.shape
    head_dim = q_gain.shape[0]
    n = w.shape[1]
    mw = n - LANES
    return pl.pallas_call(
        functools.partial(_side_proj_kernel, head_dim=head_dim),
        grid=(nb, S // tm),
        in_specs=[pl.BlockSpec((1, tm, K), lambda b, i: (b, i, 0)),
                  pl.BlockSpec((K, n), lambda b, i: (0, 0)),
                  pl.BlockSpec((1, head_dim), lambda b, i: (0, 0))],
        out_specs=[pl.BlockSpec((1, tm, mw), lambda b, i: (b, i, 0)),
                   pl.BlockSpec((1, tm, LANES), lambda b, i: (b, i, 0))],
        out_shape=[jax.ShapeDtypeStruct((nb, S, mw), BF16), jax.ShapeDtypeStruct((nb, S, LANES), F32)],
        compiler_params=_cparams("parallel", "parallel"),
        name="side_proj",
    )(h3, w, q_gain.reshape(1, head_dim))


def _outproj_kernel(x_ref, a1_ref, a2_ref, w1_ref, w2_ref, o_ref):
    for c0 in range(0, o_ref.shape[1], PROJ_CHUNK):
        cols = slice(c0, c0 + PROJ_CHUNK)
        o_ref[:, cols] = x_ref[:, cols] + _dot(a1_ref[...], w1_ref[:, cols]) + _dot(a2_ref[...], w2_ref[:, cols])


def outproj(x2, a1, a2, w1, w2, tm, tn):
    T, D = x2.shape
    k1, k2 = a1.shape[1], a2.shape[1]
    return pl.pallas_call(
        _outproj_kernel,
        grid=(T // tm, D // tn),
        in_specs=[pl.BlockSpec((tm, tn), lambda i, j: (i, j)),
                  pl.BlockSpec((tm, k1), lambda i, j: (i, 0)),
                  pl.BlockSpec((tm, k2), lambda i, j: (i, 0)),
                  pl.BlockSpec((k1, tn), lambda i, j: (0, j)),
                  pl.BlockSpec((k2, tn), lambda i, j: (0, j))],
        out_specs=pl.BlockSpec((tm, tn), lambda i, j: (i, j)),
        out_shape=jax.ShapeDtypeStruct((T, D), F32),
        compiler_params=_cparams("parallel", "arbitrary"),
        name="outproj",
    )(x2, a1, a2, w1, w2)


def _memkv_kernel(mem_ref, g_ref, w_ref, kg_ref, k_ref, v_ref):
    mw = k_ref.shape[2]
    hd = kg_ref.shape[1]
    memn = _rms(mem_ref[0], g_ref[...]).astype(BF16)
    kv = _dot(memn, w_ref[...])
    for hh in range(mw // hd):
        cols = slice(hh * hd, (hh + 1) * hd)
        k_ref[0, :, cols] = _rms(kv[:, cols], kg_ref[...]).astype(BF16)
    v_ref[0] = kv[:, mw:].astype(BF16)


def mem_kv(mem, mem_gain, w_kv, k_gain):
    nb, M, D = mem.shape
    mw = w_kv.shape[1] // 2
    hd = k_gain.shape[0]
    out = jax.ShapeDtypeStruct((nb, M, mw), BF16)
    return pl.pallas_call(
        _memkv_kernel,
        grid=(nb,),
        in_specs=[pl.BlockSpec((1, M, D), lambda b: (b, 0, 0)),
                  pl.BlockSpec((1, D), lambda b: (0, 0)),
                  pl.BlockSpec((D, 2 * mw), lambda b: (0, 0)),
                  pl.BlockSpec((1, hd), lambda b: (0, 0))],
        out_specs=[pl.BlockSpec((1, M, mw), lambda b: (b, 0, 0))] * 2,
        out_shape=[out, out],
        compiler_params=_cparams("parallel"),
        name="mem_kv",
    )(mem, mem_gain.reshape(1, D), w_kv, k_gain.reshape(1, hd))


def _memattn_kernel(q_ref, k_ref, v_ref, o_ref, *, heads):
    hd = q_ref.shape[2] // heads
    scale = hd ** -0.5
    for hh in range(heads):
        cols = slice(hh * hd, (hh + 1) * hd)
        s = _dot_nt(q_ref[0, :, cols], k_ref[0, :, cols]) * scale
        e = jnp.exp(s - jnp.max(s, axis=-1, keepdims=True))
        p = e / jnp.sum(e, axis=-1, keepdims=True)
        o_ref[0, :, cols] = _dot(p.astype(BF16), v_ref[0, :, cols]).astype(o_ref.dtype)


def mem_attention(q, k, v, tq):
    nb, S, mw = q.shape
    M = k.shape[1]
    return pl.pallas_call(
        functools.partial(_memattn_kernel, heads=MEM_HEADS),
        grid=(nb, S // tq),
        in_specs=[pl.BlockSpec((1, tq, mw), lambda b, i: (b, i, 0)),
                  pl.BlockSpec((1, M, mw), lambda b, i: (b, 0, 0)),
                  pl.BlockSpec((1, M, mw), lambda b, i: (b, 0, 0))],
        out_specs=pl.BlockSpec((1, tq, mw), lambda b, i: (b, i, 0)),
        out_shape=jax.ShapeDtypeStruct((nb, S, mw), BF16),
        compiler_params=_cparams("parallel", "parallel"),
        name="mem_attn",
    )(q, k, v)


SSD_GROUP_WIDTH = SSD_HEADS_PER_GROUP * SSD_HEAD_DIM
SSD_GROUP_CONV = SSD_GROUP_WIDTH + 2 * SSD_STATE
SSD_GROUP_COLS = SSD_GROUP_CONV + SSD_GROUP_WIDTH
HALO_ROWS = 16


def _ssd_inproj_kernel(hp_ref, h_ref, hn_ref, w_ref, cw_ref, cb_ref, o_ref, lhs_ref, ext_ref):
    i = pl.program_id(1)
    tm = h_ref.shape[1]
    lo = HALO_ROWS - SSD_CONV // 2

    @pl.when(pl.program_id(2) == 0)
    def _():
        lhs_ref[0:HALO_ROWS] = jnp.where(i > 0, hp_ref[0], jnp.zeros_like(hp_ref[0]))
        lhs_ref[HALO_ROWS:HALO_ROWS + tm] = h_ref[0]
        lhs_ref[HALO_ROWS + tm:] = jnp.where(i < pl.num_programs(1) - 1, hn_ref[0], jnp.zeros_like(hn_ref[0]))

    for c0 in range(0, SSD_GROUP_COLS, PROJ_CHUNK):
        if c0 >= SSD_GROUP_CONV:
            o_ref[0, :, c0:c0 + PROJ_CHUNK] = jax.nn.silu(_dot(h_ref[0], w_ref[:, c0:c0 + PROJ_CHUNK]))
            continue
        half = (tm + 2 * HALO_ROWS) // 2
        for r0 in (0, half):
            res = _dot(lhs_ref[r0:r0 + half, :], w_ref[:, c0:c0 + PROJ_CHUNK])
            for c in range(c0, c0 + PROJ_CHUNK, LANES):
                ext_ref[c // LANES, r0:r0 + half] = res[:, c - c0:c - c0 + LANES]
        for c in range(c0, c0 + PROJ_CHUNK, LANES):
            s = c // LANES
            if c >= SSD_GROUP_CONV:
                o_ref[0, :, c:c + LANES] = jax.nn.silu(ext_ref[s, HALO_ROWS:HALO_ROWS + tm, :])
                continue
            acc = cb_ref[:, c:c + LANES] + cw_ref[0:1, c:c + LANES] * ext_ref[s, lo:lo + tm, :]
            for k in range(1, SSD_CONV):
                acc = acc + cw_ref[k:k + 1, c:c + LANES] * ext_ref[s, lo + k:lo + k + tm, :]
            o_ref[0, :, c:c + LANES] = jax.nn.silu(acc)


def ssd_inproj(h3, w, conv_w, conv_b, tm):
    nb, S, K = h3.shape
    G, gc, cols = SSD_GROUPS, SSD_GROUP_CONV, SSD_GROUP_COLS
    assert PROJ_CHUNK % LANES == 0 and gc % LANES == 0 and cols % PROJ_CHUNK == 0 and tm % HALO_ROWS == 0
    nh = tm // HALO_ROWS
    last = S // HALO_ROWS - 1
    return pl.pallas_call(
        _ssd_inproj_kernel,
        grid=(nb, S // tm, G),
        in_specs=[pl.BlockSpec((1, HALO_ROWS, K), lambda b, i, g: (b, jnp.maximum(i * nh - 1, 0), 0)),
                  pl.BlockSpec((1, tm, K), lambda b, i, g: (b, i, 0)),
                  pl.BlockSpec((1, HALO_ROWS, K), lambda b, i, g: (b, jnp.minimum((i + 1) * nh, last), 0)),
                  pl.BlockSpec((K, cols), lambda b, i, g: (0, g)),
                  pl.BlockSpec((SSD_CONV, gc), lambda b, i, g: (0, g)),
                  pl.BlockSpec((1, gc), lambda b, i, g: (0, g))],
        out_specs=pl.BlockSpec((1, tm, cols), lambda b, i, g: (b, i, g)),
        out_shape=jax.ShapeDtypeStruct((nb, S, G * cols), F32),
        scratch_shapes=[pltpu.VMEM((tm + 2 * HALO_ROWS, K), BF16),
                        pltpu.VMEM((pl.cdiv(gc, PROJ_CHUNK) * PROJ_CHUNK // LANES, tm + 2 * HALO_ROWS, LANES), F32)],
        compiler_params=_cparams("parallel", "parallel", "arbitrary"),
        name="ssd_inproj",
    )(h3, h3, h3, w, conv_w, conv_b.reshape(1, G * gc))


LOG2E = math.log2(math.e)


def _ssd_prep_kernel(dtr_ref, bias_ref, alog_ref, tr_ref):
    Q = SSD_CHUNK
    ii = lax.broadcasted_iota(jnp.int32, (Q, Q), 0)
    jj = lax.broadcasted_iota(jnp.int32, (Q, Q), 1)
    tril = (jj <= ii).astype(F32)
    triu = (jj >= ii).astype(F32)
    lane = lax.broadcasted_iota(jnp.int32, (Q, LANES), 1)
    is_bwd = (lane % GROUP_LANES) >= GROUP_LANES // 2
    hi = lax.Precision.HIGHEST
    for i in range(tr_ref.shape[1]):
        x = dtr_ref[0, i * Q:(i + 1) * Q, :] + bias_ref[...]
        dt = jnp.maximum(x, 0.0) + jnp.log1p(jnp.exp(-jnp.abs(x)))
        a = dt * (-jnp.exp(alog_ref[...]))
        cs_fwd = jnp.dot(tril, a, precision=hi, preferred_element_type=F32)
        cs_bwd = jnp.dot(triu, a, precision=hi, preferred_element_type=F32)
        cs = jnp.where(is_bwd, cs_bwd, cs_fwd)
        last = jnp.where(is_bwd[0:1], cs[0:1], cs[Q - 1:Q])
        cs2 = cs * LOG2E
        parts = (cs2, cs2 - jnp.log2(dt), jnp.exp(last - cs) * dt, jnp.broadcast_to(jnp.exp(last), (Q, LANES)))
        for p, part in enumerate(parts):
            rows = part.T
            for g in range(SSD_GROUPS):
                tr_ref[0, i, g, p * GROUP_LANES:(p + 1) * GROUP_LANES] = rows[g * GROUP_LANES:(g + 1) * GROUP_LANES]


def ssd_prep(dtr, bias_l, alog_l):
    nb, S, _ = dtr.shape
    Q = SSD_CHUNK
    nc = S // Q
    kp = next(k for k in (8, 4, 2, 1) if nc % k == 0)
    vec_spec = pl.BlockSpec((1, LANES), lambda b, c: (0, 0))
    return pl.pallas_call(
        _ssd_prep_kernel,
        grid=(nb, nc // kp),
        in_specs=[pl.BlockSpec((1, kp * Q, LANES), lambda b, c: (b, c, 0)), vec_spec, vec_spec],
        out_specs=pl.BlockSpec((1, kp, SSD_GROUPS, 4 * GROUP_LANES, Q), lambda b, c: (b, c, 0, 0, 0)),
        out_shape=jax.ShapeDtypeStruct((nb, nc, SSD_GROUPS, 4 * GROUP_LANES, Q), F32),
        compiler_params=_cparams("parallel", "parallel"),
        name="ssd_prep",
    )(dtr, bias_l, alog_l)


def _ssd_direction(d, g, sub, xc_ref, tr_ref, st_ref):
    Q = SSD_CHUNK
    P, N, gw = SSD_HEAD_DIM, SSD_STATE, SSD_GROUP_WIDTH
    rows = slice(sub * Q, (sub + 1) * Q)
    xs = xc_ref[0, rows, 0:gw]
    bm = xc_ref[0, rows, gw:gw + N]
    cm = xc_ref[0, rows, gw + N:gw + 2 * N]
    cb = _dot_nt(cm.astype(BF16), bm.astype(BF16))
    bt = bm.T

    cs2T, rowT, wT, cdT = (tr_ref[0, sub, 0, part * GROUP_LANES:(part + 1) * GROUP_LANES, :] for part in range(4))
    ii = lax.broadcasted_iota(jnp.int32, (Q, Q), 0)
    jj = lax.broadcasted_iota(jnp.int32, (Q, Q), 1)
    causal = (jj <= ii) if d == 0 else (jj >= ii)
    lo = lax.broadcasted_iota(jnp.int32, (Q, LANES), 1) < P
    st = st_ref[d]
    ys, sts = [], []
    for p in range(SSD_HEADS_PER_GROUP // 2):
        cols = slice(p * LANES, (p + 1) * LANES)
        ms, ss = [], []
        k0 = d * (GROUP_LANES // 2) + 2 * p
        for k in (k0, k0 + 1):
            col2 = jnp.broadcast_to(cs2T[k:k + 1, :], (Q, Q)).T
            decay_dt = jnp.exp2(jnp.where(causal, col2 - rowT[k:k + 1, :], -jnp.inf))
            ms.append((cb * decay_dt).astype(BF16))
            ms.append((cm * jnp.exp2(col2)).astype(BF16))
            ss.append((bt * wT[k:k + 1, :]).astype(BF16))
        cd = jnp.where(lo[0:1], cdT[k0:k0 + 1, :], cdT[k0 + 1:k0 + 2, :])
        xp, sp = xs[:, cols], st[:, cols]
        x_lo, x_hi = jnp.where(lo, xp, 0.0).astype(BF16), jnp.where(lo, 0.0, xp).astype(BF16)
        s_lo, s_hi = jnp.where(lo, sp, 0.0).astype(BF16), jnp.where(lo, 0.0, sp).astype(BF16)
        ys.append(_dot(jnp.concatenate(ms, axis=1), jnp.concatenate([x_lo, s_lo, x_hi, s_hi], axis=0)))
        sts.append(cd * sp + _dot(jnp.concatenate(ss, axis=1), jnp.concatenate([x_lo, x_hi], axis=0)))
    st_ref[d] = jnp.concatenate(sts, axis=1)
    return jnp.concatenate(ys, axis=1)


def _ssd_scan_kernel(xcf, xcb, trf, trb, dskip_ref, ng_ref, y_ref, st_ref, ysum_ref, *, kc):
    g = pl.program_id(1)
    c = pl.program_id(2)
    nsteps = pl.num_programs(2)
    Q = SSD_CHUNK

    @pl.when(c == 0)
    def _():
        st_ref[...] = jnp.zeros_like(st_ref)
        ysum_ref[...] = jnp.zeros_like(ysum_ref)

    def finish(y_dir, row0, xc_ref, sub):
        rows = pl.ds(pl.multiple_of(row0, Q), Q)
        blk = slice(sub * Q, (sub + 1) * Q)
        tot = ysum_ref[rows, :] + y_dir + dskip_ref[...] * xc_ref[0, blk, 0:SSD_GROUP_WIDTH]
        ysum_ref[rows, :] = y_dir
        gated = tot * xc_ref[0, blk, SSD_GROUP_CONV:SSD_GROUP_COLS]
        y_ref[0, rows, :] = _rms(gated, ng_ref[...]).astype(y_ref.dtype)

    for i in range(kc):
        y_f = _ssd_direction(0, g, i, xcf, trf, st_ref)
        finish(y_f, (c * kc + i) * Q, xcf, i)
        y_b = _ssd_direction(1, g, kc - 1 - i, xcb, trb, st_ref)
        finish(y_b, ((nsteps - 1 - c) * kc + kc - 1 - i) * Q, xcb, kc - 1 - i)


def ssd_scan(xc, tr, dskip, norm_g, kc):
    nb, S, _ = xc.shape
    Q = SSD_CHUNK
    nsteps = S // (Q * kc)
    assert nsteps % 2 == 0 and Q == LANES
    G, N, gw = SSD_GROUPS, SSD_STATE, SSD_GROUP_WIDTH
    inner = G * gw

    def both(shape, f):
        return [pl.BlockSpec(shape, lambda b, g, c: f(b, g, c)),
                pl.BlockSpec(shape, lambda b, g, c: f(b, g, nsteps - 1 - c))]

    in_specs = (both((1, kc * Q, SSD_GROUP_COLS), lambda b, g, c: (b, c, g))
                + both((1, kc, 1, 4 * GROUP_LANES, Q), lambda b, g, c: (b, c, g, 0, 0))
                + [pl.BlockSpec((1, gw), lambda b, g, c: (0, g)),
                   pl.BlockSpec((1, gw), lambda b, g, c: (0, g))])
    return pl.pallas_call(
        functools.partial(_ssd_scan_kernel, kc=kc),
        grid=(nb, G, nsteps),
        in_specs=in_specs,
        out_specs=pl.BlockSpec((1, S, gw), lambda b, g, c: (b, 0, g)),
        out_shape=jax.ShapeDtypeStruct((nb, S, inner), BF16),
        scratch_shapes=[pltpu.VMEM((2, N, gw), F32), pltpu.VMEM((S, gw), F32)],
        compiler_params=_cparams("parallel", "parallel", "arbitrary"),
        name="ssd_scan",
    )(xc, xc, tr, tr, dskip.reshape(1, inner), norm_g.reshape(1, inner))


def _t5_bucket_np(rel):
    half = REL_BUCKETS // 2
    exact = half // 2
    n = np.abs(rel)
    far = exact + (np.log(np.maximum(n, 1).astype(np.float32) / np.float32(exact))
                   / np.float32(math.log(REL_MAX_DISTANCE / exact)) * np.float32(half - exact)).astype(np.int32)
    far = np.minimum(far, half - 1)
    return np.where(rel > 0, half, 0) + np.where(n < exact, n, far)


def _dil_bucket_rows(win):
    d = np.arange(win)
    rel = d - DIL_HALF
    rows = []
    for _, dilation in DIL_CONFIGS:
        rows.append(np.where(d <= 2 * DIL_HALF, _t5_bucket_np(rel * dilation), -1))
    return np.broadcast_to(np.stack(rows)[:, None, :], (len(DIL_CONFIGS), 8, win)).astype(np.int32)


def _dil_kernel(tbl_ref, bm_ref, *refs, seq, seg, tq):
    ngroups = len(DIL_CONFIGS)
    win = tq + 2 * DIL_HALF
    in_refs = refs[:7 * ngroups]
    o_ref = refs[7 * ngroups]
    bias_ref, m_ref, l_ref, a_ref, kext_ref, vext_ref = refs[7 * ngroups + 1:]
    h = pl.program_id(1)
    t = pl.program_id(2)
    scale = DIL_HEAD_DIM ** -0.5

    @pl.when(t == 0)
    def _():
        m_ref[...] = jnp.full_like(m_ref, -jnp.inf)
        l_ref[...] = jnp.zeros_like(l_ref)
        a_ref[...] = jnp.zeros_like(a_ref)
        kcol = lax.broadcasted_iota(jnp.int32, (tq, win), 1)
        for gi in range(ngroups):
            ids = bm_ref[gi]
            row = jnp.zeros(ids.shape, F32)
            for u in range(REL_BUCKETS):
                row = jnp.where(ids == u, tbl_ref[u, gi * DIL_HEADS + h], row)
            row = jnp.where(ids < 0, -jnp.inf, row)
            band = pltpu.roll(jnp.broadcast_to(row[0:1], (tq, win)), 0, 1, stride=1, stride_axis=0)
            no_left = jnp.where(kcol >= DIL_HALF, band, -jnp.inf)
            bias_ref[gi, 0] = band
            bias_ref[gi, 1] = no_left
            bias_ref[gi, 2] = jnp.where(kcol < win - DIL_HALF, band, -jnp.inf)
            bias_ref[gi, 3] = jnp.where(kcol < win - DIL_HALF, no_left, -jnp.inf)

    for gi, (_, r) in enumerate(DIL_CONFIGS):
        q_ref, kp, kc, kn, vp, vc, vn = in_refs[7 * gi:7 * gi + 7]
        sub_len = seq // r
        piece = min(seg, sub_len)
        nblk = piece // tq
        for ext, prev, cur, nxt in ((kext_ref, kp, kc, kn), (vext_ref, vp, vc, vn)):
            ext[gi, 0:DIL_HALF] = prev[0]
            ext[gi, DIL_HALF:DIL_HALF + seg] = cur[0]
            ext[gi, DIL_HALF + seg:] = nxt[0]
        for p in range(seg // piece):
            first_row = t * seg + p * piece
            m_res = first_row // sub_len
            j0 = first_row % sub_len
            for jb in range(nblk):
                row0 = p * piece + jb * tq
                q = q_ref[0, row0:row0 + tq, :]
                kw = kext_ref[gi, row0:row0 + win, :]
                vw = vext_ref[gi, row0:row0 + win, :]
                variant = 0
                if jb == 0:
                    variant = variant + (j0 == 0).astype(jnp.int32)
                if jb == nblk - 1:
                    variant = variant + 2 * (j0 + piece == sub_len).astype(jnp.int32)
                s = _dot_nt(q, kw) * scale + bias_ref[gi, variant]
                mb = jnp.max(s, axis=-1, keepdims=True)
                e = jnp.exp(s - mb)
                lb = jnp.sum(e, axis=-1, keepdims=True)
                acc = _dot(e.astype(BF16), vw)
                start = (j0 + jb * tq) * r + m_res
                rows = pl.ds(start, tq, stride=r) if r > 1 else pl.ds(start, tq)
                m_old = m_ref[rows, :]
                m_new = jnp.maximum(m_old, mb)
                alpha = jnp.exp(m_old - m_new)
                beta = jnp.exp(mb - m_new)
                l_new = alpha * l_ref[rows, :] + beta * lb
                a_new = alpha * a_ref[rows, :] + beta * acc
                m_ref[rows, :] = m_new
                l_ref[rows, :] = l_new
                a_ref[rows, :] = a_new

    @pl.when(t == pl.num_programs(2) - 1)
    def _():
        o_ref[0] = (a_ref[...] / l_ref[...]).astype(o_ref.dtype)


def dilated_attention(qkvs, rel_bias):
    nb, S, _ = qkvs[0].shape
    hd = DIL_HEAD_DIM
    seg = S // DIL_SEGS
    tq = _tile(seg, DIL_QUERY_BLOCK)
    win = tq + 2 * DIL_HALF
    assert seg % tq == 0 and tq % LANES == 0
    for w, r in DIL_CONFIGS:
        piece = min(seg, S // r)
        assert w == 2 * DIL_HALF * r and (S // r) % piece == 0 and seg % piece == 0 and piece % tq == 0
    nhb = seg // DIL_HALF
    last_hb = S // DIL_HALF - 1
    in_specs = [pl.BlockSpec(memory_space=pltpu.SMEM),
                pl.BlockSpec((len(DIL_CONFIGS), 8, win), lambda b, h, t: (0, 0, 0))]
    args = [rel_bias, jnp.asarray(_dil_bucket_rows(win))]
    for g in range(len(DIL_CONFIGS)):
        in_specs.append(pl.BlockSpec((1, seg, hd), lambda b, h, t: (b, t, h)))
        for part in (1, 2):
            off = part * DIL_HEADS
            in_specs += [
                pl.BlockSpec((1, DIL_HALF, hd), lambda b, h, t, off=off: (b, jnp.maximum(t * nhb - 1, 0), off + h)),
                pl.BlockSpec((1, seg, hd), lambda b, h, t, off=off: (b, t, off + h)),
                pl.BlockSpec((1, DIL_HALF, hd), lambda b, h, t, off=off: (b, jnp.minimum((t + 1) * nhb, last_hb), off + h)),
            ]
        args += [qkvs[g]] * 7
    kern = functools.partial(_dil_kernel, seq=S, seg=seg, tq=tq)
    return pl.pallas_call(
        kern,
        grid=(nb, DIL_HEADS, DIL_SEGS),
        in_specs=in_specs,
        out_specs=pl.BlockSpec((1, S, hd), lambda b, h, t: (b, 0, h)),
        out_shape=jax.ShapeDtypeStruct((nb, S, DIL_HEADS * hd), BF16),
        scratch_shapes=[pltpu.VMEM((len(DIL_CONFIGS), 4, tq, win), F32),
                        pltpu.VMEM((S, hd), F32), pltpu.VMEM((S, hd), F32), pltpu.VMEM((S, hd), F32),
                        pltpu.VMEM((len(DIL_CONFIGS), seg + 2 * DIL_HALF, hd), BF16),
                        pltpu.VMEM((len(DIL_CONFIGS), seg + 2 * DIL_HALF, hd), BF16)],
        compiler_params=_cparams("parallel", "parallel", "arbitrary"),
        name="dil_attn",
    )(*args)


def _ssd_lane_layout(v):
    v = v.reshape(2, SSD_GROUPS, SSD_HEADS_PER_GROUP).transpose(1, 0, 2)
    v = jnp.pad(v, ((0, 0), (0, 0), (0, GROUP_LANES // 2 - SSD_HEADS_PER_GROUP)))
    return v.reshape(1, LANES)


def _ssd_mixer(x2, h3, kmem, vmem_, w_in, conv_w, conv_b, dt_bias, a_log, d_skip, norm_g, w_out, mem_q_gain):
    nb, S, D = h3.shape
    inner = SSD_GROUPS * SSD_HEADS_PER_GROUP * SSD_HEAD_DIM
    conv_ch = inner + 2 * SSD_GROUPS * SSD_STATE
    nheads = SSD_GROUPS * SSD_HEADS_PER_GROUP
    mw = kmem.shape[2]
    tm = _tile(S, ROW_TILE)
    w_bf = w_in.astype(BF16)

    def grouped(t, z=None):
        lead = t.shape[:-1]
        parts = [t[..., :inner].reshape(*lead, SSD_GROUPS, SSD_GROUP_WIDTH),
                 t[..., inner:inner + SSD_GROUPS * SSD_STATE].reshape(*lead, SSD_GROUPS, SSD_STATE),
                 t[..., inner + SSD_GROUPS * SSD_STATE:].reshape(*lead, SSD_GROUPS, SSD_STATE)]
        if z is not None:
            parts.append(z.reshape(*lead, SSD_GROUPS, SSD_GROUP_WIDTH))
        return jnp.concatenate(parts, axis=-1).reshape(*lead, -1)

    w_p1 = grouped(w_bf[:, inner:inner + conv_ch], z=w_bf[:, :inner])
    xc = ssd_inproj(h3, w_p1, grouped(conv_w), grouped(conv_b), tm)
    w_dt = w_in[:, inner + conv_ch:inner + conv_ch + 2 * nheads]
    w_dt = w_dt.reshape(D, 2, SSD_GROUPS, SSD_HEADS_PER_GROUP).transpose(0, 2, 1, 3)
    w_dt = jnp.pad(w_dt, ((0, 0), (0, 0), (0, 0), (0, GROUP_LANES // 2 - SSD_HEADS_PER_GROUP)))
    q_off = inner + conv_ch + 2 * nheads
    w_side = jnp.concatenate([w_bf[:, q_off:], w_dt.reshape(D, LANES).astype(BF16)], axis=1)
    q_mem, dtr = side_proj(h3, w_side, mem_q_gain, tm)

    tr = ssd_prep(dtr, _ssd_lane_layout(dt_bias), _ssd_lane_layout(a_log))
    dskip = jnp.repeat(d_skip, SSD_HEAD_DIM)
    nchunks = S // SSD_CHUNK
    kc = next(k for k in (8, 4, 2, 1) if nchunks % (2 * k) == 0)
    y = ssd_scan(xc, tr, dskip, norm_g, kc)
    o_mem = mem_attention(q_mem, kmem, vmem_, tq=tm)
    w_out_bf = w_out.astype(BF16)
    return outproj(x2, y.reshape(nb * S, inner), o_mem.reshape(nb * S, mw),
                   w_out_bf[:inner], w_out_bf[inner:], tm=_tile(nb * S, ROW_TILE), tn=_tile(D, OUT_TILE))


def _dil_mixer(x2, h3, kmem, vmem_, w_in, q_gain, k_gain, w_out, rel_bias, mem_q_gain):
    nb, S, D = h3.shape
    width = DIL_HEADS * DIL_HEAD_DIM
    mw = kmem.shape[2]
    tm = _tile(S, ROW_TILE)
    w_bf = w_in.astype(BF16)
    qkvs = []
    for g, (_, r) in enumerate(DIL_CONFIGS):
        gains = jnp.stack([jnp.tile(q_gain[g], DIL_HEADS), jnp.tile(k_gain[g], DIL_HEADS),
                           jnp.ones((width,), F32)]).reshape(3, 1, width)
        out = proj(h3, w_bf, g * 3 * width, 3 * width, gains, out_dtype=BF16, tm=tm, tn=width,
                   norm_width=DIL_HEAD_DIM, norm_tiles=2, r=r)
        qkvs.append(out.reshape(nb, S, 3 * width))
    q_off = len(DIL_CONFIGS) * 3 * width
    hd = mem_q_gain.shape[0]
    qg = jnp.tile(mem_q_gain, mw // hd).reshape(1, 1, mw)
    q_mem = proj(h3, w_bf, q_off, mw, qg, out_dtype=BF16, tm=tm, tn=mw, norm_width=hd)[:, 0]
    o = dilated_attention(qkvs, rel_bias)
    o_mem = mem_attention(q_mem, kmem, vmem_, tq=tm)
    w_out_bf = w_out.astype(BF16)
    return outproj(x2, o.reshape(nb * S, width), o_mem.reshape(nb * S, mw),
                   w_out_bf[:width], w_out_bf[width:], tm=_tile(nb * S, ROW_TILE), tn=_tile(D, OUT_TILE))


def kernel(x, mem, rel_bias, ffn_norm, ffn_w_in, ffn_w_out, mix_norm, mem_norm, mem_w_kv, mem_q_gain, mem_k_gain, ssd_w_in, ssd_conv_w, ssd_conv_b, ssd_dt_bias, ssd_A_log, ssd_D, ssd_norm, ssd_w_out, dil_w_in, dil_q_gain, dil_k_gain, dil_w_out):
    nb, S, D = x.shape
    depth = ffn_norm.shape[0]
    T = nb * S
    x2 = x.reshape(T, D)
    tm_ffn, tf = _tile(T, ROW_TILE), _tile(ffn_w_out.shape[2], FF_TILE)
    ffn_w_in = ffn_w_in.astype(BF16)
    ffn_w_out = ffn_w_out.astype(BF16)
    for i in range(depth):
        x2, h2 = ffn(x2, ffn_norm[i, 0], ffn_w_in, ffn_w_out, i, 0, tm_ffn, tf, next_gain=mix_norm[i])
        h3 = h2.reshape(nb, S, D)
        kmem, vmem_ = mem_kv(mem, mem_norm[i], mem_w_kv[i].astype(BF16), mem_k_gain[i])
        j = i // 2
        if i % 2 == 0:
            x2 = _ssd_mixer(x2, h3, kmem, vmem_, ssd_w_in[j], ssd_conv_w[j], ssd_conv_b[j], ssd_dt_bias[j],
                            ssd_A_log[j], ssd_D[j], ssd_norm[j], ssd_w_out[j], mem_q_gain[i])
        else:
            x2 = _dil_mixer(x2, h3, kmem, vmem_, dil_w_in[j], dil_q_gain[j], dil_k_gain[j], dil_w_out[j],
                            rel_bias, mem_q_gain[i])
        x2 = ffn(x2, ffn_norm[i, 1], ffn_w_in, ffn_w_out, i, 1, tm_ffn, tf)
    return x2.reshape(nb, S, D)
```

```python
import functools
import math

import jax
import jax.numpy as jnp
import numpy as np
from jax import lax
from jax.experimental import pallas as pl
from jax.experimental.pallas import tpu as pltpu

F32 = jnp.float32
BF16 = jnp.bfloat16
EPS = 1e-6

MEM_HEADS = 4
SSD_HEAD_DIM = 64
SSD_GROUPS = 8
SSD_HEADS_PER_GROUP = 6
SSD_STATE = 128
SSD_CONV = 5
SSD_CHUNK = 128
DIL_CONFIGS = ((128, 1), (512, 4), (2048, 16))
DIL_HEADS = 8
DIL_HEAD_DIM = 128
DIL_HALF = 64
DIL_SEGS = 4
DIL_QUERY_BLOCK = 128
REL_BUCKETS = 32
REL_MAX_DISTANCE = 1024

LANES = 128
VMEM_LIMIT_BYTES = 62 * 1024 * 1024
GROUP_LANES = 16

ROW_TILE = 1024
FF_TILE = 512
OUT_TILE = 1024


def _tile(n, pref):
    return pref if n % pref == 0 else n


def _cparams(*sem):
    return pltpu.CompilerParams(dimension_semantics=sem, vmem_limit_bytes=VMEM_LIMIT_BYTES)


def _rms(x, gain):
    ms = jnp.mean(x * x, axis=-1, keepdims=True)
    return x * lax.rsqrt(ms + EPS) * gain


def _dot(a, b):
    return jnp.dot(a, b, preferred_element_type=F32)


def _dot_nt(a, b):
    return lax.dot_general(a, b, (((1,), (1,)), ((), ())), preferred_element_type=F32)


FFN_SUB_ROWS = 512


def _ffn_kernel(x_ref, g_ref, wg_ref, wu_ref, wo_ref, *rest, emit_norm):
    if emit_norm:
        g2_ref, o_ref, hn_ref = rest
        h_ref = hn_ref
    else:
        o_ref, h_ref = rest
    j = pl.program_id(1)

    tm = x_ref.shape[0]
    sub = min(tm, FFN_SUB_ROWS)

    @pl.when(j == 0)
    def _():
        for r0 in range(0, tm, sub // 2):
            rows = slice(r0, r0 + sub // 2)
            h_ref[rows, :] = _rms(x_ref[rows, :], g_ref[...]).astype(BF16)

    def step(base_ref, norm_out=False):
        for r0 in range(0, tm, sub):
            h = h_ref[r0:r0 + sub, :]
            gate = _dot(h, wg_ref[...])
            up = _dot(h, wu_ref[...])
            a = (jax.nn.silu(gate) * up * 0.5).astype(BF16)
            o_ref[r0:r0 + sub, :] = base_ref[r0:r0 + sub, :] + _dot(a, wo_ref[...])
            if norm_out:
                for q0 in range(r0, r0 + sub, sub // 2):
                    rows = slice(q0, q0 + sub // 2)
                    hn_ref[rows, :] = _rms(o_ref[rows, :], g2_ref[...]).astype(hn_ref.dtype)

    last = pl.num_programs(1) - 1
    pl.when(j == 0)(lambda: step(x_ref))
    if emit_norm:
        pl.when((j > 0) & (j < last))(lambda: step(o_ref))
        pl.when(j == last)(lambda: step(o_ref, norm_out=True))
    else:
        pl.when(j > 0)(lambda: step(o_ref))


def ffn(x2, gain, w_in, w_out, layer, which, tm, tf, next_gain=None):
    T, D = x2.shape
    F = w_out.shape[2]
    nf = F // tf
    emit_norm = next_gain is not None
    assert nf > 1 or not emit_norm
    row_spec = pl.BlockSpec((tm, D), lambda i, j: (i, 0))
    vec_spec = pl.BlockSpec((1, D), lambda i, j: (0, 0))
    in_specs = [row_spec, vec_spec,
                pl.BlockSpec((None, None, D, tf), lambda i, j: (layer, which, 0, j)),
                pl.BlockSpec((None, None, D, tf), lambda i, j: (layer, which, 0, j + nf)),
                pl.BlockSpec((None, None, tf, D), lambda i, j: (layer, which, j, 0))]
    args = [x2, gain.reshape(1, D), w_in, w_in, w_out]
    out_specs, out_shape = row_spec, jax.ShapeDtypeStruct((T, D), F32)
    if emit_norm:
        in_specs.append(vec_spec)
        args.append(next_gain.reshape(1, D))
        out_specs, out_shape = [row_spec, row_spec], [out_shape, jax.ShapeDtypeStruct((T, D), BF16)]
    return pl.pallas_call(
        functools.partial(_ffn_kernel, emit_norm=emit_norm),
        grid=(T // tm, nf),
        in_specs=in_specs,
        out_specs=out_specs,
        out_shape=out_shape,
        scratch_shapes=[] if emit_norm else [pltpu.VMEM((tm, D), BF16)],
        compiler_params=_cparams("parallel", "arbitrary"),
        name="ffn",
    )(*args)


PROJ_CHUNK = 256
EINSHAPE_MIN_STRIDE = 16


def _proj_kernel(h_ref, w_ref, g_ref, o_ref, *scratch, norm_width, norm_tiles, r):
    tm, tn = h_ref.shape[1], w_ref.shape[1]
    rows_per = tm // r
    nw = norm_width if norm_width else PROJ_CHUNK
    assert PROJ_CHUNK % nw == 0 or nw % PROJ_CHUNK == 0
    chunk = max(PROJ_CHUNK, nw)

    def emit(normed):
        for c0 in range(0, tn, chunk):
            res = _dot(h_ref[0], w_ref[:, c0:c0 + chunk])
            if r == 1:
                if not normed:
                    o_ref[0, 0, :, c0:c0 + chunk] = res.astype(o_ref.dtype)
                    continue
                for c in range(c0, c0 + chunk, nw):
                    blk = res[:, c - c0:c - c0 + nw]
                    o_ref[0, 0, :, c:c + nw] = _rms(blk, g_ref[0, :, c:c + nw]).astype(o_ref.dtype)
                continue
            if r >= EINSHAPE_MIN_STRIDE:
                for c in range(c0, c0 + chunk, LANES):
                    blk = pltpu.einshape("(jm)l->mjl", res[:, c - c0:c - c0 + LANES], m=r)
                    if normed:
                        blk = _rms(blk, g_ref[0, :, c:c + LANES])
                    o_ref[0, :, :, c:c + LANES] = blk.astype(o_ref.dtype)
                continue
            acc_ref, = scratch
            for c in range(c0, c0 + chunk, LANES):
                acc_ref[c // LANES] = res[:, c - c0:c - c0 + LANES]
            for m in range(r):
                rows = pl.ds(m, rows_per, stride=r)
                for c in range(c0, c0 + chunk, LANES):
                    blk = acc_ref[c // LANES, rows, :]
                    if normed:
                        blk = _rms(blk, g_ref[0, :, c:c + LANES])
                    o_ref[0, m, :, c:c + LANES] = blk.astype(o_ref.dtype)

    if r > 1:
        assert norm_width in (0, LANES)
    if norm_width == 0:
        emit(False)
    elif norm_tiles is None:
        emit(True)
    else:
        j = pl.program_id(2)
        pl.when(j < norm_tiles)(lambda: emit(True))
        pl.when(j >= norm_tiles)(lambda: emit(False))


def proj(h3, w, col0, ncols, gains, *, out_dtype, tm, tn, norm_width=0, norm_tiles=None, r=1):
    nb, S, K = h3.shape
    assert col0 % tn == 0 and ncols % tn == 0 and S % tm == 0 and tm % (8 * r) == 0
    j0 = col0 // tn
    nj = ncols // tn
    if gains is None:
        gains = jnp.ones((nj, 1, tn), F32)
    strided = 1 < r < EINSHAPE_MIN_STRIDE
    scratch = [pltpu.VMEM((tn // LANES, tm, LANES), F32)] if strided else []
    kern = functools.partial(_proj_kernel, norm_width=norm_width, norm_tiles=norm_tiles, r=r)
    return pl.pallas_call(
        kern,
        grid=(nb, S // tm, nj),
        in_specs=[pl.BlockSpec((1, tm, K), lambda b, i, j: (b, i, 0)),
                  pl.BlockSpec((K, tn), lambda b, i, j: (0, j0 + j)),
                  pl.BlockSpec((1, 1, tn), lambda b, i, j: (j, 0, 0))],
        out_specs=pl.BlockSpec((1, r, tm // r, tn), lambda b, i, j: (b, 0, i, j)),
        out_shape=jax.ShapeDtypeStruct((nb, r, S // r, ncols), out_dtype),
        scratch_shapes=scratch,
        compiler_params=_cparams("parallel", "parallel", "arbitrary"),
        name="proj",
    )(h3, w, gains)


def _side_proj_kernel(h_ref, w_ref, g_ref, q_ref, dt_ref, *, head_dim):
    mw = q_ref.shape[2]
    for c in range(mw // head_dim):
        cols = slice(c * head_dim, (c + 1) * head_dim)
        q_ref[0, :, cols] = _rms(_dot(h_ref[0], w_ref[:, cols]), g_ref[...]).astype(q_ref.dtype)
    dt_ref[0] = _dot(h_ref[0], w_ref[:, mw:])


def side_proj(h3, w, q_gain, tm):
    nb, S, K = h3.shape
    head_dim = q_gain.shape[0]
    n = w.shape[1]
    mw = n - LANES
    return pl.pallas_call(
        functools.partial(_side_proj_kernel, head_dim=head_dim),
        grid=(nb, S // tm),
        in_specs=[pl.BlockSpec((1, tm, K), lambda b, i: (b, i, 0)),
                  pl.BlockSpec((K, n), lambda b, i: (0, 0)),
                  pl.BlockSpec((1, head_dim), lambda b, i: (0, 0))],
        out_specs=[pl.BlockSpec((1, tm, mw), lambda b, i: (b, i, 0)),
                   pl.BlockSpec((1, tm, LANES), lambda b, i: (b, i, 0))],
        out_shape=[jax.ShapeDtypeStruct((nb, S, mw), BF16), jax.ShapeDtypeStruct((nb, S, LANES), F32)],
        compiler_params=_cparams("parallel", "parallel"),
        name="side_proj",
    )(h3, w, q_gain.reshape(1, head_dim))


def _outproj_kernel(x_ref, a1_ref, a2_ref, w1_ref, w2_ref, o_ref):
    for c0 in range(0, o_ref.shape[1], PROJ_CHUNK):
        cols = slice(c0, c0 + PROJ_CHUNK)
        o_ref[:, cols] = x_ref[:, cols] + _dot(a1_ref[...], w1_ref[:, cols]) + _dot(a2_ref[...], w2_ref[:, cols])


def outproj(x2, a1, a2, w1, w2, tm, tn):
    T, D = x2.shape
    k1, k2 = a1.shape[1], a2.shape[1]
    return pl.pallas_call(
        _outproj_kernel,
        grid=(T // tm, D // tn),
        in_specs=[pl.BlockSpec((tm, tn), lambda i, j: (i, j)),
                  pl.BlockSpec((tm, k1), lambda i, j: (i, 0)),
                  pl.BlockSpec((tm, k2), lambda i, j: (i, 0)),
                  pl.BlockSpec((k1, tn), lambda i, j: (0, j)),
                  pl.BlockSpec((k2, tn), lambda i, j: (0, j))],
        out_specs=pl.BlockSpec((tm, tn), lambda i, j: (i, j)),
        out_shape=jax.ShapeDtypeStruct((T, D), F32),
        compiler_params=_cparams("parallel", "arbitrary"),
        name="outproj",
    )(x2, a1, a2, w1, w2)


def _memkv_kernel(mem_ref, g_ref, w_ref, kg_ref, k_ref, v_ref):
    mw = k_ref.shape[2]
    hd = kg_ref.shape[1]
    memn = _rms(mem_ref[0], g_ref[...]).astype(BF16)
    kv = _dot(memn, w_ref[...])
    for hh in range(mw // hd):
        cols = slice(hh * hd, (hh + 1) * hd)
        k_ref[0, :, cols] = _rms(kv[:, cols], kg_ref[...]).astype(BF16)
    v_ref[0] = kv[:, mw:].astype(BF16)


def mem_kv(mem, mem_gain, w_kv, k_gain):
    nb, M, D = mem.shape
    mw = w_kv.shape[1] // 2
    hd = k_gain.shape[0]
    out = jax.ShapeDtypeStruct((nb, M, mw), BF16)
    return pl.pallas_call(
        _memkv_kernel,
        grid=(nb,),
        in_specs=[pl.BlockSpec((1, M, D), lambda b: (b, 0, 0)),
                  pl.BlockSpec((1, D), lambda b: (0, 0)),
                  pl.BlockSpec((D, 2 * mw), lambda b: (0, 0)),
                  pl.BlockSpec((1, hd), lambda b: (0, 0))],
        out_specs=[pl.BlockSpec((1, M, mw), lambda b: (b, 0, 0))] * 2,
        out_shape=[out, out],
        compiler_params=_cparams("parallel"),
        name="mem_kv",
    )(mem, mem_gain.reshape(1, D), w_kv, k_gain.reshape(1, hd))


def _memattn_kernel(q_ref, k_ref, v_ref, o_ref, *, heads):
    hd = q_ref.shape[2] // heads
    scale = hd ** -0.5
    for hh in range(heads):
        cols = slice(hh * hd, (hh + 1) * hd)
        s = _dot_nt(q_ref[0, :, cols], k_ref[0, :, cols]) * scale
        e = jnp.exp(s - jnp.max(s, axis=-1, keepdims=True))
        p = e / jnp.sum(e, axis=-1, keepdims=True)
        o_ref[0, :, cols] = _dot(p.astype(BF16), v_ref[0, :, cols]).astype(o_ref.dtype)


def mem_attention(q, k, v, tq):
    nb, S, mw = q.shape
    M = k.shape[1]
    return pl.pallas_call(
        functools.partial(_memattn_kernel, heads=MEM_HEADS),
        grid=(nb, S // tq),
        in_specs=[pl.BlockSpec((1, tq, mw), lambda b, i: (b, i, 0)),
                  pl.BlockSpec((1, M, mw), lambda b, i: (b, 0, 0)),
                  pl.BlockSpec((1, M, mw), lambda b, i: (b, 0, 0))],
        out_specs=pl.BlockSpec((1, tq, mw), lambda b, i: (b, i, 0)),
        out_shape=jax.ShapeDtypeStruct((nb, S, mw), BF16),
        compiler_params=_cparams("parallel", "parallel"),
        name="mem_attn",
    )(q, k, v)


SSD_GROUP_WIDTH = SSD_HEADS_PER_GROUP * SSD_HEAD_DIM
SSD_GROUP_CONV = SSD_GROUP_WIDTH + 2 * SSD_STATE
SSD_GROUP_COLS = SSD_GROUP_CONV + SSD_GROUP_WIDTH
HALO_ROWS = 16


def _ssd_inproj_kernel(hp_ref, h_ref, hn_ref, w_ref, cw_ref, cb_ref, o_ref, lhs_ref, ext_ref):
    i = pl.program_id(1)
    tm = h_ref.shape[1]
    lo = HALO_ROWS - SSD_CONV // 2

    @pl.when(pl.program_id(2) == 0)
    def _():
        lhs_ref[0:HALO_ROWS] = jnp.where(i > 0, hp_ref[0], jnp.zeros_like(hp_ref[0]))
        lhs_ref[HALO_ROWS:HALO_ROWS + tm] = h_ref[0]
        lhs_ref[HALO_ROWS + tm:] = jnp.where(i < pl.num_programs(1) - 1, hn_ref[0], jnp.zeros_like(hn_ref[0]))

    for c0 in range(0, SSD_GROUP_COLS, PROJ_CHUNK):
        if c0 >= SSD_GROUP_CONV:
            o_ref[0, :, c0:c0 + PROJ_CHUNK] = jax.nn.silu(_dot(h_ref[0], w_ref[:, c0:c0 + PROJ_CHUNK]))
            continue
        half = (tm + 2 * HALO_ROWS) // 2
        for r0 in (0, half):
            res = _dot(lhs_ref[r0:r0 + half, :], w_ref[:, c0:c0 + PROJ_CHUNK])
            for c in range(c0, c0 + PROJ_CHUNK, LANES):
                ext_ref[c // LANES, r0:r0 + half] = res[:, c - c0:c - c0 + LANES]
        for c in range(c0, c0 + PROJ_CHUNK, LANES):
            s = c // LANES
            if c >= SSD_GROUP_CONV:
                o_ref[0, :, c:c + LANES] = jax.nn.silu(ext_ref[s, HALO_ROWS:HALO_ROWS + tm, :])
                continue
            acc = cb_ref[:, c:c + LANES] + cw_ref[0:1, c:c + LANES] * ext_ref[s, lo:lo + tm, :]
            for k in range(1, SSD_CONV):
                acc = acc + cw_ref[k:k + 1, c:c + LANES] * ext_ref[s, lo + k:lo + k + tm, :]
            o_ref[0, :, c:c + LANES] = jax.nn.silu(acc)


def ssd_inproj(h3, w, conv_w, conv_b, tm):
    nb, S, K = h3.shape
    G, gc, cols = SSD_GROUPS, SSD_GROUP_CONV, SSD_GROUP_COLS
    assert PROJ_CHUNK % LANES == 0 and gc % LANES == 0 and cols % PROJ_CHUNK == 0 and tm % HALO_ROWS == 0
    nh = tm // HALO_ROWS
    last = S // HALO_ROWS - 1
    return pl.pallas_call(
        _ssd_inproj_kernel,
        grid=(nb, S // tm, G),
        in_specs=[pl.BlockSpec((1, HALO_ROWS, K), lambda b, i, g: (b, jnp.maximum(i * nh - 1, 0), 0)),
                  pl.BlockSpec((1, tm, K), lambda b, i, g: (b, i, 0)),
                  pl.BlockSpec((1, HALO_ROWS, K), lambda b, i, g: (b, jnp.minimum((i + 1) * nh, last), 0)),
                  pl.BlockSpec((K, cols), lambda b, i, g: (0, g)),
                  pl.BlockSpec((SSD_CONV, gc), lambda b, i, g: (0, g)),
                  pl.BlockSpec((1, gc), lambda b, i, g: (0, g))],
        out_specs=pl.BlockSpec((1, tm, cols), lambda b, i, g: (b, i, g)),
        out_shape=jax.ShapeDtypeStruct((nb, S, G * cols), F32),
        scratch_shapes=[pltpu.VMEM((tm + 2 * HALO_ROWS, K), BF16),
                        pltpu.VMEM((pl.cdiv(gc, PROJ_CHUNK) * PROJ_CHUNK // LANES, tm + 2 * HALO_ROWS, LANES), F32)],
        compiler_params=_cparams("parallel", "parallel", "arbitrary"),
        name="ssd_inproj",
    )(h3, h3, h3, w, conv_w, conv_b.reshape(1, G * gc))


LOG2E = math.log2(math.e)


def _ssd_prep_kernel(dtr_ref, bias_ref, alog_ref, tr_ref):
    Q = SSD_CHUNK
    ii = lax.broadcasted_iota(jnp.int32, (Q, Q), 0)
    jj = lax.broadcasted_iota(jnp.int32, (Q, Q), 1)
    tril = (jj <= ii).astype(F32)
    triu = (jj >= ii).astype(F32)
    lane = lax.broadcasted_iota(jnp.int32, (Q, LANES), 1)
    is_bwd = (lane % GROUP_LANES) >= GROUP_LANES // 2
    hi = lax.Precision.HIGHEST
    for i in range(tr_ref.shape[1]):
        x = dtr_ref[0, i * Q:(i + 1) * Q, :] + bias_ref[...]
        dt = jnp.maximum(x, 0.0) + jnp.log1p(jnp.exp(-jnp.abs(x)))
        a = dt * (-jnp.exp(alog_ref[...]))
        cs_fwd = jnp.dot(tril, a, precision=hi, preferred_element_type=F32)
        cs_bwd = jnp.dot(triu, a, precision=hi, preferred_element_type=F32)
        cs = jnp.where(is_bwd, cs_bwd, cs_fwd)
        last = jnp.where(is_bwd[0:1], cs[0:1], cs[Q - 1:Q])
        cs2 = cs * LOG2E
        parts = (cs2, cs2 - jnp.log2(dt), jnp.exp(last - cs) * dt, jnp.broadcast_to(jnp.exp(last), (Q, LANES)))
        for p, part in enumerate(parts):
            rows = part.T
            for g in range(SSD_GROUPS):
                tr_ref[0, i, g, p * GROUP_LANES:(p + 1) * GROUP_LANES] = rows[g * GROUP_LANES:(g + 1) * GROUP_LANES]


def ssd_prep(dtr, bias_l, alog_l):
    nb, S, _ = dtr.shape
    Q = SSD_CHUNK
    nc = S // Q
    kp = next(k for k in (8, 4, 2, 1) if nc % k == 0)
    vec_spec = pl.BlockSpec((1, LANES), lambda b, c: (0, 0))
    return pl.pallas_call(
        _ssd_prep_kernel,
        grid=(nb, nc // kp),
        in_specs=[pl.BlockSpec((1, kp * Q, LANES), lambda b, c: (b, c, 0)), vec_spec, vec_spec],
        out_specs=pl.BlockSpec((1, kp, SSD_GROUPS, 4 * GROUP_LANES, Q), lambda b, c: (b, c, 0, 0, 0)),
        out_shape=jax.ShapeDtypeStruct((nb, nc, SSD_GROUPS, 4 * GROUP_LANES, Q), F32),
        compiler_params=_cparams("parallel", "parallel"),
        name="ssd_prep",
    )(dtr, bias_l, alog_l)


def _ssd_direction(d, g, sub, xc_ref, tr_ref, st_ref):
    Q = SSD_CHUNK
    P, N, gw = SSD_HEAD_DIM, SSD_STATE, SSD_GROUP_WIDTH
    rows = slice(sub * Q, (sub + 1) * Q)
    xs = xc_ref[0, rows, 0:gw]
    bm = xc_ref[0, rows, gw:gw + N]
    cm = xc_ref[0, rows, gw + N:gw + 2 * N]
    cb = _dot_nt(cm.astype(BF16), bm.astype(BF16))
    bt = bm.T

    cs2T, rowT, wT, cdT = (tr_ref[0, sub, 0, part * GROUP_LANES:(part + 1) * GROUP_LANES, :] for part in range(4))
    ii = lax.broadcasted_iota(jnp.int32, (Q, Q), 0)
    jj = lax.broadcasted_iota(jnp.int32, (Q, Q), 1)
    causal = (jj <= ii) if d == 0 else (jj >= ii)
    lo = lax.broadcasted_iota(jnp.int32, (Q, LANES), 1) < P
    st = st_ref[d]
    ys, sts = [], []
    for p in range(SSD_HEADS_PER_GROUP // 2):
        cols = slice(p * LANES, (p + 1) * LANES)
        ms, ss = [], []
        k0 = d * (GROUP_LANES // 2) + 2 * p
        for k in (k0, k0 + 1):
            col2 = jnp.broadcast_to(cs2T[k:k + 1, :], (Q, Q)).T
            decay_dt = jnp.exp2(jnp.where(causal, col2 - rowT[k:k + 1, :], -jnp.inf))
            ms.append((cb * decay_dt).astype(BF16))
            ms.append((cm * jnp.exp2(col2)).astype(BF16))
            ss.append((bt * wT[k:k + 1, :]).astype(BF16))
        cd = jnp.where(lo[0:1], cdT[k0:k0 + 1, :], cdT[k0 + 1:k0 + 2, :])
        xp, sp = xs[:, cols], st[:, cols]
        x_lo, x_hi = jnp.where(lo, xp, 0.0).astype(BF16), jnp.where(lo, 0.0, xp).astype(BF16)
        s_lo, s_hi = jnp.where(lo, sp, 0.0).astype(BF16), jnp.where(lo, 0.0, sp).astype(BF16)
        ys.append(_dot(jnp.concatenate(ms, axis=1), jnp.concatenate([x_lo, s_lo, x_hi, s_hi], axis=0)))
        sts.append(cd * sp + _dot(jnp.concatenate(ss, axis=1), jnp.concatenate([x_lo, x_hi], axis=0)))
    st_ref[d] = jnp.concatenate(sts, axis=1)
    return jnp.concatenate(ys, axis=1)


SCAN_RING = 3


def _ssd_scan_kernel(xc_hbm, trf, trb, dskip_ref, ng_ref, y_ref, st_ref, ysum_ref, xbuf, sem, *, kc):
    b = pl.program_id(0)
    g = pl.program_id(1)
    c = pl.program_id(2)
    ngroups = pl.num_programs(1)
    nsteps = pl.num_programs(2)
    Q = SSD_CHUNK
    rows_per = kc * Q
    n = (b * ngroups + g) * nsteps + c
    total = pl.num_programs(0) * ngroups * nsteps

    def block_copies(step):
        slot = step % SCAN_RING
        bb = step // (ngroups * nsteps)
        gg = (step // nsteps) % ngroups
        cc = step % nsteps
        col0 = pl.multiple_of(gg * SSD_GROUP_COLS, SSD_GROUP_COLS)
        out = []
        for d, blk in ((0, cc), (1, nsteps - 1 - cc)):
            row0 = pl.multiple_of(blk * rows_per, rows_per)
            src = xc_hbm.at[pl.ds(bb, 1), pl.ds(row0, rows_per), pl.ds(col0, SSD_GROUP_COLS)]
            out.append(pltpu.make_async_copy(src, xbuf.at[d, slot], sem.at[d, slot]))
        return out

    @pl.when(n == 0)
    def _():
        for first in range(SCAN_RING - 1):
            for cp in block_copies(jnp.int32(first)):
                cp.start()

    ahead = n + (SCAN_RING - 1)

    @pl.when(ahead < total)
    def _():
        for cp in block_copies(ahead):
            cp.start()

    for cp in block_copies(n):
        cp.wait()
    slot = n % SCAN_RING
    xcf = xbuf.at[0, slot]
    xcb = xbuf.at[1, slot]

    @pl.when(c == 0)
    def _():
        st_ref[...] = jnp.zeros_like(st_ref)
        ysum_ref[...] = jnp.zeros_like(ysum_ref)

    def finish(y_dir, row0, xc_ref, sub):
        rows = pl.ds(pl.multiple_of(row0, Q), Q)
        blk = slice(sub * Q, (sub + 1) * Q)
        tot = ysum_ref[rows, :] + y_dir + dskip_ref[...] * xc_ref[0, blk, 0:SSD_GROUP_WIDTH]
        ysum_ref[rows, :] = y_dir
        gated = tot * xc_ref[0, blk, SSD_GROUP_CONV:SSD_GROUP_COLS]
        y_ref[0, rows, :] = _rms(gated, ng_ref[...]).astype(y_ref.dtype)

    for i in range(kc):
        y_f = _ssd_direction(0, g, i, xcf, trf, st_ref)
        finish(y_f, (c * kc + i) * Q, xcf, i)
        y_b = _ssd_direction(1, g, kc - 1 - i, xcb, trb, st_ref)
        finish(y_b, ((nsteps - 1 - c) * kc + kc - 1 - i) * Q, xcb, kc - 1 - i)


def ssd_scan(xc, tr, dskip, norm_g, kc):
    nb, S, _ = xc.shape
    Q = SSD_CHUNK
    nsteps = S // (Q * kc)
    assert nsteps % 2 == 0 and Q == LANES
    G, N, gw = SSD_GROUPS, SSD_STATE, SSD_GROUP_WIDTH
    inner = G * gw

    def both(shape, f):
        return [pl.BlockSpec(shape, lambda b, g, c: f(b, g, c)),
                pl.BlockSpec(shape, lambda b, g, c: f(b, g, nsteps - 1 - c))]

    in_specs = ([pl.BlockSpec(memory_space=pl.ANY)]
                + both((1, kc, 1, 4 * GROUP_LANES, Q), lambda b, g, c: (b, c, g, 0, 0))
                + [pl.BlockSpec((1, gw), lambda b, g, c: (0, g)),
                   pl.BlockSpec((1, gw), lambda b, g, c: (0, g))])
    return pl.pallas_call(
        functools.partial(_ssd_scan_kernel, kc=kc),
        grid=(nb, G, nsteps),
        in_specs=in_specs,
        out_specs=pl.BlockSpec((1, S, gw), lambda b, g, c: (b, 0, g)),
        out_shape=jax.ShapeDtypeStruct((nb, S, inner), BF16),
        scratch_shapes=[pltpu.VMEM((2, N, gw), F32), pltpu.VMEM((S, gw), F32),
                        pltpu.VMEM((2, SCAN_RING, 1, kc * Q, SSD_GROUP_COLS), F32),
                        pltpu.SemaphoreType.DMA((2, SCAN_RING))],
        compiler_params=_cparams("arbitrary", "arbitrary", "arbitrary"),
        name="ssd_scan",
    )(xc, tr, tr, dskip.reshape(1, inner), norm_g.reshape(1, inner))


def _t5_bucket_np(rel):
    half = REL_BUCKETS // 2
    exact = half // 2
    n = np.abs(rel)
    far = exact + (np.log(np.maximum(n, 1).astype(np.float32) / np.float32(exact))
                   / np.float32(math.log(REL_MAX_DISTANCE / exact)) * np.float32(half - exact)).astype(np.int32)
    far = np.minimum(far, half - 1)
    return np.where(rel > 0, half, 0) + np.where(n < exact, n, far)


def _dil_bucket_rows(win):
    d = np.arange(win)
    rel = d - DIL_HALF
    rows = []
    for _, dilation in DIL_CONFIGS:
        rows.append(np.where(d <= 2 * DIL_HALF, _t5_bucket_np(rel * dilation), -1))
    return np.broadcast_to(np.stack(rows)[:, None, :], (len(DIL_CONFIGS), 8, win)).astype(np.int32)


def _dil_kernel(tbl_ref, bm_ref, *refs, seq, seg, tq):
    ngroups = len(DIL_CONFIGS)
    win = tq + 2 * DIL_HALF
    in_refs = refs[:7 * ngroups]
    o_ref = refs[7 * ngroups]
    bias_ref, m_ref, l_ref, a_ref, kext_ref, vext_ref = refs[7 * ngroups + 1:]
    h = pl.program_id(1)
    t = pl.program_id(2)
    scale = DIL_HEAD_DIM ** -0.5

    @pl.when(t == 0)
    def _():
        m_ref[...] = jnp.full_like(m_ref, -jnp.inf)
        l_ref[...] = jnp.zeros_like(l_ref)
        a_ref[...] = jnp.zeros_like(a_ref)
        kcol = lax.broadcasted_iota(jnp.int32, (tq, win), 1)
        for gi in range(ngroups):
            ids = bm_ref[gi]
            row = jnp.zeros(ids.shape, F32)
            for u in range(REL_BUCKETS):
                row = jnp.where(ids == u, tbl_ref[u, gi * DIL_HEADS + h], row)
            row = jnp.where(ids < 0, -jnp.inf, row)
            band = pltpu.roll(jnp.broadcast_to(row[0:1], (tq, win)), 0, 1, stride=1, stride_axis=0)
            no_left = jnp.where(kcol >= DIL_HALF, band, -jnp.inf)
            bias_ref[gi, 0] = band
            bias_ref[gi, 1] = no_left
            bias_ref[gi, 2] = jnp.where(kcol < win - DIL_HALF, band, -jnp.inf)
            bias_ref[gi, 3] = jnp.where(kcol < win - DIL_HALF, no_left, -jnp.inf)

    for gi, (_, r) in enumerate(DIL_CONFIGS):
        q_ref, kp, kc, kn, vp, vc, vn = in_refs[7 * gi:7 * gi + 7]
        sub_len = seq // r
        piece = min(seg, sub_len)
        nblk = piece // tq
        for ext, prev, cur, nxt in ((kext_ref, kp, kc, kn), (vext_ref, vp, vc, vn)):
            ext[gi, 0:DIL_HALF] = prev[0]
            ext[gi, DIL_HALF:DIL_HALF + seg] = cur[0]
            ext[gi, DIL_HALF + seg:] = nxt[0]
        for p in range(seg // piece):
            first_row = t * seg + p * piece
            m_res = first_row // sub_len
            j0 = first_row % sub_len
            for jb in range(nblk):
                row0 = p * piece + jb * tq
                q = q_ref[0, row0:row0 + tq, :]
                kw = kext_ref[gi, row0:row0 + win, :]
                vw = vext_ref[gi, row0:row0 + win, :]
                variant = 0
                if jb == 0:
                    variant = variant + (j0 == 0).astype(jnp.int32)
                if jb == nblk - 1:
                    variant = variant + 2 * (j0 + piece == sub_len).astype(jnp.int32)
                s = _dot_nt(q, kw) * scale + bias_ref[gi, variant]
                mb = jnp.max(s, axis=-1, keepdims=True)
                e = jnp.exp(s - mb)
                lb = jnp.sum(e, axis=-1, keepdims=True)
                acc = _dot(e.astype(BF16), vw)
                start = (j0 + jb * tq) * r + m_res
                rows = pl.ds(start, tq, stride=r) if r > 1 else pl.ds(start, tq)
                m_old = m_ref[rows, :]
                m_new = jnp.maximum(m_old, mb)
                alpha = jnp.exp(m_old - m_new)
                beta = jnp.exp(mb - m_new)
                l_new = alpha * l_ref[rows, :] + beta * lb
                a_new = alpha * a_ref[rows, :] + beta * acc
                m_ref[rows, :] = m_new
                l_ref[rows, :] = l_new
                a_ref[rows, :] = a_new

    @pl.when(t == pl.num_programs(2) - 1)
    def _():
        o_ref[0] = (a_ref[...] / l_ref[...]).astype(o_ref.dtype)


def dilated_attention(qkvs, rel_bias):
    nb, S, _ = qkvs[0].shape
    hd = DIL_HEAD_DIM
    seg = S // DIL_SEGS
    tq = _tile(seg, DIL_QUERY_BLOCK)
    win = tq + 2 * DIL_HALF
    assert seg % tq == 0 and tq % LANES == 0
    for w, r in DIL_CONFIGS:
        piece = min(seg, S // r)
        assert w == 2 * DIL_HALF * r and (S // r) % piece == 0 and seg % piece == 0 and piece % tq == 0
    nhb = seg // DIL_HALF
    last_hb = S // DIL_HALF - 1
    in_specs = [pl.BlockSpec(memory_space=pltpu.SMEM),
                pl.BlockSpec((len(DIL_CONFIGS), 8, win), lambda b, h, t: (0, 0, 0))]
    args = [rel_bias, jnp.asarray(_dil_bucket_rows(win))]
    for g in range(len(DIL_CONFIGS)):
        in_specs.append(pl.BlockSpec((1, seg, hd), lambda b, h, t: (b, t, h)))
        for part in (1, 2):
            off = part * DIL_HEADS
            in_specs += [
                pl.BlockSpec((1, DIL_HALF, hd), lambda b, h, t, off=off: (b, jnp.maximum(t * nhb - 1, 0), off + h)),
                pl.BlockSpec((1, seg, hd), lambda b, h, t, off=off: (b, t, off + h)),
                pl.BlockSpec((1, DIL_HALF, hd), lambda b, h, t, off=off: (b, jnp.minimum((t + 1) * nhb, last_hb), off + h)),
            ]
        args += [qkvs[g]] * 7
    kern = functools.partial(_dil_kernel, seq=S, seg=seg, tq=tq)
    return pl.pallas_call(
        kern,
        grid=(nb, DIL_HEADS, DIL_SEGS),
        in_specs=in_specs,
        out_specs=pl.BlockSpec((1, S, hd), lambda b, h, t: (b, 0, h)),
        out_shape=jax.ShapeDtypeStruct((nb, S, DIL_HEADS * hd), BF16),
        scratch_shapes=[pltpu.VMEM((len(DIL_CONFIGS), 4, tq, win), F32),
                        pltpu.VMEM((S, hd), F32), pltpu.VMEM((S, hd), F32), pltpu.VMEM((S, hd), F32),
                        pltpu.VMEM((len(DIL_CONFIGS), seg + 2 * DIL_HALF, hd), BF16),
                        pltpu.VMEM((len(DIL_CONFIGS), seg + 2 * DIL_HALF, hd), BF16)],
        compiler_params=_cparams("parallel", "parallel", "arbitrary"),
        name="dil_attn",
    )(*args)


def _ssd_lane_layout(v):
    v = v.reshape(2, SSD_GROUPS, SSD_HEADS_PER_GROUP).transpose(1, 0, 2)
    v = jnp.pad(v, ((0, 0), (0, 0), (0, GROUP_LANES // 2 - SSD_HEADS_PER_GROUP)))
    return v.reshape(1, LANES)


def _ssd_mixer(x2, h3, kmem, vmem_, w_in, conv_w, conv_b, dt_bias, a_log, d_skip, norm_g, w_out, mem_q_gain):
    nb, S, D = h3.shape
    inner = SSD_GROUPS * SSD_HEADS_PER_GROUP * SSD_HEAD_DIM
    conv_ch = inner + 2 * SSD_GROUPS * SSD_STATE
    nheads = SSD_GROUPS * SSD_HEADS_PER_GROUP
    mw = kmem.shape[2]
    tm = _tile(S, ROW_TILE)
    w_bf = w_in.astype(BF16)

    def grouped(t, z=None):
        lead = t.shape[:-1]
        parts = [t[..., :inner].reshape(*lead, SSD_GROUPS, SSD_GROUP_WIDTH),
                 t[..., inner:inner + SSD_GROUPS * SSD_STATE].reshape(*lead, SSD_GROUPS, SSD_STATE),
                 t[..., inner + SSD_GROUPS * SSD_STATE:].reshape(*lead, SSD_GROUPS, SSD_STATE)]
        if z is not None:
            parts.append(z.reshape(*lead, SSD_GROUPS, SSD_GROUP_WIDTH))
        return jnp.concatenate(parts, axis=-1).reshape(*lead, -1)

    w_p1 = grouped(w_bf[:, inner:inner + conv_ch], z=w_bf[:, :inner])
    xc = ssd_inproj(h3, w_p1, grouped(conv_w), grouped(conv_b), tm)
    w_dt = w_in[:, inner + conv_ch:inner + conv_ch + 2 * nheads]
    w_dt = w_dt.reshape(D, 2, SSD_GROUPS, SSD_HEADS_PER_GROUP).transpose(0, 2, 1, 3)
    w_dt = jnp.pad(w_dt, ((0, 0), (0, 0), (0, 0), (0, GROUP_LANES // 2 - SSD_HEADS_PER_GROUP)))
    q_off = inner + conv_ch + 2 * nheads
    w_side = jnp.concatenate([w_bf[:, q_off:], w_dt.reshape(D, LANES).astype(BF16)], axis=1)
    q_mem, dtr = side_proj(h3, w_side, mem_q_gain, tm)

    tr = ssd_prep(dtr, _ssd_lane_layout(dt_bias), _ssd_lane_layout(a_log))
    dskip = jnp.repeat(d_skip, SSD_HEAD_DIM)
    nchunks = S // SSD_CHUNK
    kc = next(k for k in (8, 4, 2, 1) if nchunks % (2 * k) == 0)
    y = ssd_scan(xc, tr, dskip, norm_g, kc)
    o_mem = mem_attention(q_mem, kmem, vmem_, tq=tm)
    w_out_bf = w_out.astype(BF16)
    return outproj(x2, y.reshape(nb * S, inner), o_mem.reshape(nb * S, mw),
                   w_out_bf[:inner], w_out_bf[inner:], tm=_tile(nb * S, ROW_TILE), tn=_tile(D, OUT_TILE))


def _dil_mixer(x2, h3, kmem, vmem_, w_in, q_gain, k_gain, w_out, rel_bias, mem_q_gain):
    nb, S, D = h3.shape
    width = DIL_HEADS * DIL_HEAD_DIM
    mw = kmem.shape[2]
    tm = _tile(S, ROW_TILE)
    w_bf = w_in.astype(BF16)
    qkvs = []
    for g, (_, r) in enumerate(DIL_CONFIGS):
        gains = jnp.stack([jnp.tile(q_gain[g], DIL_HEADS), jnp.tile(k_gain[g], DIL_HEADS),
                           jnp.ones((width,), F32)]).reshape(3, 1, width)
        out = proj(h3, w_bf, g * 3 * width, 3 * width, gains, out_dtype=BF16, tm=tm, tn=width,
                   norm_width=DIL_HEAD_DIM, norm_tiles=2, r=r)
        qkvs.append(out.reshape(nb, S, 3 * width))
    q_off = len(DIL_CONFIGS) * 3 * width
    hd = mem_q_gain.shape[0]
    qg = jnp.tile(mem_q_gain, mw // hd).reshape(1, 1, mw)
    q_mem = proj(h3, w_bf, q_off, mw, qg, out_dtype=BF16, tm=tm, tn=mw, norm_width=hd)[:, 0]
    o = dilated_attention(qkvs, rel_bias)
    o_mem = mem_attention(q_mem, kmem, vmem_, tq=tm)
    w_out_bf = w_out.astype(BF16)
    return outproj(x2, o.reshape(nb * S, width), o_mem.reshape(nb * S, mw),
                   w_out_bf[:width], w_out_bf[width:], tm=_tile(nb * S, ROW_TILE), tn=_tile(D, OUT_TILE))


def kernel(x, mem, rel_bias, ffn_norm, ffn_w_in, ffn_w_out, mix_norm, mem_norm, mem_w_kv, mem_q_gain, mem_k_gain, ssd_w_in, ssd_conv_w, ssd_conv_b, ssd_dt_bias, ssd_A_log, ssd_D, ssd_norm, ssd_w_out, dil_w_in, dil_q_gain, dil_k_gain, dil_w_out):
    nb, S, D = x.shape
    depth = ffn_norm.shape[0]
    T = nb * S
    x2 = x.reshape(T, D)
    tm_ffn, tf = _tile(T, ROW_TILE), _tile(ffn_w_out.shape[2], FF_TILE)
    ffn_w_in = ffn_w_in.astype(BF16)
    ffn_w_out = ffn_w_out.astype(BF16)
    for i in range(depth):
        x2, h2 = ffn(x2, ffn_norm[i, 0], ffn_w_in, ffn_w_out, i, 0, tm_ffn, tf, next_gain=mix_norm[i])
        h3 = h2.reshape(nb, S, D)
        kmem, vmem_ = mem_kv(mem, mem_norm[i], mem_w_kv[i].astype(BF16), mem_k_gain[i])
        j = i // 2
        if i % 2 == 0:
            x2 = _ssd_mixer(x2, h3, kmem, vmem_, ssd_w_in[j], ssd_conv_w[j], ssd_conv_b[j], ssd_dt_bias[j],
                            ssd_A_log[j], ssd_D[j], ssd_norm[j], ssd_w_out[j], mem_q_gain[i])
        else:
            x2 = _dil_mixer(x2, h3, kmem, vmem_, dil_w_in[j], dil_q_gain[j], dil_k_gain[j], dil_w_out[j],
                            rel_bias, mem_q_gain[i])
        x2 = ffn(x2, ffn_norm[i, 1], ffn_w_in, ffn_w_out, i, 1, tm_ffn, tf)
    return x2.reshape(nb, S, D)
```
